```python
import numpy as np
import jax
import jax.numpy as jnp
from jax import lax

D_MODEL = 1024
BATCH = 32
SEQ = 2048
DEPTH = 2

HEAD_DIM = 64
N_HEADS_TOTAL = D_MODEL // HEAD_DIM
NSA_HEADS = N_HEADS_TOTAL // 4
FOX_HEADS = (N_HEADS_TOTAL - NSA_HEADS) // 2
SB_HEADS = N_HEADS_TOTAL - NSA_HEADS - FOX_HEADS
NSA_KV_HEADS = 1
NSA_GROUP = NSA_HEADS // NSA_KV_HEADS
FOX_W = FOX_HEADS * HEAD_DIM
SB_W = SB_HEADS * HEAD_DIM
NSA_W = NSA_HEADS * HEAD_DIM
NSA_KV_W = NSA_KV_HEADS * HEAD_DIM
D_MIX = FOX_W + SB_W + NSA_W
CMP_BLOCK = 32
CMP_STRIDE = 16
SEL_BLOCK = 64
SEL_TOPK = 8
SEL_N_LOCAL = 2
WINDOW = 512
Q_BLOCK = 128
NORM_EPS = 1e-6
SPLIT_SIZES = (FOX_W, FOX_W, FOX_W, FOX_HEADS, FOX_W,
               SB_W, SB_W, SB_W, SB_W,
               NSA_W, NSA_KV_W, NSA_KV_W, NSA_KV_W, NSA_KV_W, NSA_KV_W, NSA_KV_W,
               3 * NSA_HEADS, NSA_W)
D_IN = sum(SPLIT_SIZES)

kernel_name = 'hybrid_fox_stickbreak_nsa_block'


def rms_norm(x, g):
    xf = x.astype(jnp.float32)
    ms = jnp.mean(xf * xf, axis=-1, keepdims=True)
    return (xf * lax.rsqrt(ms + NORM_EPS) * g.astype(jnp.float32)).astype(x.dtype)


def masked_softmax(logits, mask):
    logits = jnp.where(mask, logits, -jnp.inf)
    m = jnp.max(logits, axis=-1, keepdims=True)
    m = jnp.where(jnp.isfinite(m), m, 0.0)
    p = jnp.exp(logits - m)
    s = jnp.sum(p, axis=-1, keepdims=True)
    return p / jnp.where(s > 0, s, 1.0)


def alibi_slopes(n):
    return 2.0 ** (-8.0 * jnp.arange(1, n + 1, dtype=jnp.float32) / n)


def fox_attention(q, k, v, f_logit):
    B, S, H, dh = q.shape
    scale = dh ** -0.5
    c = jnp.cumsum(jax.nn.log_sigmoid(f_logit.astype(jnp.float32)), axis=1).transpose(0, 2, 1)
    kpos = jnp.arange(S)

    def block(i):
        qs = i * Q_BLOCK
        qb = lax.dynamic_slice_in_dim(q, qs, Q_BLOCK, axis=1)
        cq = lax.dynamic_slice_in_dim(c, qs, Q_BLOCK, axis=2)
        tpos = qs + jnp.arange(Q_BLOCK)
        s = (jnp.einsum('bqhd,bkhd->bhqk', qb, k).astype(jnp.float32) * scale
             + cq[..., None] - c[:, :, None, :])
        p = masked_softmax(s, kpos[None, :] <= tpos[:, None])
        return jnp.einsum('bhqk,bkhd->bqhd', p.astype(v.dtype), v)

    out = lax.map(block, jnp.arange(S // Q_BLOCK))
    return out.transpose(1, 0, 2, 3, 4).reshape(B, S, H, dh)


def stick_breaking_attention(q, k, v):
    B, S, H, dh = q.shape
    scale = dh ** -0.5
    kpos = jnp.arange(S)

    def block(i):
        qs = i * Q_BLOCK
        qb = lax.dynamic_slice_in_dim(q, qs, Q_BLOCK, axis=1)
        tpos = qs + jnp.arange(Q_BLOCK)
        z = jnp.einsum('bqhd,bkhd->bhqk', qb, k).astype(jnp.float32) * scale
        mask = kpos[None, :] < tpos[:, None]
        log_1m = jnp.where(mask, jax.nn.log_sigmoid(-z), 0.0)
        between = lax.cumsum(log_1m, axis=3, reverse=True) - log_1m
        a = jnp.where(mask, jnp.exp(jax.nn.log_sigmoid(z) + between), 0.0)
        return jnp.einsum('bhqk,bkhd->bqhd', a.astype(v.dtype), v)

    out = lax.map(block, jnp.arange(S // Q_BLOCK))
    return out.transpose(1, 0, 2, 3, 4).reshape(B, S, H, dh)


def compress_blocks(kv, pos_emb, w1, w2, idx):
    B = kv.shape[0]
    n_cmp = idx.shape[0]
    blk = kv[:, idx] + pos_emb[None, None, :, None, :]
    blk = blk.transpose(0, 1, 3, 2, 4).reshape(B, n_cmp, NSA_KV_HEADS, CMP_BLOCK * HEAD_DIM)
    hid = jax.nn.silu(jnp.einsum('bnhf,fe->bnhe', blk, w1))
    return jnp.einsum('bnhe,ed->bnhd', hid, w2)


def nsa_attention(q, k_cmp, v_cmp, k_slc, v_slc, k_win, v_win, gates,
                  pos_k, w1_k, w2_k, pos_v, w1_v, w2_v, slopes):
    B, S = q.shape[0], q.shape[1]
    Hkv, G, dh = NSA_KV_HEADS, NSA_GROUP, HEAD_DIM
    scale = dh ** -0.5
    dt = q.dtype
    n_cmp = (S - CMP_BLOCK) // CMP_STRIDE + 1
    n_sel = S // SEL_BLOCK
    k_top = min(SEL_TOPK, n_sel)
    idx = np.arange(n_cmp)[:, None] * CMP_STRIDE + np.arange(CMP_BLOCK)[None, :]
    cmp_end = jnp.asarray(idx[:, -1])
    cs, ce = idx[:, 0], idx[:, -1]
    ss = np.arange(n_sel) * SEL_BLOCK
    se = ss + SEL_BLOCK - 1
    cmp_to_sel = jnp.asarray(((cs[:, None] <= se[None, :]) & (ce[:, None] >= ss[None, :])).astype(np.float32))
    kc = compress_blocks(k_cmp, pos_k, w1_k, w2_k, idx)
    vc = compress_blocks(v_cmp, pos_v, w1_v, w2_v, idx)
    ksb = k_slc.reshape(B, n_sel, SEL_BLOCK, Hkv, dh).transpose(0, 3, 1, 2, 4)
    vsb = v_slc.reshape(B, n_sel, SEL_BLOCK, Hkv, dh).transpose(0, 3, 1, 2, 4)
    pad = ((0, 0), (WINDOW, 0), (0, 0), (0, 0))
    kwp = jnp.pad(k_win, pad)
    vwp = jnp.pad(v_win, pad)
    bi = jnp.arange(B)[:, None, None, None]
    hi = jnp.arange(Hkv)[None, :, None, None]
    blk_id = jnp.arange(n_sel)
    sl = slopes[:, :, None, None]

    def block(i):
        qs = i * Q_BLOCK
        qb = lax.dynamic_slice_in_dim(q, qs, Q_BLOCK, axis=1)
        tpos = qs + jnp.arange(Q_BLOCK)
        dist_c = (tpos[:, None] - cmp_end[None, :]).astype(jnp.float32)
        s_c = jnp.einsum('bqhgd,bnhd->bhgqn', qb, kc).astype(jnp.float32) * scale - sl * dist_c
        p_c = masked_softmax(s_c, dist_c >= 0)
        o_c = jnp.einsum('bhgqn,bnhd->bqhgd', p_c.astype(dt), vc)
        imp = jnp.einsum('bhgqn,nj->bhqj', p_c, cmp_to_sel)
        cur = tpos // SEL_BLOCK
        back = cur[:, None] - blk_id[None, :]
        valid = back >= 0
        forced = (blk_id[None, :] == 0) | (valid & (back < SEL_N_LOCAL))
        imp = jnp.where(forced, jnp.inf, jnp.where(valid, imp, -jnp.inf))
        _, sel = lax.top_k(imp, k_top)
        kg = ksb[bi, hi, sel].reshape(B, Hkv, Q_BLOCK, k_top * SEL_BLOCK, dh)
        vg = vsb[bi, hi, sel].reshape(B, Hkv, Q_BLOCK, k_top * SEL_BLOCK, dh)
        kp = (sel[..., None] * SEL_BLOCK + jnp.arange(SEL_BLOCK)).reshape(B, Hkv, Q_BLOCK, k_top * SEL_BLOCK)
        dist_s = (tpos[None, None, :, None] - kp).astype(jnp.float32)[:, :, None]
        qh = qb.transpose(0, 2, 3, 1, 4)
        s_s = (jnp.einsum('bhgqd,bhqkd->bhgqk', qh, kg).astype(jnp.float32) * scale
               - slopes[None, :, :, None, None] * dist_s)
        p_s = masked_softmax(s_s, dist_s >= 0)
        o_s = jnp.einsum('bhgqk,bhqkd->bqhgd', p_s.astype(dt), vg)
        kw = lax.dynamic_slice_in_dim(kwp, qs, WINDOW + Q_BLOCK, axis=1)
        vw = lax.dynamic_slice_in_dim(vwp, qs, WINDOW + Q_BLOCK, axis=1)
        wpos = qs - WINDOW + jnp.arange(WINDOW + Q_BLOCK)
        dist_w = tpos[:, None] - wpos[None, :]
        mask_w = (wpos[None, :] >= 0) & (dist_w >= 0) & (dist_w < WINDOW)
        s_w = (jnp.einsum('bqhgd,bkhd->bhgqk', qb, kw).astype(jnp.float32) * scale
               - sl * dist_w.astype(jnp.float32))
        p_w = masked_softmax(s_w, mask_w)
        o_w = jnp.einsum('bhgqk,bkhd->bqhgd', p_w.astype(dt), vw)
        g = lax.dynamic_slice_in_dim(gates, qs, Q_BLOCK, axis=1).astype(dt)
        return o_c * g[..., 0:1] + o_s * g[..., 1:2] + o_w * g[..., 2:3]

    out = lax.map(block, jnp.arange(S // Q_BLOCK))
    return out.transpose(1, 0, 2, 3, 4, 5).reshape(B, S, NSA_W)


def setup_inputs(seed: int = 0) -> dict:
    key = jax.random.key(seed)
    ks = jax.random.split(key, 14)
    f32 = jnp.float32
    lf = CMP_BLOCK * HEAD_DIM
    x = jax.random.normal(ks[0], (BATCH, SEQ, D_MODEL), f32)
    norm_g = 1.0 + 0.01 * jax.random.normal(ks[1], (DEPTH, D_MODEL), f32)
    w_in = jax.random.normal(ks[2], (DEPTH, D_MODEL, D_IN), f32) * D_MODEL ** -0.5
    b_f = 2.0 + 0.1 * jax.random.normal(ks[3], (DEPTH, FOX_HEADS), f32)
    cmp_pos_k = 0.02 * jax.random.normal(ks[4], (DEPTH, CMP_BLOCK, HEAD_DIM), f32)
    cmp_w1_k = jax.random.normal(ks[5], (DEPTH, lf, HEAD_DIM), f32) * lf ** -0.5
    cmp_w2_k = jax.random.normal(ks[6], (DEPTH, HEAD_DIM, HEAD_DIM), f32) * HEAD_DIM ** -0.5
    cmp_pos_v = 0.02 * jax.random.normal(ks[7], (DEPTH, CMP_BLOCK, HEAD_DIM), f32)
    cmp_w1_v = jax.random.normal(ks[8], (DEPTH, lf, HEAD_DIM), f32) * lf ** -0.5
    cmp_w2_v = jax.random.normal(ks[9], (DEPTH, HEAD_DIM, HEAD_DIM), f32) * HEAD_DIM ** -0.5
    w_out = jax.random.normal(ks[10], (DEPTH, D_MIX, D_MODEL), f32) * D_MIX ** -0.5
    final_g = 1.0 + 0.01 * jax.random.normal(ks[11], (D_MODEL,), f32)
    return {'x': x, 'norm_g': norm_g, 'w_in': w_in, 'b_f': b_f,
            'cmp_pos_k': cmp_pos_k, 'cmp_w1_k': cmp_w1_k, 'cmp_w2_k': cmp_w2_k,
            'cmp_pos_v': cmp_pos_v, 'cmp_w1_v': cmp_w1_v, 'cmp_w2_v': cmp_w2_v,
            'w_out': w_out, 'final_g': final_g}


def reference(x, norm_g, w_in, b_f, cmp_pos_k, cmp_w1_k, cmp_w2_k,
              cmp_pos_v, cmp_w1_v, cmp_w2_v, w_out, final_g):
    B, S, _ = x.shape
    offsets = np.cumsum(SPLIT_SIZES)[:-1].tolist()
    slopes = alibi_slopes(NSA_HEADS).reshape(NSA_KV_HEADS, NSA_GROUP)

    def heads(t, n):
        return t.reshape(B, S, n, HEAD_DIM)

    def kv_heads(t):
        return t.reshape(B, S, NSA_KV_HEADS, HEAD_DIM)

    for l in range(DEPTH):
        h = rms_norm(x, norm_g[l])
        proj = jnp.einsum('bsd,de->bse', h, w_in[l])
        (fq, fk, fv, ff, fz, sq, sk, sv, sz,
         nq, nkc, nvc, nks, nvs, nkw, nvw, ng, nz) = jnp.split(proj, offsets, axis=-1)
        o_fox = fox_attention(heads(fq, FOX_HEADS), heads(fk, FOX_HEADS), heads(fv, FOX_HEADS),
                              ff + b_f[l]).reshape(B, S, FOX_W) * jax.nn.silu(fz)
        o_sb = stick_breaking_attention(heads(sq, SB_HEADS), heads(sk, SB_HEADS),
                                        heads(sv, SB_HEADS)).reshape(B, S, SB_W) * jax.nn.silu(sz)
        gates = jax.nn.sigmoid(ng.astype(jnp.float32)).reshape(B, S, NSA_KV_HEADS, NSA_GROUP, 3)
        o_nsa = nsa_attention(nq.reshape(B, S, NSA_KV_HEADS, NSA_GROUP, HEAD_DIM),
                              kv_heads(nkc), kv_heads(nvc), kv_heads(nks), kv_heads(nvs),
                              kv_heads(nkw), kv_heads(nvw), gates,
                              cmp_pos_k[l], cmp_w1_k[l], cmp_w2_k[l],
                              cmp_pos_v[l], cmp_w1_v[l], cmp_w2_v[l], slopes) * jax.nn.silu(nz)
        mixed = jnp.concatenate([o_fox, o_sb, o_nsa], axis=-1)
        x = x + jnp.einsum('bse,ed->bsd', mixed, w_out[l])
    return rms_norm(x, final_g)
```

```python
import functools

import numpy as np
import jax
import jax.numpy as jnp
from jax import lax
from jax.experimental import pallas as pl
from jax.experimental.pallas import tpu as pltpu

F32 = jnp.float32
BF16 = jnp.bfloat16

D_MODEL = 1024
DEPTH = 2
HEAD_DIM = 64
LANES = 128
FOX_HEADS = 6
SB_HEADS = 6
NSA_HEADS = 4
FOX_W = FOX_HEADS * HEAD_DIM
SB_W = SB_HEADS * HEAD_DIM
NSA_W = NSA_HEADS * HEAD_DIM
CMP_BLOCK = 32
CMP_STRIDE = 16
SEL_BLOCK = 64
SEL_TOPK = 8
SEL_N_LOCAL = 2
WINDOW = 512
NORM_EPS = 1e-6
QK_SCALE = HEAD_DIM ** -0.5
NEG_BIG = -1e30
PEN = -(2.0 ** 100)

PB_FQ, PB_FK, PB_FV, PB_SQ, PB_SK, PB_SV, PB_NQ, PB_NKS, PB_NVS, PB_NKW, PB_NVW = (
    0, 3, 6, 9, 12, 15, 18, 20, 21, 22, 23)
PB_BLOCKS = 24
PF_FZ, PF_SZ, PF_NZ, PF_KVC, PF_MISC = 0, 3, 6, 8, 9
PF_BLOCKS = 10
NB_COLS = PB_BLOCKS * LANES
NF_COLS = PF_BLOCKS * LANES
MISC_NG = FOX_HEADS

VMEM_LIMIT = 56 * 1024 * 1024

_NT = (((1,), (1,)), ((), ()))


def _dot(a, b):
    return jnp.dot(a, b, preferred_element_type=F32)


def _dot_nt(a, b):
    return lax.dot_general(a, b, _NT, preferred_element_type=F32)


def _sigmoid(x):
    return 1.0 / (1.0 + jnp.exp(-x))


def _log_sigmoid(x):
    return -(jnp.maximum(-x, 0.0) + jnp.log1p(jnp.exp(-jnp.abs(x))))


def _split3(x):
    hi = x.astype(BF16)
    r = x - hi.astype(F32)
    mid = r.astype(BF16)
    lo = (r - mid.astype(F32)).astype(BF16)
    return hi, mid, lo


def _split2(x):
    hi = x.astype(BF16)
    lo = (x - hi.astype(F32)).astype(BF16)
    return hi, lo


def _rep(x, n):
    return x if n == 1 else jnp.concatenate([x] * n, axis=1)


def _inproj_kernel(x_ref, g_ref, w_ref, pb_ref, pf_ref):
    x = x_ref[...]
    ms = jnp.mean(x * x, axis=-1, keepdims=True)
    h = (x * lax.rsqrt(ms + NORM_EPS) * g_ref[...]).astype(BF16)
    cb = 512
    for c in range(0, NB_COLS, cb):
        pb_ref[:, c:c + cb] = _dot(h, w_ref[:, c:c + cb]).astype(BF16)
    cf = 640
    for c in range(0, NF_COLS, cf):
        pf_ref[:, c:c + cf] = _dot(h, w_ref[:, NB_COLS + c:NB_COLS + c + cf])


def _inproj(xf, g, w_all, tm=512):
    m, d = xf.shape
    return pl.pallas_call(
        _inproj_kernel,
        grid=(m // tm,),
        in_specs=[
            pl.BlockSpec((tm, d), lambda i: (i, 0)),
            pl.BlockSpec((1, d), lambda i: (0, 0)),
            pl.BlockSpec((d, NB_COLS + NF_COLS), lambda i: (0, 0)),
        ],
        out_specs=[
            pl.BlockSpec((tm, NB_COLS), lambda i: (i, 0)),
            pl.BlockSpec((tm, NF_COLS), lambda i: (i, 0)),
        ],
        out_shape=[
            jax.ShapeDtypeStruct((m, NB_COLS), BF16),
            jax.ShapeDtypeStruct((m, NF_COLS), F32),
        ],
        compiler_params=pltpu.CompilerParams(
            dimension_semantics=("arbitrary",), vmem_limit_bytes=VMEM_LIMIT),
        name="inproj",
    )(xf, g.reshape(1, d), w_all)


def _pack_w_in(w):
    sizes = (FOX_W, FOX_W, FOX_W, FOX_HEADS, FOX_W, SB_W, SB_W, SB_W, SB_W,
             NSA_W, HEAD_DIM, HEAD_DIM, HEAD_DIM, HEAD_DIM, HEAD_DIM, HEAD_DIM,
             3 * NSA_HEADS, NSA_W)
    offs = np.concatenate([[0], np.cumsum(sizes)])
    (fq, fk, fv, ff, fz, sq, sk, sv, sz, nq, nkc, nvc, nks, nvs, nkw, nvw, ng, nz) = [
        w[:, offs[i]:offs[i + 1]] for i in range(len(sizes))]
    pad = jnp.zeros((w.shape[0], LANES - FOX_HEADS - 3 * NSA_HEADS), w.dtype)
    cols = [fq * QK_SCALE, fk, fv, sq * QK_SCALE, sk, sv, nq * QK_SCALE,
            nks, nks, nvs, nvs, nkw, nkw, nvw, nvw,
            fz, sz, nz, nkc, nvc, ff, ng, pad]
    return jnp.concatenate(cols, axis=1).astype(BF16)


def _outproj_kernel(of_ref, os_ref, on_ref, x_ref, w_ref, g_ref, o_ref, *, final):
    y = (x_ref[...]
         + _dot(of_ref[...], w_ref[0:FOX_W, :])
         + _dot(os_ref[...], w_ref[FOX_W:FOX_W + SB_W, :])
         + _dot(on_ref[...], w_ref[FOX_W + SB_W:, :]))
    if final:
        ms = jnp.mean(y * y, axis=-1, keepdims=True)
        y = y * lax.rsqrt(ms + NORM_EPS) * g_ref[...]
    o_ref[...] = y


def _outproj(o_fox, o_sb, o_nsa, xf, w, g, final, tm=512):
    m, d = xf.shape
    return pl.pallas_call(
        functools.partial(_outproj_kernel, final=final),
        grid=(m // tm,),
        in_specs=[
            pl.BlockSpec((tm, FOX_W), lambda i: (i, 0)),
            pl.BlockSpec((tm, SB_W), lambda i: (i, 0)),
            pl.BlockSpec((tm, NSA_W), lambda i: (i, 0)),
            pl.BlockSpec((tm, d), lambda i: (i, 0)),
            pl.BlockSpec((d, d), lambda i: (0, 0)),
            pl.BlockSpec((1, d), lambda i: (0, 0)),
        ],
        out_specs=pl.BlockSpec((tm, d), lambda i: (i, 0)),
        out_shape=jax.ShapeDtypeStruct((m, d), F32),
        compiler_params=pltpu.CompilerParams(
            dimension_semantics=("arbitrary",), vmem_limit_bytes=VMEM_LIMIT),
        name="outproj_final" if final else "outproj",
    )(o_fox, o_sb, o_nsa, xf, w, g.reshape(1, d))


def _fox_kernel(q_ref, k_ref, v_ref, z_ref, misc_ref, bf_ref, tri_ref, o_ref,
                ccol_ref, crow_ref, m_ref, l_ref, acc_ref, *, seq, blk):
    pair = pl.program_id(1)
    lane1 = lax.broadcasted_iota(jnp.int32, (1, LANES), 1)

    tri = tri_ref[...]
    carry = jnp.zeros((1, LANES), F32)
    for b in range(seq // blk):
        rows = slice(b * blk, (b + 1) * blk)
        ls = _log_sigmoid(misc_ref[rows, :] + bf_ref[...])
        xa = jnp.sum(jnp.where(lane1 == 2 * pair, ls, 0.0), axis=1, keepdims=True)
        xb = jnp.sum(jnp.where(lane1 == 2 * pair + 1, ls, 0.0), axis=1, keepdims=True)
        xs = jnp.where(lane1 == 0, xa, jnp.where(lane1 == 1, xb, 0.0))
        hi, mid, lo = _split3(xs)
        cb = _dot(tri, hi) + _dot(tri, mid) + _dot(tri, lo) + carry
        carry = cb[blk - 1:blk, :]
        ccol_ref[rows, :] = cb
        crow_ref[:, rows] = cb.T[0:8, :]

    low = lax.broadcasted_iota(jnp.int32, (blk, LANES), 1) < HEAD_DIM
    causal = (lax.broadcasted_iota(jnp.int32, (blk, blk), 1)
              <= lax.broadcasted_iota(jnp.int32, (blk, blk), 0))
    nrep = blk // LANES

    def q_block(i, _):
        qs = pl.multiple_of(i * blk, blk)
        q2 = q_ref[pl.ds(qs, blk), :]
        zero = jnp.zeros_like(q2)
        qh = (jnp.where(low, q2, zero), jnp.where(low, zero, q2))
        cq = (ccol_ref[pl.ds(qs, blk), 0:1], ccol_ref[pl.ds(qs, blk), 1:2])
        m_ref[...] = jnp.full(m_ref.shape, NEG_BIG, F32)
        l_ref[...] = jnp.zeros(l_ref.shape, F32)
        acc_ref[...] = jnp.zeros(acc_ref.shape, F32)

        def k_step(ks, masked):
            kb = k_ref[pl.ds(ks, blk), :]
            vb = v_ref[pl.ds(ks, blk), :]
            for h in range(2):
                s = _dot_nt(qh[h], kb) + (cq[h] - crow_ref[h:h + 1, pl.ds(ks, blk)])
                if masked:
                    s = jnp.where(causal, s, -jnp.inf)
                m_prev = m_ref[h]
                m_next = jnp.maximum(m_prev, jnp.max(s, axis=1, keepdims=True))
                p = jnp.exp(s - _rep(m_next, nrep))
                alpha = jnp.exp(m_prev - m_next)
                l_ref[h] = alpha * l_ref[h] + jnp.sum(p, axis=1, keepdims=True)
                acc_ref[h] = alpha * acc_ref[h] + _dot(p.astype(BF16), vb)
                m_ref[h] = m_next

        def k_loop(j, _):
            k_step(pl.multiple_of(j * blk, blk), False)
            return 0

        lax.fori_loop(0, i, k_loop, 0)
        k_step(qs, True)

        o = jnp.where(low, acc_ref[0] / l_ref[0], acc_ref[1] / l_ref[1])
        z = z_ref[pl.ds(qs, blk), :]
        o_ref[pl.ds(qs, blk), :] = (o * (z * _sigmoid(z))).astype(BF16)
        return 0

    lax.fori_loop(0, seq // blk, q_block, 0)


def _fox(pb3, pf3, b_f, blk=256):
    bsz, seq, _ = pb3.shape
    npair = FOX_HEADS // 2
    bias = jnp.zeros((1, LANES), F32).at[0, :FOX_HEADS].set(b_f)
    tri = jnp.asarray(np.tril(np.ones((blk, blk), np.float32)), BF16)

    def col(base):
        return pl.BlockSpec((None, seq, LANES), lambda b, p: (b, 0, base + p))

    return pl.pallas_call(
        functools.partial(_fox_kernel, seq=seq, blk=blk),
        grid=(bsz, npair),
        in_specs=[
            col(PB_FQ), col(PB_FK), col(PB_FV), col(PF_FZ),
            pl.BlockSpec((None, seq, LANES), lambda b, p: (b, 0, PF_MISC)),
            pl.BlockSpec((1, LANES), lambda b, p: (0, 0)),
            pl.BlockSpec((blk, blk), lambda b, p: (0, 0)),
        ],
        out_specs=pl.BlockSpec((None, seq, LANES), lambda b, p: (b, 0, p)),
        out_shape=jax.ShapeDtypeStruct((bsz, seq, FOX_W), BF16),
        scratch_shapes=[
            pltpu.VMEM((seq, LANES), F32),
            pltpu.VMEM((8, seq), F32),
            pltpu.VMEM((2, blk, LANES), F32),
            pltpu.VMEM((2, blk, LANES), F32),
            pltpu.VMEM((2, blk, LANES), F32),
        ],
        compiler_params=pltpu.CompilerParams(
            dimension_semantics=("arbitrary", "arbitrary"), vmem_limit_bytes=VMEM_LIMIT),
        name="fox_attn",
    )(pb3, pb3, pb3, pf3, pf3, bias, tri)


def _sb_kernel(q_ref, k_ref, v_ref, z_ref, suf_ref, o_ref, carry_ref, acc_ref, *, seq, blk):
    low = lax.broadcasted_iota(jnp.int32, (blk, LANES), 1) < HEAD_DIM
    strict = (lax.broadcasted_iota(jnp.int32, (blk, blk), 1)
              < lax.broadcasted_iota(jnp.int32, (blk, blk), 0))
    nrep = blk // LANES

    def q_block(i, _):
        qs = pl.multiple_of(i * blk, blk)
        q2 = q_ref[pl.ds(qs, blk), :]
        zero = jnp.zeros_like(q2)
        qh = (jnp.where(low, q2, zero), jnp.where(low, zero, q2))
        carry_ref[...] = jnp.zeros(carry_ref.shape, F32)
        acc_ref[...] = jnp.zeros(acc_ref.shape, F32)

        def k_step(ks, masked):
            kb = k_ref[pl.ds(ks, blk), :]
            vb = v_ref[pl.ds(ks, blk), :]
            suf = suf_ref[...]
            for h in range(2):
                z = _dot_nt(qh[h], kb)
                lsz = jnp.minimum(z, 0.0) - jnp.log1p(jnp.exp(-jnp.abs(z)))
                l1m = lsz - z
                if masked:
                    l1m = jnp.where(strict, l1m, 0.0)
                hi, lo = _split2(l1m)
                incl = _dot(hi, suf) + _dot(lo, suf)
                carry = carry_ref[h]
                a = jnp.exp(lsz + (_rep(carry, nrep) + incl - l1m))
                if masked:
                    a = jnp.where(strict, a, 0.0)
                acc_ref[h] = acc_ref[h] + _dot(a.astype(BF16), vb)
                carry_ref[h] = carry + incl[:, 0:1]

        k_step(qs, True)

        def k_loop(j, _):
            k_step(pl.multiple_of((i - 1 - j) * blk, blk), False)
            return 0

        lax.fori_loop(0, i, k_loop, 0)

        o = jnp.where(low, acc_ref[0], acc_ref[1])
        z = z_ref[pl.ds(qs, blk), :]
        o_ref[pl.ds(qs, blk), :] = (o * (z * _sigmoid(z))).astype(BF16)
        return 0

    lax.fori_loop(0, seq // blk, q_block, 0)


def _sb(pb3, pf3, blk=256):
    bsz, seq, _ = pb3.shape
    npair = SB_HEADS // 2
    suf = jnp.asarray(np.tril(np.ones((blk, blk), np.float32)), BF16)

    def col(base):
        return pl.BlockSpec((None, seq, LANES), lambda b, p: (b, 0, base + p))

    return pl.pallas_call(
        functools.partial(_sb_kernel, seq=seq, blk=blk),
        grid=(bsz, npair),
        in_specs=[
            col(PB_SQ), col(PB_SK), col(PB_SV), col(PF_SZ),
            pl.BlockSpec((blk, blk), lambda b, p: (0, 0)),
        ],
        out_specs=pl.BlockSpec((None, seq, LANES), lambda b, p: (b, 0, p)),
        out_shape=jax.ShapeDtypeStruct((bsz, seq, SB_W), BF16),
        scratch_shapes=[
            pltpu.VMEM((2, blk, LANES), F32),
            pltpu.VMEM((2, blk, LANES), F32),
        ],
        compiler_params=pltpu.CompilerParams(
            dimension_semantics=("arbitrary", "arbitrary"), vmem_limit_bytes=VMEM_LIMIT),
        name="sb_attn",
    )(pb3, pb3, pb3, pf3, suf)


def _nsa_kernel(q_ref, ks_ref, vs_ref, kw_ref, vw_ref, z_ref, misc_ref, xk_ref, xv_ref,
                posk_ref, w1k_ref, w2k_ref, posv_ref, w1v_ref, w2v_ref, msel_ref, eneg_ref,
                o_ref, kc_ref, vc_ref, pen_ref, m_ref, l_ref, acc_ref, osum_ref, *, seq, blk):
    g_heads = NSA_HEADS
    rows4 = g_heads * blk
    n_cmp = (seq - CMP_BLOCK) // CMP_STRIDE + 1

    def compress(x_ref, pos_ref, w1_ref, w2_ref):
        x = x_ref[...]
        a = _dot((x + pos_ref[0:1, :]).astype(BF16), w1_ref[0])
        b = _dot((x + pos_ref[1:2, :]).astype(BF16), w1_ref[1])
        hid = a + pltpu.roll(b, b.shape[0] - 1, axis=0)
        hid = hid * _sigmoid(hid)
        return _dot(hid.astype(BF16), w2_ref[...]).astype(BF16)

    kc_ref[...] = compress(xk_ref, posk_ref, w1k_ref, w2k_ref)
    vc_ref[...] = compress(xv_ref, posv_ref, w1v_ref, w2v_ref)

    low = lax.broadcasted_iota(jnp.int32, (blk, LANES), 1) < HEAD_DIM
    row4 = lax.broadcasted_iota(jnp.int32, (rows4, 1), 0)
    head = row4 // blk
    r4 = row4 - head * blk
    slope = jnp.where(head == 0, 2.0 ** -2, jnp.where(head == 1, 2.0 ** -4,
                      jnp.where(head == 2, 2.0 ** -6, 2.0 ** -8))).astype(F32)
    col_b = lax.broadcasted_iota(jnp.int32, (1, blk), 1)
    rel = (r4 - col_b).astype(F32)
    srel = slope * rel
    causal4 = col_b <= r4
    after4 = col_b > r4
    lane1 = lax.broadcasted_iota(jnp.int32, (1, LANES), 1)
    cmp_end = (lane1 * CMP_STRIDE + (CMP_BLOCK - 1)).astype(F32)
    rowq = lax.broadcasted_iota(jnp.int32, (blk, 1), 0)
    nrep = blk // LANES

    def reset():
        m_ref[...] = jnp.full(m_ref.shape, NEG_BIG, F32)
        l_ref[...] = jnp.zeros(l_ref.shape, F32)
        acc_ref[...] = jnp.zeros(acc_ref.shape, F32)

    def online(s, vb):
        m_prev = m_ref[...]
        m_next = jnp.maximum(m_prev, jnp.max(s, axis=1, keepdims=True))
        p = jnp.exp(s - _rep(m_next, nrep))
        alpha = jnp.exp(m_prev - m_next)
        l_ref[...] = alpha * l_ref[...] + jnp.sum(p, axis=1, keepdims=True)
        acc_ref[...] = alpha * acc_ref[...] + _dot(p.astype(BF16), vb)
        m_ref[...] = m_next

    def q_block(i, _):
        qs = pl.multiple_of(i * blk, blk)
        q01 = q_ref[pl.ds(qs, blk), 0:LANES]
        q23 = q_ref[pl.ds(qs, blk), LANES:2 * LANES]
        zero = jnp.zeros_like(q01)
        qst = jnp.concatenate([jnp.where(low, q01, zero), jnp.where(low, zero, q01),
                               jnp.where(low, q23, zero), jnp.where(low, zero, q23)], axis=0)
        tpos4 = (qs + r4).astype(F32)

        dist_c = tpos4 - cmp_end
        valid_c = (dist_c >= 0.0) & (lane1 < n_cmp)
        sc = _dot_nt(qst, kc_ref[...]) - slope * dist_c
        sc = jnp.where(valid_c, sc, -jnp.inf)
        mc = jnp.max(sc, axis=1, keepdims=True)
        mc = jnp.where(mc == -jnp.inf, 0.0, mc)
        pc = jnp.exp(sc - mc)
        ssum = jnp.sum(pc, axis=1, keepdims=True)
        pc = pc / jnp.where(ssum > 0.0, ssum, 1.0)
        osum_ref[...] = _dot(pc.astype(BF16), vc_ref[...])

        pcs = pc[0:blk] + pc[blk:2 * blk] + pc[2 * blk:3 * blk] + pc[3 * blk:4 * blk]
        hi, lo = _split2(pcs)
        imp = _dot(hi, msel_ref[...]) + _dot(lo, msel_ref[...])
        cur = (qs + rowq) // SEL_BLOCK
        back = cur - lane1
        valid_b = back >= 0
        forced = (lane1 == 0) | (valid_b & (back < SEL_N_LOCAL))
        imp = jnp.where(forced, jnp.inf, jnp.where(valid_b, imp, -jnp.inf))
        rank = jnp.zeros((blk, LANES), F32)
        for c in range(seq // SEL_BLOCK):
            colv = imp[:, c:c + 1]
            tie = jnp.where(lane1 > c, 1.0, 0.0)
            rank = rank + jnp.where(colv > imp, 1.0, jnp.where(colv == imp, tie, 0.0))
        unsel = jnp.where((rank >= float(SEL_TOPK)) & (lane1 < seq // SEL_BLOCK), 1.0, 0.0)
        pen_ref[...] = _dot(unsel.astype(BF16), eneg_ref[...])

        def gate(branch):
            sg = _sigmoid(misc_ref[pl.ds(qs, blk), :])
            return jnp.concatenate(
                [sg[:, MISC_NG + 3 * g + branch:MISC_NG + 3 * g + branch + 1]
                 for g in range(g_heads)], axis=0)

        osum_ref[...] = osum_ref[...] * gate(0)

        reset()

        def sel_step(ks, diag):
            off = (qs - ks).astype(F32)
            pen = pen_ref[:, pl.ds(ks, blk)]
            s = (_dot_nt(qst, ks_ref[pl.ds(ks, blk), :])
                 + jnp.concatenate([pen] * g_heads, axis=0)
                 - (srel + slope * off))
            if diag:
                s = jnp.where(causal4, s, -jnp.inf)
            online(s, vs_ref[pl.ds(ks, blk), :])

        def sel_loop(j, _):
            sel_step(pl.multiple_of(j * blk, blk), False)
            return 0

        lax.fori_loop(0, i, sel_loop, 0)
        sel_step(qs, True)
        osum_ref[...] = osum_ref[...] + (acc_ref[...] / l_ref[...]) * gate(1)

        reset()
        nwin = WINDOW // blk

        def win_step(ks, mode):
            off = (qs - ks).astype(F32)
            s = _dot_nt(qst, kw_ref[pl.ds(ks, blk), :]) - (srel + slope * off)
            if mode == "left":
                s = jnp.where(after4, s, -jnp.inf)
            elif mode == "diag":
                s = jnp.where(causal4, s, -jnp.inf)
            online(s, vw_ref[pl.ds(ks, blk), :])

        @pl.when(i >= nwin)
        def _():
            win_step(pl.multiple_of((i - nwin) * blk, blk), "left")

        def win_loop(j, _):
            win_step(pl.multiple_of(j * blk, blk), "mid")
            return 0

        lax.fori_loop(jnp.maximum(i - nwin + 1, 0), i, win_loop, 0)
        win_step(qs, "diag")
        osum = osum_ref[...] + (acc_ref[...] / l_ref[...]) * gate(2)

        z = z_ref[pl.ds(qs, blk), :]
        zz = z * _sigmoid(z)
        o01 = jnp.where(low, osum[0:blk], osum[blk:2 * blk])
        o23 = jnp.where(low, osum[2 * blk:3 * blk], osum[3 * blk:4 * blk])
        o_ref[pl.ds(qs, blk), 0:LANES] = (o01 * zz[:, 0:LANES]).astype(BF16)
        o_ref[pl.ds(qs, blk), LANES:2 * LANES] = (o23 * zz[:, LANES:2 * LANES]).astype(BF16)
        return 0

    lax.fori_loop(0, seq // blk, q_block, 0)


def _nsa_constants(seq):
    n_cmp = (seq - CMP_BLOCK) // CMP_STRIDE + 1
    n_sel = seq // SEL_BLOCK
    cs = np.arange(n_cmp) * CMP_STRIDE
    ce = cs + CMP_BLOCK - 1
    ss = np.arange(n_sel) * SEL_BLOCK
    se = ss + SEL_BLOCK - 1
    msel = np.zeros((LANES, LANES), np.float32)
    msel[:n_cmp, :n_sel] = (cs[:, None] <= se[None, :]) & (ce[:, None] >= ss[None, :])
    eneg = np.zeros((LANES, seq), np.float32)
    eneg[:n_sel, :] = np.where(np.arange(seq)[None, :] // SEL_BLOCK == np.arange(n_sel)[:, None],
                               PEN, 0.0)
    return jnp.asarray(msel, BF16), jnp.asarray(eneg, BF16)


def _nsa(pb3, pf3, pos_k, w1_k, w2_k, pos_v, w1_v, w2_v, blk=128):
    bsz, seq, _ = pb3.shape
    nchunk = seq // CMP_STRIDE
    cw = CMP_STRIDE * HEAD_DIM
    kvc = pf3[:, :, PF_KVC * LANES:(PF_KVC + 1) * LANES]
    xk = kvc[:, :, :HEAD_DIM].reshape(bsz, nchunk, cw)
    xv = kvc[:, :, HEAD_DIM:].reshape(bsz, nchunk, cw)

    def prep(pos, w1, w2):
        w1d = jnp.concatenate([w1, w1], axis=1).astype(BF16).reshape(2, cw, LANES)
        w2p = jnp.zeros((LANES, LANES), F32).at[:HEAD_DIM, :].set(
            jnp.concatenate([w2, w2], axis=1)).astype(BF16)
        return pos.reshape(2, cw), w1d, w2p

    msel, eneg = _nsa_constants(seq)

    def col(base, nblk=1):
        return pl.BlockSpec((None, seq, nblk * LANES), lambda b: (b, 0, base // nblk))

    def whole(shape):
        return pl.BlockSpec(shape, lambda b: (0,) * len(shape))

    rows4 = NSA_HEADS * blk
    return pl.pallas_call(
        functools.partial(_nsa_kernel, seq=seq, blk=blk),
        grid=(bsz,),
        in_specs=[
            col(PB_NQ, 2), col(PB_NKS), col(PB_NVS), col(PB_NKW), col(PB_NVW),
            col(PF_NZ, 2), col(PF_MISC),
            pl.BlockSpec((None, nchunk, cw), lambda b: (b, 0, 0)),
            pl.BlockSpec((None, nchunk, cw), lambda b: (b, 0, 0)),
            whole((2, cw)), whole((2, cw, LANES)), whole((LANES, LANES)),
            whole((2, cw)), whole((2, cw, LANES)), whole((LANES, LANES)),
            whole((LANES, LANES)), whole((LANES, seq)),
        ],
        out_specs=pl.BlockSpec((None, seq, NSA_W), lambda b: (b, 0, 0)),
        out_shape=jax.ShapeDtypeStruct((bsz, seq, NSA_W), BF16),
        scratch_shapes=[
            pltpu.VMEM((nchunk, LANES), BF16),
            pltpu.VMEM((nchunk, LANES), BF16),
            pltpu.VMEM((blk, seq), F32),
            pltpu.VMEM((rows4, LANES), F32),
            pltpu.VMEM((rows4, LANES), F32),
            pltpu.VMEM((rows4, LANES), F32),
            pltpu.VMEM((rows4, LANES), F32),
        ],
        compiler_params=pltpu.CompilerParams(
            dimension_semantics=("arbitrary",), vmem_limit_bytes=VMEM_LIMIT),
        name="nsa_attn",
    )(pb3, pb3, pb3, pb3, pb3, pf3, pf3, xk, xv,
      *prep(pos_k, w1_k, w2_k), *prep(pos_v, w1_v, w2_v), msel, eneg)


def kernel(x, norm_g, w_in, b_f, cmp_pos_k, cmp_w1_k, cmp_w2_k,
           cmp_pos_v, cmp_w1_v, cmp_w2_v, w_out, final_g):
    bsz, seq, d = x.shape
    xf = x.reshape(bsz * seq, d)
    for l in range(DEPTH):
        pb, pf = _inproj(xf, norm_g[l], _pack_w_in(w_in[l]))
        pb3 = pb.reshape(bsz, seq, NB_COLS)
        pf3 = pf.reshape(bsz, seq, NF_COLS)
        o_fox = _fox(pb3, pf3, b_f[l])
        o_sb = _sb(pb3, pf3)
        o_nsa = _nsa(pb3, pf3, cmp_pos_k[l], cmp_w1_k[l], cmp_w2_k[l],
                     cmp_pos_v[l], cmp_w1_v[l], cmp_w2_v[l])
        xf = _outproj(o_fox.reshape(bsz * seq, FOX_W), o_sb.reshape(bsz * seq, SB_W),
                      o_nsa.reshape(bsz * seq, NSA_W), xf, w_out[l].astype(BF16),
                      final_g, final=(l == DEPTH - 1))
    return xf.reshape(bsz, seq, d)
```

```python
import functools

import numpy as np
import jax
import jax.numpy as jnp
from jax import lax
from jax.experimental import pallas as pl
from jax.experimental.pallas import tpu as pltpu

F32 = jnp.float32
BF16 = jnp.bfloat16

D_MODEL = 1024
DEPTH = 2
HEAD_DIM = 64
LANES = 128
FOX_HEADS = 6
SB_HEADS = 6
NSA_HEADS = 4
FOX_W = FOX_HEADS * HEAD_DIM
SB_W = SB_HEADS * HEAD_DIM
NSA_W = NSA_HEADS * HEAD_DIM
CMP_BLOCK = 32
CMP_STRIDE = 16
SEL_BLOCK = 64
SEL_TOPK = 8
SEL_N_LOCAL = 2
WINDOW = 512
NORM_EPS = 1e-6
QK_SCALE = HEAD_DIM ** -0.5
NEG_BIG = -1e30
PEN = -(2.0 ** 100)

PB_FQ, PB_FK, PB_FV, PB_SQ, PB_SK, PB_SV, PB_NQ, PB_NKS, PB_NVS, PB_NKW, PB_NVW = (
    0, 3, 6, 9, 12, 15, 18, 20, 21, 22, 23)
PB_BLOCKS = 24
PF_FZ, PF_SZ, PF_NZ, PF_KVC, PF_MISC = 0, 3, 6, 8, 9
PF_BLOCKS = 10
NB_COLS = PB_BLOCKS * LANES
NF_COLS = PF_BLOCKS * LANES
MISC_NG = FOX_HEADS

VMEM_LIMIT = 56 * 1024 * 1024

_NT = (((1,), (1,)), ((), ()))


def _dot(a, b):
    return jnp.dot(a, b, preferred_element_type=F32)


def _dot_nt(a, b):
    return lax.dot_general(a, b, _NT, preferred_element_type=F32)


def _sigmoid(x):
    return 1.0 / (1.0 + jnp.exp(-x))


def _log_sigmoid(x):
    return -(jnp.maximum(-x, 0.0) + jnp.log1p(jnp.exp(-jnp.abs(x))))


def _split3(x):
    hi = x.astype(BF16)
    r = x - hi.astype(F32)
    mid = r.astype(BF16)
    lo = (r - mid.astype(F32)).astype(BF16)
    return hi, mid, lo


def _split2(x):
    hi = x.astype(BF16)
    lo = (x - hi.astype(F32)).astype(BF16)
    return hi, lo


def _rep(x, n):
    return x if n == 1 else jnp.concatenate([x] * n, axis=1)


def _inproj_kernel(x_ref, g_ref, w_ref, pb_ref, pf_ref):
    x = x_ref[...]
    ms = jnp.mean(x * x, axis=-1, keepdims=True)
    h = (x * lax.rsqrt(ms + NORM_EPS) * g_ref[...]).astype(BF16)
    cb = 512
    for c in range(0, NB_COLS, cb):
        pb_ref[:, c:c + cb] = _dot(h, w_ref[:, c:c + cb]).astype(BF16)
    cf = 640
    for c in range(0, NF_COLS, cf):
        pf_ref[:, c:c + cf] = _dot(h, w_ref[:, NB_COLS + c:NB_COLS + c + cf])


def _inproj(xf, g, w_all, tm=512):
    m, d = xf.shape
    return pl.pallas_call(
        _inproj_kernel,
        grid=(m // tm,),
        in_specs=[
            pl.BlockSpec((tm, d), lambda i: (i, 0)),
            pl.BlockSpec((1, d), lambda i: (0, 0)),
            pl.BlockSpec((d, NB_COLS + NF_COLS), lambda i: (0, 0)),
        ],
        out_specs=[
            pl.BlockSpec((tm, NB_COLS), lambda i: (i, 0)),
            pl.BlockSpec((tm, NF_COLS), lambda i: (i, 0)),
        ],
        out_shape=[
            jax.ShapeDtypeStruct((m, NB_COLS), BF16),
            jax.ShapeDtypeStruct((m, NF_COLS), F32),
        ],
        compiler_params=pltpu.CompilerParams(
            dimension_semantics=("arbitrary",), vmem_limit_bytes=VMEM_LIMIT),
        name="inproj",
    )(xf, g.reshape(1, d), w_all)


def _pack_w_in(w):
    sizes = (FOX_W, FOX_W, FOX_W, FOX_HEADS, FOX_W, SB_W, SB_W, SB_W, SB_W,
             NSA_W, HEAD_DIM, HEAD_DIM, HEAD_DIM, HEAD_DIM, HEAD_DIM, HEAD_DIM,
             3 * NSA_HEADS, NSA_W)
    offs = np.concatenate([[0], np.cumsum(sizes)])
    (fq, fk, fv, ff, fz, sq, sk, sv, sz, nq, nkc, nvc, nks, nvs, nkw, nvw, ng, nz) = [
        w[:, offs[i]:offs[i + 1]] for i in range(len(sizes))]
    pad = jnp.zeros((w.shape[0], LANES - FOX_HEADS - 3 * NSA_HEADS), w.dtype)
    cols = [fq * QK_SCALE, fk, fv, sq * QK_SCALE, sk, sv, nq * QK_SCALE,
            nks, nks, nvs, nvs, nkw, nkw, nvw, nvw,
            fz, sz, nz, nkc, nvc, ff, ng, pad]
    return jnp.concatenate(cols, axis=1).astype(BF16)


def _outproj_kernel(of_ref, os_ref, on_ref, x_ref, w_ref, g_ref, o_ref, *, final):
    y = (x_ref[...]
         + _dot(of_ref[...], w_ref[0:FOX_W, :])
         + _dot(os_ref[...], w_ref[FOX_W:FOX_W + SB_W, :])
         + _dot(on_ref[...], w_ref[FOX_W + SB_W:, :]))
    if final:
        ms = jnp.mean(y * y, axis=-1, keepdims=True)
        y = y * lax.rsqrt(ms + NORM_EPS) * g_ref[...]
    o_ref[...] = y


def _outproj(o_fox, o_sb, o_nsa, xf, w, g, final, tm=512):
    m, d = xf.shape
    return pl.pallas_call(
        functools.partial(_outproj_kernel, final=final),
        grid=(m // tm,),
        in_specs=[
            pl.BlockSpec((tm, FOX_W), lambda i: (i, 0)),
            pl.BlockSpec((tm, SB_W), lambda i: (i, 0)),
            pl.BlockSpec((tm, NSA_W), lambda i: (i, 0)),
            pl.BlockSpec((tm, d), lambda i: (i, 0)),
            pl.BlockSpec((d, d), lambda i: (0, 0)),
            pl.BlockSpec((1, d), lambda i: (0, 0)),
        ],
        out_specs=pl.BlockSpec((tm, d), lambda i: (i, 0)),
        out_shape=jax.ShapeDtypeStruct((m, d), F32),
        compiler_params=pltpu.CompilerParams(
            dimension_semantics=("arbitrary",), vmem_limit_bytes=VMEM_LIMIT),
        name="outproj_final" if final else "outproj",
    )(o_fox, o_sb, o_nsa, xf, w, g.reshape(1, d))


def _fox_place():
    place = np.zeros((9, LANES, LANES), np.float32)
    for t in range(3):
        place[t, 0, 3 + t] = -1.0
        place[t, 1, 9 + t] = -1.0
        place[3 + t, 0, t] = 1.0
        place[6 + t, 1, 6 + t] = 1.0
    ones = np.zeros((8, LANES), np.float32)
    ones[0, [0, 1, 2, 6, 7, 8]] = 1.0
    ones[1, [3, 4, 5]] = 1.0
    ones[2, [9, 10, 11]] = 1.0
    return jnp.asarray(place, BF16), jnp.asarray(ones, F32)


def _fox_kernel(q_ref, k_ref, v_ref, z_ref, misc_ref, bf_ref, tri_ref, place_ref, ones_ref,
                o_ref, qq_ref, kk_ref, vp_ref, *, seq, blk):
    pair = pl.program_id(1)
    lane1 = lax.broadcasted_iota(jnp.int32, (1, LANES), 1)
    low = lax.broadcasted_iota(jnp.int32, (blk, LANES), 1) < HEAD_DIM
    nblk = seq // blk

    tri = tri_ref[...]
    carry = jnp.zeros((1, LANES), F32)
    for b in range(nblk):
        rows = slice(b * blk, (b + 1) * blk)
        ls = _log_sigmoid(misc_ref[rows, :] + bf_ref[...])
        xa = jnp.sum(jnp.where(lane1 == 2 * pair, ls, 0.0), axis=1, keepdims=True)
        xb = jnp.sum(jnp.where(lane1 == 2 * pair + 1, ls, 0.0), axis=1, keepdims=True)
        xs = jnp.where(lane1 == 0, xa, jnp.where(lane1 == 1, xb, 0.0))
        hi, mid, lo = _split3(xs)
        cb = _dot(tri, hi) + _dot(tri, mid) + _dot(tri, lo) + carry
        carry = cb[blk - 1:blk, :]
        parts = _split3(cb)

        def placed(base, ones_row):
            out = ones_ref[ones_row:ones_row + 1, :]
            for t in range(3):
                out = out + _dot(parts[t], place_ref[base + t])
            return out.astype(BF16)

        q2 = q_ref[rows, :]
        v2 = v_ref[rows, :]
        zero = jnp.zeros_like(q2)
        one = jnp.ones_like(v2)
        kk_ref[rows, 0:LANES] = k_ref[rows, :]
        kk_ref[rows, LANES:2 * LANES] = placed(0, 0)
        qq_ref[0, rows, 0:LANES] = jnp.where(low, q2, zero)
        qq_ref[0, rows, LANES:2 * LANES] = placed(3, 1)
        qq_ref[1, rows, 0:LANES] = jnp.where(low, zero, q2)
        qq_ref[1, rows, LANES:2 * LANES] = placed(6, 2)
        vp_ref[0, rows, :] = jnp.where(low, v2, one)
        vp_ref[1, rows, :] = jnp.where(low, one, v2)

    causal = (lax.broadcasted_iota(jnp.int32, (blk, blk), 1)
              <= lax.broadcasted_iota(jnp.int32, (blk, blk), 0))

    for i in range(nblk):
        r0 = i * blk
        rows = slice(r0, r0 + blk)
        acc = []
        for h in range(2):
            qa = qq_ref[h, rows, :]
            sd = jnp.where(causal, _dot_nt(qa, kk_ref[rows, :]), -jnp.inf)
            m = jnp.max(sd, axis=1, keepdims=True)
            if i > 0:
                sm = _dot_nt(qa, kk_ref[0:r0, :])
                m = jnp.maximum(m, jnp.max(sm, axis=1, keepdims=True))
                a = (_dot(jnp.exp(sm - m).astype(BF16), vp_ref[h, 0:r0, :])
                     + _dot(jnp.exp(sd - m).astype(BF16), vp_ref[h, rows, :]))
            else:
                a = _dot(jnp.exp(sd - m).astype(BF16), vp_ref[h, rows, :])
            acc.append(a / pltpu.roll(a, HEAD_DIM, axis=1))
        o = jnp.where(low, acc[0], acc[1])
        z = z_ref[rows, :]
        o_ref[rows, :] = (o * (z * _sigmoid(z))).astype(BF16)


def _fox(pb3, pf3, b_f, blk=256):
    bsz, seq, _ = pb3.shape
    npair = FOX_HEADS // 2
    bias = jnp.zeros((1, LANES), F32).at[0, :FOX_HEADS].set(b_f)
    tri = jnp.asarray(np.tril(np.ones((blk, blk), np.float32)), BF16)
    place, ones = _fox_place()

    def col(base):
        return pl.BlockSpec((None, seq, LANES), lambda b, p: (b, 0, base + p))

    def whole(shape):
        return pl.BlockSpec(shape, lambda b, p: (0,) * len(shape))

    return pl.pallas_call(
        functools.partial(_fox_kernel, seq=seq, blk=blk),
        grid=(bsz, npair),
        in_specs=[
            col(PB_FQ), col(PB_FK), col(PB_FV), col(PF_FZ),
            pl.BlockSpec((None, seq, LANES), lambda b, p: (b, 0, PF_MISC)),
            whole((1, LANES)), whole((blk, blk)), whole(place.shape), whole(ones.shape),
        ],
        out_specs=pl.BlockSpec((None, seq, LANES), lambda b, p: (b, 0, p)),
        out_shape=jax.ShapeDtypeStruct((bsz, seq, FOX_W), BF16),
        scratch_shapes=[
            pltpu.VMEM((2, seq, 2 * LANES), BF16),
            pltpu.VMEM((seq, 2 * LANES), BF16),
            pltpu.VMEM((2, seq, LANES), BF16),
        ],
        compiler_params=pltpu.CompilerParams(
            dimension_semantics=("arbitrary", "arbitrary"), vmem_limit_bytes=VMEM_LIMIT),
        name="fox_attn",
    )(pb3, pb3, pb3, pf3, pf3, bias, tri, place, ones)


def _sb_kernel(q_ref, k_ref, v_ref, z_ref, suf_ref, o_ref, *, seq, blk):
    low = lax.broadcasted_iota(jnp.int32, (blk, LANES), 1) < HEAD_DIM
    strict = (lax.broadcasted_iota(jnp.int32, (blk, blk), 1)
              < lax.broadcasted_iota(jnp.int32, (blk, blk), 0))
    nrep = blk // LANES
    suf = suf_ref[...]

    for i in range(seq // blk):
        r0 = i * blk
        rows = slice(r0, r0 + blk)
        q2 = q_ref[rows, :]
        zero = jnp.zeros_like(q2)
        acc = []
        for h in range(2):
            qh = jnp.where(low, q2, zero) if h == 0 else jnp.where(low, zero, q2)
            z = _dot_nt(qh, k_ref[0:r0 + blk, :])
            lsz = jnp.minimum(z, 0.0) - jnp.log(1.0 + jnp.exp(-jnp.abs(z)))
            l1m = lsz - z
            carry = jnp.zeros((blk, LANES), F32)
            out = None
            for j in range(i, -1, -1):
                cs = slice(j * blk, (j + 1) * blk)
                l1 = l1m[:, cs]
                if j == i:
                    l1 = jnp.where(strict, l1, 0.0)
                hi, lo = _split2(l1)
                r = _dot(jnp.concatenate([hi, lo], axis=1), suf)
                a = jnp.exp(lsz[:, cs] + (r[:, 0:blk] + _rep(carry, nrep)))
                if j == i:
                    a = jnp.where(strict, a, 0.0)
                pv = _dot(a.astype(BF16), v_ref[cs, :])
                out = pv if out is None else out + pv
                carry = carry + r[:, blk:blk + LANES]
            acc.append(out)
        o = jnp.where(low, acc[0], acc[1])
        zg = z_ref[rows, :]
        o_ref[rows, :] = (o * (zg * _sigmoid(zg))).astype(BF16)


def _sb(pb3, pf3, blk=256):
    bsz, seq, _ = pb3.shape
    npair = SB_HEADS // 2
    tstrict = np.tril(np.ones((blk, blk), np.float32), -1)
    t3 = np.concatenate([tstrict, np.ones((blk, LANES), np.float32)], axis=1)
    suf = jnp.asarray(np.concatenate([t3, t3], axis=0), BF16)

    def col(base):
        return pl.BlockSpec((None, seq, LANES), lambda b, p: (b, 0, base + p))

    return pl.pallas_call(
        functools.partial(_sb_kernel, seq=seq, blk=blk),
        grid=(bsz, npair),
        in_specs=[
            col(PB_SQ), col(PB_SK), col(PB_SV), col(PF_SZ),
            pl.BlockSpec(suf.shape, lambda b, p: (0, 0)),
        ],
        out_specs=pl.BlockSpec((None, seq, LANES), lambda b, p: (b, 0, p)),
        out_shape=jax.ShapeDtypeStruct((bsz, seq, SB_W), BF16),
        compiler_params=pltpu.CompilerParams(
            dimension_semantics=("arbitrary", "arbitrary"), vmem_limit_bytes=VMEM_LIMIT),
        name="sb_attn",
    )(pb3, pb3, pb3, pf3, suf)


def _nsa_kernel(q_ref, ks_ref, vs_ref, kw_ref, vw_ref, z_ref, misc_ref, xk_ref, xv_ref,
                posk_ref, w1k_ref, w2k_ref, posv_ref, w1v_ref, w2v_ref, msel_ref, eneg_ref,
                o_ref, kc_ref, vc_ref, pen_ref, m_ref, l_ref, acc_ref, osum_ref, *, seq, blk):
    g_heads = NSA_HEADS
    rows4 = g_heads * blk
    n_cmp = (seq - CMP_BLOCK) // CMP_STRIDE + 1

    def compress(x_ref, pos_ref, w1_ref, w2_ref):
        x = x_ref[...]
        a = _dot((x + pos_ref[0:1, :]).astype(BF16), w1_ref[0])
        b = _dot((x + pos_ref[1:2, :]).astype(BF16), w1_ref[1])
        hid = a + pltpu.roll(b, b.shape[0] - 1, axis=0)
        hid = hid * _sigmoid(hid)
        return _dot(hid.astype(BF16), w2_ref[...]).astype(BF16)

    kc_ref[...] = compress(xk_ref, posk_ref, w1k_ref, w2k_ref)
    vc_ref[...] = compress(xv_ref, posv_ref, w1v_ref, w2v_ref)

    low = lax.broadcasted_iota(jnp.int32, (blk, LANES), 1) < HEAD_DIM
    row4 = lax.broadcasted_iota(jnp.int32, (rows4, 1), 0)
    head = row4 // blk
    r4 = row4 - head * blk
    slope = jnp.where(head == 0, 2.0 ** -2, jnp.where(head == 1, 2.0 ** -4,
                      jnp.where(head == 2, 2.0 ** -6, 2.0 ** -8))).astype(F32)
    col_b = lax.broadcasted_iota(jnp.int32, (1, blk), 1)
    rel = (r4 - col_b).astype(F32)
    srel = slope * rel
    causal4 = col_b <= r4
    after4 = col_b > r4
    lane1 = lax.broadcasted_iota(jnp.int32, (1, LANES), 1)
    cmp_end = (lane1 * CMP_STRIDE + (CMP_BLOCK - 1)).astype(F32)
    rowq = lax.broadcasted_iota(jnp.int32, (blk, 1), 0)
    nrep = blk // LANES

    def reset():
        m_ref[...] = jnp.full(m_ref.shape, NEG_BIG, F32)
        l_ref[...] = jnp.zeros(l_ref.shape, F32)
        acc_ref[...] = jnp.zeros(acc_ref.shape, F32)

    def online(s, vb):
        m_prev = m_ref[...]
        m_next = jnp.maximum(m_prev, jnp.max(s, axis=1, keepdims=True))
        p = jnp.exp(s - _rep(m_next, nrep))
        alpha = jnp.exp(m_prev - m_next)
        l_ref[...] = alpha * l_ref[...] + jnp.sum(p, axis=1, keepdims=True)
        acc_ref[...] = alpha * acc_ref[...] + _dot(p.astype(BF16), vb)
        m_ref[...] = m_next

    def q_block(i, _):
        qs = pl.multiple_of(i * blk, blk)
        q01 = q_ref[pl.ds(qs, blk), 0:LANES]
        q23 = q_ref[pl.ds(qs, blk), LANES:2 * LANES]
        zero = jnp.zeros_like(q01)
        qst = jnp.concatenate([jnp.where(low, q01, zero), jnp.where(low, zero, q01),
                               jnp.where(low, q23, zero), jnp.where(low, zero, q23)], axis=0)
        tpos4 = (qs + r4).astype(F32)

        dist_c = tpos4 - cmp_end
        valid_c = (dist_c >= 0.0) & (lane1 < n_cmp)
        sc = _dot_nt(qst, kc_ref[...]) - slope * dist_c
        sc = jnp.where(valid_c, sc, -jnp.inf)
        mc = jnp.max(sc, axis=1, keepdims=True)
        mc = jnp.where(mc == -jnp.inf, 0.0, mc)
        pc = jnp.exp(sc - mc)
        ssum = jnp.sum(pc, axis=1, keepdims=True)
        pc = pc / jnp.where(ssum > 0.0, ssum, 1.0)
        osum_ref[...] = _dot(pc.astype(BF16), vc_ref[...])

        pcs = pc[0:blk] + pc[blk:2 * blk] + pc[2 * blk:3 * blk] + pc[3 * blk:4 * blk]
        hi, lo = _split2(pcs)
        imp = _dot(hi, msel_ref[...]) + _dot(lo, msel_ref[...])
        cur = (qs + rowq) // SEL_BLOCK
        back = cur - lane1
        valid_b = back >= 0
        forced = (lane1 == 0) | (valid_b & (back < SEL_N_LOCAL))
        imp = jnp.where(forced, jnp.inf, jnp.where(valid_b, imp, -jnp.inf))
        rank = jnp.zeros((blk, LANES), F32)
        for c in range(seq // SEL_BLOCK):
            colv = imp[:, c:c + 1]
            tie = jnp.where(lane1 > c, 1.0, 0.0)
            rank = rank + jnp.where(colv > imp, 1.0, jnp.where(colv == imp, tie, 0.0))
        unsel = jnp.where((rank >= float(SEL_TOPK)) & (lane1 < seq // SEL_BLOCK), 1.0, 0.0)
        pen_ref[...] = _dot(unsel.astype(BF16), eneg_ref[...])

        def gate(branch):
            sg = _sigmoid(misc_ref[pl.ds(qs, blk), :])
            return jnp.concatenate(
                [sg[:, MISC_NG + 3 * g + branch:MISC_NG + 3 * g + branch + 1]
                 for g in range(g_heads)], axis=0)

        osum_ref[...] = osum_ref[...] * gate(0)

        reset()

        def sel_step(ks, diag):
            off = (qs - ks).astype(F32)
            pen = pen_ref[:, pl.ds(ks, blk)]
            s = (_dot_nt(qst, ks_ref[pl.ds(ks, blk), :])
                 + jnp.concatenate([pen] * g_heads, axis=0)
                 - (srel + slope * off))
            if diag:
                s = jnp.where(causal4, s, -jnp.inf)
            online(s, vs_ref[pl.ds(ks, blk), :])

        def sel_loop(j, _):
            sel_step(pl.multiple_of(j * blk, blk), False)
            return 0

        lax.fori_loop(0, i, sel_loop, 0)
        sel_step(qs, True)
        osum_ref[...] = osum_ref[...] + (acc_ref[...] / l_ref[...]) * gate(1)

        reset()
        nwin = WINDOW // blk

        def win_step(ks, mode):
            off = (qs - ks).astype(F32)
            s = _dot_nt(qst, kw_ref[pl.ds(ks, blk), :]) - (srel + slope * off)
            if mode == "left":
                s = jnp.where(after4, s, -jnp.inf)
            elif mode == "diag":
                s = jnp.where(causal4, s, -jnp.inf)
            online(s, vw_ref[pl.ds(ks, blk), :])

        @pl.when(i >= nwin)
        def _():
            win_step(pl.multiple_of((i - nwin) * blk, blk), "left")

        def win_loop(j, _):
            win_step(pl.multiple_of(j * blk, blk), "mid")
            return 0

        lax.fori_loop(jnp.maximum(i - nwin + 1, 0), i, win_loop, 0)
        win_step(qs, "diag")
        osum = osum_ref[...] + (acc_ref[...] / l_ref[...]) * gate(2)

        z = z_ref[pl.ds(qs, blk), :]
        zz = z * _sigmoid(z)
        o01 = jnp.where(low, osum[0:blk], osum[blk:2 * blk])
        o23 = jnp.where(low, osum[2 * blk:3 * blk], osum[3 * blk:4 * blk])
        o_ref[pl.ds(qs, blk), 0:LANES] = (o01 * zz[:, 0:LANES]).astype(BF16)
        o_ref[pl.ds(qs, blk), LANES:2 * LANES] = (o23 * zz[:, LANES:2 * LANES]).astype(BF16)
        return 0

    lax.fori_loop(0, seq // blk, q_block, 0)


def _nsa_constants(seq):
    n_cmp = (seq - CMP_BLOCK) // CMP_STRIDE + 1
    n_sel = seq // SEL_BLOCK
    cs = np.arange(n_cmp) * CMP_STRIDE
    ce = cs + CMP_BLOCK - 1
    ss = np.arange(n_sel) * SEL_BLOCK
    se = ss + SEL_BLOCK - 1
    msel = np.zeros((LANES, LANES), np.float32)
    msel[:n_cmp, :n_sel] = (cs[:, None] <= se[None, :]) & (ce[:, None] >= ss[None, :])
    eneg = np.zeros((LANES, seq), np.float32)
    eneg[:n_sel, :] = np.where(np.arange(seq)[None, :] // SEL_BLOCK == np.arange(n_sel)[:, None],
                               PEN, 0.0)
    return jnp.asarray(msel, BF16), jnp.asarray(eneg, BF16)


def _nsa(pb3, pf3, pos_k, w1_k, w2_k, pos_v, w1_v, w2_v, blk=128):
    bsz, seq, _ = pb3.shape
    nchunk = seq // CMP_STRIDE
    cw = CMP_STRIDE * HEAD_DIM
    kvc = pf3[:, :, PF_KVC * LANES:(PF_KVC + 1) * LANES]
    xk = kvc[:, :, :HEAD_DIM].reshape(bsz, nchunk, cw)
    xv = kvc[:, :, HEAD_DIM:].reshape(bsz, nchunk, cw)

    def prep(pos, w1, w2):
        w1d = jnp.concatenate([w1, w1], axis=1).astype(BF16).reshape(2, cw, LANES)
        w2p = jnp.zeros((LANES, LANES), F32).at[:HEAD_DIM, :].set(
            jnp.concatenate([w2, w2], axis=1)).astype(BF16)
        return pos.reshape(2, cw), w1d, w2p

    msel, eneg = _nsa_constants(seq)

    def col(base, nblk=1):
        return pl.BlockSpec((None, seq, nblk * LANES), lambda b: (b, 0, base // nblk))

    def whole(shape):
        return pl.BlockSpec(shape, lambda b: (0,) * len(shape))

    rows4 = NSA_HEADS * blk
    return pl.pallas_call(
        functools.partial(_nsa_kernel, seq=seq, blk=blk),
        grid=(bsz,),
        in_specs=[
            col(PB_NQ, 2), col(PB_NKS), col(PB_NVS), col(PB_NKW), col(PB_NVW),
            col(PF_NZ, 2), col(PF_MISC),
            pl.BlockSpec((None, nchunk, cw), lambda b: (b, 0, 0)),
            pl.BlockSpec((None, nchunk, cw), lambda b: (b, 0, 0)),
            whole((2, cw)), whole((2, cw, LANES)), whole((LANES, LANES)),
            whole((2, cw)), whole((2, cw, LANES)), whole((LANES, LANES)),
            whole((LANES, LANES)), whole((LANES, seq)),
        ],
        out_specs=pl.BlockSpec((None, seq, NSA_W), lambda b: (b, 0, 0)),
        out_shape=jax.ShapeDtypeStruct((bsz, seq, NSA_W), BF16),
        scratch_shapes=[
            pltpu.VMEM((nchunk, LANES), BF16),
            pltpu.VMEM((nchunk, LANES), BF16),
            pltpu.VMEM((blk, seq), F32),
            pltpu.VMEM((rows4, LANES), F32),
            pltpu.VMEM((rows4, LANES), F32),
            pltpu.VMEM((rows4, LANES), F32),
            pltpu.VMEM((rows4, LANES), F32),
        ],
        compiler_params=pltpu.CompilerParams(
            dimension_semantics=("arbitrary",), vmem_limit_bytes=VMEM_LIMIT),
        name="nsa_attn",
    )(pb3, pb3, pb3, pb3, pb3, pf3, pf3, xk, xv,
      *prep(pos_k, w1_k, w2_k), *prep(pos_v, w1_v, w2_v), msel, eneg)


def kernel(x, norm_g, w_in, b_f, cmp_pos_k, cmp_w1_k, cmp_w2_k,
           cmp_pos_v, cmp_w1_v, cmp_w2_v, w_out, final_g):
    bsz, seq, d = x.shape
    xf = x.reshape(bsz * seq, d)
    for l in range(DEPTH):
        pb, pf = _inproj(xf, norm_g[l], _pack_w_in(w_in[l]))
        pb3 = pb.reshape(bsz, seq, NB_COLS)
        pf3 = pf.reshape(bsz, seq, NF_COLS)
        o_fox = _fox(pb3, pf3, b_f[l])
        o_sb = _sb(pb3, pf3)
        o_nsa = _nsa(pb3, pf3, cmp_pos_k[l], cmp_w1_k[l], cmp_w2_k[l],
                     cmp_pos_v[l], cmp_w1_v[l], cmp_w2_v[l])
        xf = _outproj(o_fox.reshape(bsz * seq, FOX_W), o_sb.reshape(bsz * seq, SB_W),
                      o_nsa.reshape(bsz * seq, NSA_W), xf, w_out[l].astype(BF16),
                      final_g, final=(l == DEPTH - 1))
    return xf.reshape(bsz, seq, d)
```

```python
import functools

import numpy as np
import jax
import jax.numpy as jnp
from jax import lax
from jax.experimental import pallas as pl
from jax.experimental.pallas import tpu as pltpu

F32 = jnp.float32
BF16 = jnp.bfloat16

D_MODEL = 1024
DEPTH = 2
HEAD_DIM = 64
LANES = 128
FOX_HEADS = 6
SB_HEADS = 6
NSA_HEADS = 4
FOX_W = FOX_HEADS * HEAD_DIM
SB_W = SB_HEADS * HEAD_DIM
NSA_W = NSA_HEADS * HEAD_DIM
CMP_BLOCK = 32
CMP_STRIDE = 16
SEL_BLOCK = 64
SEL_TOPK = 8
SEL_N_LOCAL = 2
WINDOW = 512
NORM_EPS = 1e-6
QK_SCALE = HEAD_DIM ** -0.5
NEG_BIG = -1e30
PEN = -(2.0 ** 100)

PB_FQ, PB_FK, PB_FV, PB_SQ, PB_SK, PB_SV, PB_NQ, PB_NKS, PB_NVS, PB_NKW, PB_NVW = (
    0, 3, 6, 9, 12, 15, 18, 20, 21, 22, 23)
PB_BLOCKS = 24
PF_FZ, PF_SZ, PF_NZ, PF_KVC, PF_MISC = 0, 3, 6, 8, 9
PF_BLOCKS = 10
NB_COLS = PB_BLOCKS * LANES
NF_COLS = PF_BLOCKS * LANES
MISC_NG = FOX_HEADS

VMEM_LIMIT = 56 * 1024 * 1024

_NT = (((1,), (1,)), ((), ()))


def _dot(a, b):
    return jnp.dot(a, b, preferred_element_type=F32)


def _dot_nt(a, b):
    return lax.dot_general(a, b, _NT, preferred_element_type=F32)


def _sigmoid(x):
    return 1.0 / (1.0 + jnp.exp(-x))


def _log_sigmoid(x):
    return -(jnp.maximum(-x, 0.0) + jnp.log1p(jnp.exp(-jnp.abs(x))))


def _split3(x):
    hi = x.astype(BF16)
    r = x - hi.astype(F32)
    mid = r.astype(BF16)
    lo = (r - mid.astype(F32)).astype(BF16)
    return hi, mid, lo


def _split2(x):
    hi = x.astype(BF16)
    lo = (x - hi.astype(F32)).astype(BF16)
    return hi, lo


def _rep(x, n):
    return x if n == 1 else jnp.concatenate([x] * n, axis=1)


def _inproj_kernel(x_ref, g_ref, w_ref, pb_ref, pf_ref):
    x = x_ref[...]
    ms = jnp.mean(x * x, axis=-1, keepdims=True)
    h = (x * lax.rsqrt(ms + NORM_EPS) * g_ref[...]).astype(BF16)
    cb = 512
    for c in range(0, NB_COLS, cb):
        pb_ref[:, c:c + cb] = _dot(h, w_ref[:, c:c + cb]).astype(BF16)
    cf = 640
    for c in range(0, NF_COLS, cf):
        pf_ref[:, c:c + cf] = _dot(h, w_ref[:, NB_COLS + c:NB_COLS + c + cf])


def _inproj(xf, g, w_all, tm=512):
    m, d = xf.shape
    return pl.pallas_call(
        _inproj_kernel,
        grid=(m // tm,),
        in_specs=[
            pl.BlockSpec((tm, d), lambda i: (i, 0)),
            pl.BlockSpec((1, d), lambda i: (0, 0)),
            pl.BlockSpec((d, NB_COLS + NF_COLS), lambda i: (0, 0)),
        ],
        out_specs=[
            pl.BlockSpec((tm, NB_COLS), lambda i: (i, 0)),
            pl.BlockSpec((tm, NF_COLS), lambda i: (i, 0)),
        ],
        out_shape=[
            jax.ShapeDtypeStruct((m, NB_COLS), BF16),
            jax.ShapeDtypeStruct((m, NF_COLS), F32),
        ],
        compiler_params=pltpu.CompilerParams(
            dimension_semantics=("arbitrary",), vmem_limit_bytes=VMEM_LIMIT),
        name="inproj",
    )(xf, g.reshape(1, d), w_all)


def _pack_w_in(w):
    sizes = (FOX_W, FOX_W, FOX_W, FOX_HEADS, FOX_W, SB_W, SB_W, SB_W, SB_W,
             NSA_W, HEAD_DIM, HEAD_DIM, HEAD_DIM, HEAD_DIM, HEAD_DIM, HEAD_DIM,
             3 * NSA_HEADS, NSA_W)
    offs = np.concatenate([[0], np.cumsum(sizes)])
    (fq, fk, fv, ff, fz, sq, sk, sv, sz, nq, nkc, nvc, nks, nvs, nkw, nvw, ng, nz) = [
        w[:, offs[i]:offs[i + 1]] for i in range(len(sizes))]
    pad = jnp.zeros((w.shape[0], LANES - FOX_HEADS - 3 * NSA_HEADS), w.dtype)
    cols = [fq * QK_SCALE, fk, fv, sq * QK_SCALE, sk, sv, nq * QK_SCALE,
            nks, nks, nvs, nvs, nkw, nkw, nvw, nvw,
            fz, sz, nz, nkc, nvc, ff, ng, pad]
    return jnp.concatenate(cols, axis=1).astype(BF16)


def _outproj_kernel(of_ref, os_ref, on_ref, x_ref, w_ref, g_ref, o_ref, *, final):
    y = (x_ref[...]
         + _dot(of_ref[...], w_ref[0:FOX_W, :])
         + _dot(os_ref[...], w_ref[FOX_W:FOX_W + SB_W, :])
         + _dot(on_ref[...], w_ref[FOX_W + SB_W:, :]))
    if final:
        ms = jnp.mean(y * y, axis=-1, keepdims=True)
        y = y * lax.rsqrt(ms + NORM_EPS) * g_ref[...]
    o_ref[...] = y


def _outproj(o_fox, o_sb, o_nsa, xf, w, g, final, tm=512):
    m, d = xf.shape
    return pl.pallas_call(
        functools.partial(_outproj_kernel, final=final),
        grid=(m // tm,),
        in_specs=[
            pl.BlockSpec((tm, FOX_W), lambda i: (i, 0)),
            pl.BlockSpec((tm, SB_W), lambda i: (i, 0)),
            pl.BlockSpec((tm, NSA_W), lambda i: (i, 0)),
            pl.BlockSpec((tm, d), lambda i: (i, 0)),
            pl.BlockSpec((d, d), lambda i: (0, 0)),
            pl.BlockSpec((1, d), lambda i: (0, 0)),
        ],
        out_specs=pl.BlockSpec((tm, d), lambda i: (i, 0)),
        out_shape=jax.ShapeDtypeStruct((m, d), F32),
        compiler_params=pltpu.CompilerParams(
            dimension_semantics=("arbitrary",), vmem_limit_bytes=VMEM_LIMIT),
        name="outproj_final" if final else "outproj",
    )(o_fox, o_sb, o_nsa, xf, w, g.reshape(1, d))


def _fox_place():
    place = np.zeros((9, LANES, LANES), np.float32)
    for t in range(3):
        place[t, 0, 3 + t] = -1.0
        place[t, 1, 9 + t] = -1.0
        place[3 + t, 0, t] = 1.0
        place[6 + t, 1, 6 + t] = 1.0
    ones = np.zeros((8, LANES), np.float32)
    ones[0, [0, 1, 2, 6, 7, 8]] = 1.0
    ones[1, [3, 4, 5]] = 1.0
    ones[2, [9, 10, 11]] = 1.0
    return jnp.asarray(place, BF16), jnp.asarray(ones, F32)


def _fox_kernel(q_ref, k_ref, v_ref, z_ref, misc_ref, bf_ref, tri_ref, place_ref, ones_ref,
                o_ref, qq_ref, kk_ref, vp_ref, *, seq, blk):
    pair = pl.program_id(1)
    lane1 = lax.broadcasted_iota(jnp.int32, (1, LANES), 1)
    low = lax.broadcasted_iota(jnp.int32, (blk, LANES), 1) < HEAD_DIM
    nblk = seq // blk

    tri = tri_ref[...]
    carry = jnp.zeros((1, LANES), F32)
    for b in range(nblk):
        rows = slice(b * blk, (b + 1) * blk)
        ls = _log_sigmoid(misc_ref[rows, :] + bf_ref[...])
        xa = jnp.sum(jnp.where(lane1 == 2 * pair, ls, 0.0), axis=1, keepdims=True)
        xb = jnp.sum(jnp.where(lane1 == 2 * pair + 1, ls, 0.0), axis=1, keepdims=True)
        xs = jnp.where(lane1 == 0, xa, jnp.where(lane1 == 1, xb, 0.0))
        hi, mid, lo = _split3(xs)
        cb = _dot(tri, hi) + _dot(tri, mid) + _dot(tri, lo) + carry
        carry = cb[blk - 1:blk, :]
        parts = _split3(cb)

        def placed(base, ones_row):
            out = ones_ref[ones_row:ones_row + 1, :]
            for t in range(3):
                out = out + _dot(parts[t], place_ref[base + t])
            return out.astype(BF16)

        q2 = q_ref[rows, :]
        v2 = v_ref[rows, :]
        zero = jnp.zeros_like(q2)
        one = jnp.ones_like(v2)
        kk_ref[rows, 0:LANES] = k_ref[rows, :]
        kk_ref[rows, LANES:2 * LANES] = placed(0, 0)
        qq_ref[0, rows, 0:LANES] = jnp.where(low, q2, zero)
        qq_ref[0, rows, LANES:2 * LANES] = placed(3, 1)
        qq_ref[1, rows, 0:LANES] = jnp.where(low, zero, q2)
        qq_ref[1, rows, LANES:2 * LANES] = placed(6, 2)
        vp_ref[0, rows, :] = jnp.where(low, v2, one)
        vp_ref[1, rows, :] = jnp.where(low, one, v2)

    causal = (lax.broadcasted_iota(jnp.int32, (blk, blk), 1)
              <= lax.broadcasted_iota(jnp.int32, (blk, blk), 0))

    for i in range(nblk):
        r0 = i * blk
        rows = slice(r0, r0 + blk)
        acc = []
        for h in range(2):
            qa = qq_ref[h, rows, :]
            sd = jnp.where(causal, _dot_nt(qa, kk_ref[rows, :]), -jnp.inf)
            m = jnp.max(sd, axis=1, keepdims=True)
            if i > 0:
                sm = _dot_nt(qa, kk_ref[0:r0, :])
                m = jnp.maximum(m, jnp.max(sm, axis=1, keepdims=True))
                a = (_dot(jnp.exp(sm - m).astype(BF16), vp_ref[h, 0:r0, :])
                     + _dot(jnp.exp(sd - m).astype(BF16), vp_ref[h, rows, :]))
            else:
                a = _dot(jnp.exp(sd - m).astype(BF16), vp_ref[h, rows, :])
            acc.append(a / pltpu.roll(a, HEAD_DIM, axis=1))
        o = jnp.where(low, acc[0], acc[1])
        z = z_ref[rows, :]
        o_ref[rows, :] = (o * (z * _sigmoid(z))).astype(BF16)


def _fox(pb3, pf3, b_f, blk=256):
    bsz, seq, _ = pb3.shape
    npair = FOX_HEADS // 2
    bias = jnp.zeros((1, LANES), F32).at[0, :FOX_HEADS].set(b_f)
    tri = jnp.asarray(np.tril(np.ones((blk, blk), np.float32)), BF16)
    place, ones = _fox_place()

    def col(base):
        return pl.BlockSpec((None, seq, LANES), lambda b, p: (b, 0, base + p))

    def whole(shape):
        return pl.BlockSpec(shape, lambda b, p: (0,) * len(shape))

    return pl.pallas_call(
        functools.partial(_fox_kernel, seq=seq, blk=blk),
        grid=(bsz, npair),
        in_specs=[
            col(PB_FQ), col(PB_FK), col(PB_FV), col(PF_FZ),
            pl.BlockSpec((None, seq, LANES), lambda b, p: (b, 0, PF_MISC)),
            whole((1, LANES)), whole((blk, blk)), whole(place.shape), whole(ones.shape),
        ],
        out_specs=pl.BlockSpec((None, seq, LANES), lambda b, p: (b, 0, p)),
        out_shape=jax.ShapeDtypeStruct((bsz, seq, FOX_W), BF16),
        scratch_shapes=[
            pltpu.VMEM((2, seq, 2 * LANES), BF16),
            pltpu.VMEM((seq, 2 * LANES), BF16),
            pltpu.VMEM((2, seq, LANES), BF16),
        ],
        compiler_params=pltpu.CompilerParams(
            dimension_semantics=("arbitrary", "arbitrary"), vmem_limit_bytes=VMEM_LIMIT),
        name="fox_attn",
    )(pb3, pb3, pb3, pf3, pf3, bias, tri, place, ones)


def _sb_kernel(q_ref, k_ref, v_ref, z_ref, suf_ref, o_ref, *, seq, blk):
    low = lax.broadcasted_iota(jnp.int32, (blk, LANES), 1) < HEAD_DIM
    strict = (lax.broadcasted_iota(jnp.int32, (blk, blk), 1)
              < lax.broadcasted_iota(jnp.int32, (blk, blk), 0))
    nrep = blk // LANES
    suf = suf_ref[...]

    for i in range(seq // blk):
        r0 = i * blk
        rows = slice(r0, r0 + blk)
        q2 = q_ref[rows, :]
        zero = jnp.zeros_like(q2)
        acc = []
        for h in range(2):
            qh = jnp.where(low, q2, zero) if h == 0 else jnp.where(low, zero, q2)
            z = _dot_nt(qh, k_ref[0:r0 + blk, :])
            lsz = jnp.minimum(z, 0.0) - jnp.log(1.0 + jnp.exp(-jnp.abs(z)))
            l1m = lsz - z
            carry = jnp.zeros((blk, LANES), F32)
            out = None
            for j in range(i, -1, -1):
                cs = slice(j * blk, (j + 1) * blk)
                l1 = l1m[:, cs]
                if j == i:
                    l1 = jnp.where(strict, l1, 0.0)
                hi, lo = _split2(l1)
                r = _dot(jnp.concatenate([hi, lo], axis=1), suf)
                a = jnp.exp(lsz[:, cs] + (r[:, 0:blk] + _rep(carry, nrep)))
                if j == i:
                    a = jnp.where(strict, a, 0.0)
                pv = _dot(a.astype(BF16), v_ref[cs, :])
                out = pv if out is None else out + pv
                carry = carry + r[:, blk:blk + LANES]
            acc.append(out)
        o = jnp.where(low, acc[0], acc[1])
        zg = z_ref[rows, :]
        o_ref[rows, :] = (o * (zg * _sigmoid(zg))).astype(BF16)


def _sb(pb3, pf3, blk=256):
    bsz, seq, _ = pb3.shape
    npair = SB_HEADS // 2
    tstrict = np.tril(np.ones((blk, blk), np.float32), -1)
    t3 = np.concatenate([tstrict, np.ones((blk, LANES), np.float32)], axis=1)
    suf = jnp.asarray(np.concatenate([t3, t3], axis=0), BF16)

    def col(base):
        return pl.BlockSpec((None, seq, LANES), lambda b, p: (b, 0, base + p))

    return pl.pallas_call(
        functools.partial(_sb_kernel, seq=seq, blk=blk),
        grid=(bsz, npair),
        in_specs=[
            col(PB_SQ), col(PB_SK), col(PB_SV), col(PF_SZ),
            pl.BlockSpec(suf.shape, lambda b, p: (0, 0)),
        ],
        out_specs=pl.BlockSpec((None, seq, LANES), lambda b, p: (b, 0, p)),
        out_shape=jax.ShapeDtypeStruct((bsz, seq, SB_W), BF16),
        compiler_params=pltpu.CompilerParams(
            dimension_semantics=("arbitrary", "arbitrary"), vmem_limit_bytes=VMEM_LIMIT),
        name="sb_attn",
    )(pb3, pb3, pb3, pf3, suf)


NSA_HEAD_ORDER = (0, 2, 1, 3)
SEL_LANES = 32
POS_HI_LANE = 32
POS_LO_LANE = 33
NSA_SLOPES = tuple(2.0 ** (-8.0 * (g + 1) / NSA_HEADS) for g in range(NSA_HEADS))


def _nsa_kernel(q_ref, ks_ref, vs_ref, kw_ref, vw_ref, z_ref, misc_ref, xk_ref, xv_ref,
                posk_ref, w1k_ref, w2k_ref, posv_ref, w1v_ref, w2v_ref,
                mselt_ref, kaugs_ref, kaugw_ref, qslope_ref,
                o_ref, kc_ref, vc_ref, kks_ref, kkw_ref, vse_ref, vso_ref, vwe_ref, vwo_ref,
                *, seq, blk):
    rows4 = NSA_HEADS * blk
    half = rows4 // 2
    n_cmp = (seq - CMP_BLOCK) // CMP_STRIDE + 1
    n_sel = seq // SEL_BLOCK
    nwin = WINDOW // blk

    def compress(x_ref, pos_ref, w1_ref, w2_ref):
        x = x_ref[...]
        a = _dot((x + pos_ref[0:1, :]).astype(BF16), w1_ref[0])
        b = _dot((x + pos_ref[1:2, :]).astype(BF16), w1_ref[1])
        hid = a + pltpu.roll(b, b.shape[0] - 1, axis=0)
        hid = hid * _sigmoid(hid)
        return _dot(hid.astype(BF16), w2_ref[...]).astype(BF16)

    kc_ref[...] = compress(xk_ref, posk_ref, w1k_ref, w2k_ref)
    vc_ref[...] = compress(xv_ref, posv_ref, w1v_ref, w2v_ref)

    low_s = lax.broadcasted_iota(jnp.int32, (seq, LANES), 1) < HEAD_DIM
    kks_ref[:, 0:LANES] = ks_ref[...]
    kks_ref[:, LANES:2 * LANES] = kaugs_ref[...]
    kkw_ref[:, 0:LANES] = kw_ref[...]
    kkw_ref[:, LANES:2 * LANES] = kaugw_ref[...]
    vs = vs_ref[...]
    vw = vw_ref[...]
    one = jnp.ones_like(vs)
    vse_ref[...] = jnp.where(low_s, vs, one)
    vso_ref[...] = jnp.where(low_s, one, vs)
    vwe_ref[...] = jnp.where(low_s, vw, one)
    vwo_ref[...] = jnp.where(low_s, one, vw)

    low = lax.broadcasted_iota(jnp.int32, (blk, LANES), 1) < HEAD_DIM
    row4 = lax.broadcasted_iota(jnp.int32, (rows4, 1), 0)
    grp = row4 // blk
    r4 = row4 - grp * blk
    gslope = [NSA_SLOPES[g] for g in NSA_HEAD_ORDER]
    slope = jnp.where(grp == 0, gslope[0], jnp.where(grp == 1, gslope[1],
                      jnp.where(grp == 2, gslope[2], gslope[3]))).astype(F32)
    col_b = lax.broadcasted_iota(jnp.int32, (1, blk), 1)
    causal4 = col_b <= r4
    after4 = col_b > r4
    lane1 = lax.broadcasted_iota(jnp.int32, (1, LANES), 1)
    cmp_end = (lane1 * CMP_STRIDE + (CMP_BLOCK - 1)).astype(F32)
    jrow = lax.broadcasted_iota(jnp.int32, (n_sel, blk), 0)
    tcol = lax.broadcasted_iota(jnp.int32, (n_sel, blk), 1)
    eye = jnp.where(lax.broadcasted_iota(jnp.int32, (LANES, LANES), 0)
                    == lax.broadcasted_iota(jnp.int32, (LANES, LANES), 1), 1.0, 0.0).astype(BF16)
    sel_lane = lax.broadcasted_iota(jnp.int32, (rows4, LANES), 1) < SEL_LANES
    qslope = qslope_ref[...]

    def softmax_pv(parts, ve_ref, vo_ref):
        m = None
        for s, _ in parts:
            mp = jnp.max(s, axis=1, keepdims=True)
            m = mp if m is None else jnp.maximum(m, mp)
        acc_e = acc_o = None
        for s, ksl in parts:
            p = jnp.exp(s - m).astype(BF16)
            pe = _dot(p[0:half], ve_ref[ksl, :])
            po = _dot(p[half:rows4], vo_ref[ksl, :])
            acc_e = pe if acc_e is None else acc_e + pe
            acc_o = po if acc_o is None else acc_o + po
        return (acc_e / pltpu.roll(acc_e, HEAD_DIM, axis=1),
                acc_o / pltpu.roll(acc_o, HEAD_DIM, axis=1))

    for i in range(seq // blk):
        qs = i * blk
        rows = slice(qs, qs + blk)
        q01 = q_ref[rows, 0:LANES]
        q23 = q_ref[rows, LANES:2 * LANES]
        zero = jnp.zeros_like(q01)
        qst = jnp.concatenate([jnp.where(low, q01, zero), jnp.where(low, q23, zero),
                               jnp.where(low, zero, q01), jnp.where(low, zero, q23)], axis=0)
        tpos4 = (qs + r4).astype(F32)

        dist_c = tpos4 - cmp_end
        valid_c = (dist_c >= 0.0) & (lane1 < n_cmp)
        sc = _dot_nt(qst, kc_ref[...]) - slope * dist_c
        sc = jnp.where(valid_c, sc, -jnp.inf)
        mc = jnp.max(sc, axis=1, keepdims=True)
        mc = jnp.where(mc == -jnp.inf, 0.0, mc)
        pc = jnp.exp(sc - mc)
        ssum = jnp.sum(pc, axis=1, keepdims=True)
        pc = pc / jnp.where(ssum > 0.0, ssum, 1.0)
        oc = _dot(pc.astype(BF16), vc_ref[...])

        pcs = pc[0:blk] + pc[blk:2 * blk] + pc[2 * blk:3 * blk] + pc[3 * blk:4 * blk]
        hi, lo = _split2(pcs)
        imp = _dot_nt(mselt_ref[...], jnp.concatenate([hi, lo], axis=1))
        back = (qs + tcol) // SEL_BLOCK - jrow
        imp = jnp.where(back < 0, -jnp.inf,
                        jnp.where(back < SEL_N_LOCAL, jnp.inf,
                                  jnp.where(jrow == 0, jnp.inf, imp)))
        rank = jnp.zeros((n_sel, blk), F32)
        for c in range(n_sel):
            rowc = imp[c:c + 1, :]
            tie = jnp.where(jrow > c, 1.0, 0.0)
            rank = rank + jnp.where(rowc > imp, 1.0, jnp.where(rowc == imp, tie, 0.0))
        unsel_t = jnp.where(rank >= float(SEL_TOPK), 1.0, 0.0).astype(BF16)
        unsel_t = jnp.concatenate([unsel_t, jnp.zeros((LANES - n_sel, blk), BF16)], axis=0)
        unsel = _dot_nt(eye, unsel_t).astype(BF16)

        qaug = jnp.where(sel_lane, jnp.concatenate([unsel] * NSA_HEADS, axis=0), qslope)
        qop = jnp.concatenate([qst, qaug], axis=1)

        parts = []
        if i > 0:
            parts.append((_dot_nt(qop, kks_ref[0:qs, :]), slice(0, qs)))
        parts.append((jnp.where(causal4, _dot_nt(qop, kks_ref[rows, :]), -jnp.inf), rows))
        sel_e, sel_o = softmax_pv(parts, vse_ref, vso_ref)

        parts = []
        if i >= nwin:
            lsl = slice(qs - WINDOW, qs - WINDOW + blk)
            parts.append((jnp.where(after4, _dot_nt(qop, kkw_ref[lsl, :]), -jnp.inf), lsl))
        w0 = max(qs - WINDOW + blk, 0)
        if qs > w0:
            parts.append((_dot_nt(qop, kkw_ref[w0:qs, :]), slice(w0, qs)))
        parts.append((jnp.where(causal4, _dot_nt(qop, kkw_ref[rows, :]), -jnp.inf), rows))
        win_e, win_o = softmax_pv(parts, vwe_ref, vwo_ref)

        sg = _sigmoid(misc_ref[rows, :])

        def head_out(k):
            c = MISC_NG + 3 * NSA_HEAD_ORDER[k]
            sel, win = (sel_e, win_e) if k < 2 else (sel_o, win_o)
            r = slice((k % 2) * blk, (k % 2 + 1) * blk)
            return (oc[k * blk:(k + 1) * blk] * sg[:, c:c + 1] + sel[r] * sg[:, c + 1:c + 2]
                    + win[r] * sg[:, c + 2:c + 3])

        outs = {NSA_HEAD_ORDER[k]: head_out(k) for k in range(NSA_HEADS)}
        z = z_ref[rows, :]
        zz = z * _sigmoid(z)
        o_ref[rows, 0:LANES] = (jnp.where(low, outs[0], outs[1]) * zz[:, 0:LANES]).astype(BF16)
        o_ref[rows, LANES:2 * LANES] = (jnp.where(low, outs[2], outs[3])
                                        * zz[:, LANES:2 * LANES]).astype(BF16)


def _nsa_constants(seq, blk):
    n_cmp = (seq - CMP_BLOCK) // CMP_STRIDE + 1
    n_sel = seq // SEL_BLOCK
    cs = np.arange(n_cmp) * CMP_STRIDE
    ce = cs + CMP_BLOCK - 1
    ss = np.arange(n_sel) * SEL_BLOCK
    se = ss + SEL_BLOCK - 1
    msel_t = np.zeros((n_sel, LANES), np.float32)
    msel_t[:, :n_cmp] = ((cs[:, None] <= se[None, :]) & (ce[:, None] >= ss[None, :])).T
    msel_t = np.concatenate([msel_t, msel_t], axis=1)
    key = np.arange(seq)
    kaug_w = np.zeros((seq, LANES), np.float32)
    kaug_w[:, POS_HI_LANE] = key // 16
    kaug_w[:, POS_LO_LANE] = key % 16
    kaug_s = kaug_w.copy()
    kaug_s[key, key // SEL_BLOCK] = PEN
    qslope = np.zeros((NSA_HEADS * blk, LANES), np.float32)
    for k, g in enumerate(NSA_HEAD_ORDER):
        qslope[k * blk:(k + 1) * blk, POS_HI_LANE] = 16.0 * NSA_SLOPES[g]
        qslope[k * blk:(k + 1) * blk, POS_LO_LANE] = NSA_SLOPES[g]
    return (jnp.asarray(msel_t, BF16), jnp.asarray(kaug_s, BF16), jnp.asarray(kaug_w, BF16),
            jnp.asarray(qslope, BF16))


def _nsa(pb3, pf3, pos_k, w1_k, w2_k, pos_v, w1_v, w2_v, blk=128):
    bsz, seq, _ = pb3.shape
    nchunk = seq // CMP_STRIDE
    cw = CMP_STRIDE * HEAD_DIM
    kvc = pf3[:, :, PF_KVC * LANES:(PF_KVC + 1) * LANES]
    xk = kvc[:, :, :HEAD_DIM].reshape(bsz, nchunk, cw)
    xv = kvc[:, :, HEAD_DIM:].reshape(bsz, nchunk, cw)

    def prep(pos, w1, w2):
        w1d = jnp.concatenate([w1, w1], axis=1).astype(BF16).reshape(2, cw, LANES)
        w2p = jnp.zeros((LANES, LANES), F32).at[:HEAD_DIM, :].set(
            jnp.concatenate([w2, w2], axis=1)).astype(BF16)
        return pos.reshape(2, cw), w1d, w2p

    consts = _nsa_constants(seq, blk)

    def col(base, nblk=1):
        return pl.BlockSpec((None, seq, nblk * LANES), lambda b: (b, 0, base // nblk))

    def whole(shape):
        return pl.BlockSpec(shape, lambda b: (0,) * len(shape))

    return pl.pallas_call(
        functools.partial(_nsa_kernel, seq=seq, blk=blk),
        grid=(bsz,),
        in_specs=[
            col(PB_NQ, 2), col(PB_NKS), col(PB_NVS), col(PB_NKW), col(PB_NVW),
            col(PF_NZ, 2), col(PF_MISC),
            pl.BlockSpec((None, nchunk, cw), lambda b: (b, 0, 0)),
            pl.BlockSpec((None, nchunk, cw), lambda b: (b, 0, 0)),
            whole((2, cw)), whole((2, cw, LANES)), whole((LANES, LANES)),
            whole((2, cw)), whole((2, cw, LANES)), whole((LANES, LANES)),
        ] + [whole(c.shape) for c in consts],
        out_specs=pl.BlockSpec((None, seq, NSA_W), lambda b: (b, 0, 0)),
        out_shape=jax.ShapeDtypeStruct((bsz, seq, NSA_W), BF16),
        scratch_shapes=[
            pltpu.VMEM((nchunk, LANES), BF16),
            pltpu.VMEM((nchunk, LANES), BF16),
            pltpu.VMEM((seq, 2 * LANES), BF16),
            pltpu.VMEM((seq, 2 * LANES), BF16),
            pltpu.VMEM((seq, LANES), BF16),
            pltpu.VMEM((seq, LANES), BF16),
            pltpu.VMEM((seq, LANES), BF16),
            pltpu.VMEM((seq, LANES), BF16),
        ],
        compiler_params=pltpu.CompilerParams(
            dimension_semantics=("arbitrary",), vmem_limit_bytes=VMEM_LIMIT),
        name="nsa_attn",
    )(pb3, pb3, pb3, pb3, pb3, pf3, pf3, xk, xv,
      *prep(pos_k, w1_k, w2_k), *prep(pos_v, w1_v, w2_v), *consts)


def kernel(x, norm_g, w_in, b_f, cmp_pos_k, cmp_w1_k, cmp_w2_k,
           cmp_pos_v, cmp_w1_v, cmp_w2_v, w_out, final_g):
    bsz, seq, d = x.shape
    xf = x.reshape(bsz * seq, d)
    for l in range(DEPTH):
        pb, pf = _inproj(xf, norm_g[l], _pack_w_in(w_in[l]))
        pb3 = pb.reshape(bsz, seq, NB_COLS)
        pf3 = pf.reshape(bsz, seq, NF_COLS)
        o_fox = _fox(pb3, pf3, b_f[l])
        o_sb = _sb(pb3, pf3)
        o_nsa = _nsa(pb3, pf3, cmp_pos_k[l], cmp_w1_k[l], cmp_w2_k[l],
                     cmp_pos_v[l], cmp_w1_v[l], cmp_w2_v[l])
        xf = _outproj(o_fox.reshape(bsz * seq, FOX_W), o_sb.reshape(bsz * seq, SB_W),
                      o_nsa.reshape(bsz * seq, NSA_W), xf, w_out[l].astype(BF16),
                      final_g, final=(l == DEPTH - 1))
    return xf.reshape(bsz, seq, d)
```

```python
import functools

import numpy as np
import jax
import jax.numpy as jnp
from jax import lax
from jax.experimental import pallas as pl
from jax.experimental.pallas import tpu as pltpu

F32 = jnp.float32
BF16 = jnp.bfloat16

D_MODEL = 1024
DEPTH = 2
HEAD_DIM = 64
LANES = 128
FOX_HEADS = 6
SB_HEADS = 6
NSA_HEADS = 4
FOX_W = FOX_HEADS * HEAD_DIM
SB_W = SB_HEADS * HEAD_DIM
NSA_W = NSA_HEADS * HEAD_DIM
CMP_BLOCK = 32
CMP_STRIDE = 16
SEL_BLOCK = 64
SEL_TOPK = 8
SEL_N_LOCAL = 2
WINDOW = 512
NORM_EPS = 1e-6
QK_SCALE = HEAD_DIM ** -0.5
NEG_BIG = -1e30
PEN = -(2.0 ** 100)

PB_FQ, PB_FK, PB_FV, PB_SQ, PB_SK, PB_SV, PB_NQ, PB_NKS, PB_NVS, PB_NKW, PB_NVW = (
    0, 3, 6, 9, 12, 15, 18, 20, 21, 22, 23)
PB_BLOCKS = 24
PF_FZ, PF_SZ, PF_NZ, PF_KVC, PF_MISC = 0, 3, 6, 8, 9
PF_BLOCKS = 10
NB_COLS = PB_BLOCKS * LANES
NF_COLS = PF_BLOCKS * LANES
MISC_NG = FOX_HEADS

VMEM_LIMIT = 56 * 1024 * 1024

_NT = (((1,), (1,)), ((), ()))


def _dot(a, b):
    return jnp.dot(a, b, preferred_element_type=F32)


def _dot_nt(a, b):
    return lax.dot_general(a, b, _NT, preferred_element_type=F32)


def _sigmoid(x):
    return 1.0 / (1.0 + jnp.exp(-x))


def _log_sigmoid(x):
    return -(jnp.maximum(-x, 0.0) + jnp.log1p(jnp.exp(-jnp.abs(x))))


def _split3(x):
    hi = x.astype(BF16)
    r = x - hi.astype(F32)
    mid = r.astype(BF16)
    lo = (r - mid.astype(F32)).astype(BF16)
    return hi, mid, lo


def _split2(x):
    hi = x.astype(BF16)
    lo = (x - hi.astype(F32)).astype(BF16)
    return hi, lo


def _rep(x, n):
    return x if n == 1 else jnp.concatenate([x] * n, axis=1)


def _inproj_kernel(x_ref, g_ref, w_ref, pb_ref, pf_ref):
    x = x_ref[...]
    ms = jnp.mean(x * x, axis=-1, keepdims=True)
    h = (x * lax.rsqrt(ms + NORM_EPS) * g_ref[...]).astype(BF16)
    cb = 512
    for c in range(0, NB_COLS, cb):
        pb_ref[:, c:c + cb] = _dot(h, w_ref[:, c:c + cb]).astype(BF16)
    cf = 640
    for c in range(0, NF_COLS, cf):
        pf_ref[:, c:c + cf] = _dot(h, w_ref[:, NB_COLS + c:NB_COLS + c + cf])


def _inproj(xf, g, w_all, tm=512):
    m, d = xf.shape
    return pl.pallas_call(
        _inproj_kernel,
        grid=(m // tm,),
        in_specs=[
            pl.BlockSpec((tm, d), lambda i: (i, 0)),
            pl.BlockSpec((1, d), lambda i: (0, 0)),
            pl.BlockSpec((d, NB_COLS + NF_COLS), lambda i: (0, 0)),
        ],
        out_specs=[
            pl.BlockSpec((tm, NB_COLS), lambda i: (i, 0)),
            pl.BlockSpec((tm, NF_COLS), lambda i: (i, 0)),
        ],
        out_shape=[
            jax.ShapeDtypeStruct((m, NB_COLS), BF16),
            jax.ShapeDtypeStruct((m, NF_COLS), F32),
        ],
        compiler_params=pltpu.CompilerParams(
            dimension_semantics=("arbitrary",), vmem_limit_bytes=VMEM_LIMIT),
        name="inproj",
    )(xf, g.reshape(1, d), w_all)


def _pack_w_in(w):
    sizes = (FOX_W, FOX_W, FOX_W, FOX_HEADS, FOX_W, SB_W, SB_W, SB_W, SB_W,
             NSA_W, HEAD_DIM, HEAD_DIM, HEAD_DIM, HEAD_DIM, HEAD_DIM, HEAD_DIM,
             3 * NSA_HEADS, NSA_W)
    offs = np.concatenate([[0], np.cumsum(sizes)])
    (fq, fk, fv, ff, fz, sq, sk, sv, sz, nq, nkc, nvc, nks, nvs, nkw, nvw, ng, nz) = [
        w[:, offs[i]:offs[i + 1]] for i in range(len(sizes))]
    pad = jnp.zeros((w.shape[0], LANES - FOX_HEADS - 3 * NSA_HEADS), w.dtype)
    cols = [fq * QK_SCALE, fk, fv, sq * QK_SCALE, sk, sv, nq * QK_SCALE,
            nks, nks, nvs, nvs, nkw, nkw, nvw, nvw,
            fz, sz, nz, nkc, nvc, ff, ng, pad]
    return jnp.concatenate(cols, axis=1).astype(BF16)


def _outproj_kernel(of_ref, os_ref, on_ref, x_ref, w_ref, g_ref, o_ref, *, final):
    y = (x_ref[...]
         + _dot(of_ref[...], w_ref[0:FOX_W, :])
         + _dot(os_ref[...], w_ref[FOX_W:FOX_W + SB_W, :])
         + _dot(on_ref[...], w_ref[FOX_W + SB_W:, :]))
    if final:
        ms = jnp.mean(y * y, axis=-1, keepdims=True)
        y = y * lax.rsqrt(ms + NORM_EPS) * g_ref[...]
    o_ref[...] = y


def _outproj(o_fox, o_sb, o_nsa, xf, w, g, final, tm=512):
    m, d = xf.shape
    return pl.pallas_call(
        functools.partial(_outproj_kernel, final=final),
        grid=(m // tm,),
        in_specs=[
            pl.BlockSpec((tm, FOX_W), lambda i: (i, 0)),
            pl.BlockSpec((tm, SB_W), lambda i: (i, 0)),
            pl.BlockSpec((tm, NSA_W), lambda i: (i, 0)),
            pl.BlockSpec((tm, d), lambda i: (i, 0)),
            pl.BlockSpec((d, d), lambda i: (0, 0)),
            pl.BlockSpec((1, d), lambda i: (0, 0)),
        ],
        out_specs=pl.BlockSpec((tm, d), lambda i: (i, 0)),
        out_shape=jax.ShapeDtypeStruct((m, d), F32),
        compiler_params=pltpu.CompilerParams(
            dimension_semantics=("arbitrary",), vmem_limit_bytes=VMEM_LIMIT),
        name="outproj_final" if final else "outproj",
    )(o_fox, o_sb, o_nsa, xf, w, g.reshape(1, d))


def _fox_place():
    place = np.zeros((9, LANES, LANES), np.float32)
    for t in range(3):
        place[t, 0, 3 + t] = -1.0
        place[t, 1, 9 + t] = -1.0
        place[3 + t, 0, t] = 1.0
        place[6 + t, 1, 6 + t] = 1.0
    ones = np.zeros((8, LANES), np.float32)
    ones[0, [0, 1, 2, 6, 7, 8]] = 1.0
    ones[1, [3, 4, 5]] = 1.0
    ones[2, [9, 10, 11]] = 1.0
    return jnp.asarray(place, BF16), jnp.asarray(ones, F32)


def _fox_kernel(q_ref, k_ref, v_ref, z_ref, misc_ref, bf_ref, tri_ref, place_ref, ones_ref,
                o_ref, qq_ref, kk_ref, vp_ref, *, seq, blk):
    pair = pl.program_id(1)
    lane1 = lax.broadcasted_iota(jnp.int32, (1, LANES), 1)
    low = lax.broadcasted_iota(jnp.int32, (blk, LANES), 1) < HEAD_DIM
    nblk = seq // blk

    tri = tri_ref[...]
    carry = jnp.zeros((1, LANES), F32)
    for b in range(nblk):
        rows = slice(b * blk, (b + 1) * blk)
        ls = _log_sigmoid(misc_ref[rows, :] + bf_ref[...])
        xa = jnp.sum(jnp.where(lane1 == 2 * pair, ls, 0.0), axis=1, keepdims=True)
        xb = jnp.sum(jnp.where(lane1 == 2 * pair + 1, ls, 0.0), axis=1, keepdims=True)
        xs = jnp.where(lane1 == 0, xa, jnp.where(lane1 == 1, xb, 0.0))
        hi, mid, lo = _split3(xs)
        cb = _dot(tri, hi) + _dot(tri, mid) + _dot(tri, lo) + carry
        carry = cb[blk - 1:blk, :]
        parts = _split3(cb)

        def placed(base, ones_row):
            out = ones_ref[ones_row:ones_row + 1, :]
            for t in range(3):
                out = out + _dot(parts[t], place_ref[base + t])
            return out.astype(BF16)

        q2 = q_ref[rows, :]
        v2 = v_ref[rows, :]
        zero = jnp.zeros_like(q2)
        one = jnp.ones_like(v2)
        kk_ref[rows, 0:LANES] = k_ref[rows, :]
        kk_ref[rows, LANES:2 * LANES] = placed(0, 0)
        qq_ref[0, rows, 0:LANES] = jnp.where(low, q2, zero)
        qq_ref[0, rows, LANES:2 * LANES] = placed(3, 1)
        qq_ref[1, rows, 0:LANES] = jnp.where(low, zero, q2)
        qq_ref[1, rows, LANES:2 * LANES] = placed(6, 2)
        vp_ref[0, rows, :] = jnp.where(low, v2, one)
        vp_ref[1, rows, :] = jnp.where(low, one, v2)

    causal = (lax.broadcasted_iota(jnp.int32, (blk, blk), 1)
              <= lax.broadcasted_iota(jnp.int32, (blk, blk), 0))

    for i in range(nblk):
        r0 = i * blk
        rows = slice(r0, r0 + blk)
        acc = []
        for h in range(2):
            qa = qq_ref[h, rows, :]
            sd = jnp.where(causal, _dot_nt(qa, kk_ref[rows, :]), -jnp.inf)
            m = jnp.max(sd, axis=1, keepdims=True)
            if i > 0:
                sm = _dot_nt(qa, kk_ref[0:r0, :])
                m = jnp.maximum(m, jnp.max(sm, axis=1, keepdims=True))
                a = (_dot(jnp.exp(sm - m).astype(BF16), vp_ref[h, 0:r0, :])
                     + _dot(jnp.exp(sd - m).astype(BF16), vp_ref[h, rows, :]))
            else:
                a = _dot(jnp.exp(sd - m).astype(BF16), vp_ref[h, rows, :])
            acc.append(a / pltpu.roll(a, HEAD_DIM, axis=1))
        o = jnp.where(low, acc[0], acc[1])
        z = z_ref[rows, :]
        o_ref[rows, :] = (o * (z * _sigmoid(z))).astype(BF16)


def _fox(pb3, pf3, b_f, blk=256):
    bsz, seq, _ = pb3.shape
    npair = FOX_HEADS // 2
    bias = jnp.zeros((1, LANES), F32).at[0, :FOX_HEADS].set(b_f)
    tri = jnp.asarray(np.tril(np.ones((blk, blk), np.float32)), BF16)
    place, ones = _fox_place()

    def col(base):
        return pl.BlockSpec((None, seq, LANES), lambda b, p: (b, 0, base + p))

    def whole(shape):
        return pl.BlockSpec(shape, lambda b, p: (0,) * len(shape))

    return pl.pallas_call(
        functools.partial(_fox_kernel, seq=seq, blk=blk),
        grid=(bsz, npair),
        in_specs=[
            col(PB_FQ), col(PB_FK), col(PB_FV), col(PF_FZ),
            pl.BlockSpec((None, seq, LANES), lambda b, p: (b, 0, PF_MISC)),
            whole((1, LANES)), whole((blk, blk)), whole(place.shape), whole(ones.shape),
        ],
        out_specs=pl.BlockSpec((None, seq, LANES), lambda b, p: (b, 0, p)),
        out_shape=jax.ShapeDtypeStruct((bsz, seq, FOX_W), BF16),
        scratch_shapes=[
            pltpu.VMEM((2, seq, 2 * LANES), BF16),
            pltpu.VMEM((seq, 2 * LANES), BF16),
            pltpu.VMEM((2, seq, LANES), BF16),
        ],
        compiler_params=pltpu.CompilerParams(
            dimension_semantics=("arbitrary", "arbitrary"), vmem_limit_bytes=VMEM_LIMIT),
        name="fox_attn",
    )(pb3, pb3, pb3, pf3, pf3, bias, tri, place, ones)


SB_NEAR_TILES = 2
SB_DEAD = -104.0


def _sb_kernel(q_ref, k_ref, v_ref, z_ref, suf_ref, o_ref, acc_ref, *, seq, blk):
    low = lax.broadcasted_iota(jnp.int32, (blk, LANES), 1) < HEAD_DIM
    strict = (lax.broadcasted_iota(jnp.int32, (blk, blk), 1)
              < lax.broadcasted_iota(jnp.int32, (blk, blk), 0))
    nrep = blk // LANES
    suf = suf_ref[...]

    def tiles(qh, i, j_hi, j_lo, carry):
        k0 = j_lo * blk
        z = _dot_nt(qh, k_ref[k0:(j_hi + 1) * blk, :])
        lsz = jnp.minimum(z, 0.0) - jnp.log(1.0 + jnp.exp(-jnp.abs(z)))
        l1m = lsz - z
        out = None
        for j in range(j_hi, j_lo - 1, -1):
            cs = slice(j * blk - k0, (j + 1) * blk - k0)
            l1 = l1m[:, cs]
            if j == i:
                l1 = jnp.where(strict, l1, 0.0)
            hi, lo = _split2(l1)
            r = _dot(jnp.concatenate([hi, lo], axis=1), suf)
            a = jnp.exp(lsz[:, cs] + (r[:, 0:blk] + _rep(carry, nrep)))
            if j == i:
                a = jnp.where(strict, a, 0.0)
            pv = _dot(a.astype(BF16), v_ref[j * blk:(j + 1) * blk, :])
            out = pv if out is None else out + pv
            carry = carry + r[:, blk:blk + LANES]
        return out, carry

    for i in range(seq // blk):
        rows = slice(i * blk, (i + 1) * blk)
        q2 = q_ref[rows, :]
        zero = jnp.zeros_like(q2)
        j_near = max(i - SB_NEAR_TILES + 1, 0)
        for h in range(2):
            qh = jnp.where(low, q2, zero) if h == 0 else jnp.where(low, zero, q2)
            out, carry = tiles(qh, i, i, j_near, jnp.zeros((blk, LANES), F32))
            acc_ref[h] = out
            if j_near > 0:
                @pl.when(jnp.max(carry) >= SB_DEAD)
                def _():
                    far, _ = tiles(qh, i, j_near - 1, 0, carry)
                    acc_ref[h] = acc_ref[h] + far
        o = jnp.where(low, acc_ref[0], acc_ref[1])
        zg = z_ref[rows, :]
        o_ref[rows, :] = (o * (zg * _sigmoid(zg))).astype(BF16)


def _sb(pb3, pf3, blk=256):
    bsz, seq, _ = pb3.shape
    npair = SB_HEADS // 2
    tstrict = np.tril(np.ones((blk, blk), np.float32), -1)
    t3 = np.concatenate([tstrict, np.ones((blk, LANES), np.float32)], axis=1)
    suf = jnp.asarray(np.concatenate([t3, t3], axis=0), BF16)

    def col(base):
        return pl.BlockSpec((None, seq, LANES), lambda b, p: (b, 0, base + p))

    return pl.pallas_call(
        functools.partial(_sb_kernel, seq=seq, blk=blk),
        grid=(bsz, npair),
        in_specs=[
            col(PB_SQ), col(PB_SK), col(PB_SV), col(PF_SZ),
            pl.BlockSpec(suf.shape, lambda b, p: (0, 0)),
        ],
        out_specs=pl.BlockSpec((None, seq, LANES), lambda b, p: (b, 0, p)),
        out_shape=jax.ShapeDtypeStruct((bsz, seq, SB_W), BF16),
        scratch_shapes=[pltpu.VMEM((2, blk, LANES), F32)],
        compiler_params=pltpu.CompilerParams(
            dimension_semantics=("arbitrary", "arbitrary"), vmem_limit_bytes=VMEM_LIMIT),
        name="sb_attn",
    )(pb3, pb3, pb3, pf3, suf)


NSA_HEAD_ORDER = (0, 2, 1, 3)
SEL_LANES = 32
POS_HI_LANE = 32
POS_LO_LANE = 33
NSA_SLOPES = tuple(2.0 ** (-8.0 * (g + 1) / NSA_HEADS) for g in range(NSA_HEADS))


def _nsa_kernel(q_ref, ks_ref, vs_ref, kw_ref, vw_ref, z_ref, misc_ref, xk_ref, xv_ref,
                posk_ref, w1k_ref, w2k_ref, posv_ref, w1v_ref, w2v_ref,
                mselt_ref, kaugs_ref, kaugw_ref, qslope_ref,
                o_ref, kc_ref, vc_ref, kks_ref, kkw_ref, vse_ref, vso_ref, vwe_ref, vwo_ref,
                *, seq, blk):
    rows4 = NSA_HEADS * blk
    half = rows4 // 2
    n_cmp = (seq - CMP_BLOCK) // CMP_STRIDE + 1
    n_sel = seq // SEL_BLOCK
    nwin = WINDOW // blk

    def compress(x_ref, pos_ref, w1_ref, w2_ref):
        x = x_ref[...]
        a = _dot((x + pos_ref[0:1, :]).astype(BF16), w1_ref[0])
        b = _dot((x + pos_ref[1:2, :]).astype(BF16), w1_ref[1])
        hid = a + pltpu.roll(b, b.shape[0] - 1, axis=0)
        hid = hid * _sigmoid(hid)
        return _dot(hid.astype(BF16), w2_ref[...]).astype(BF16)

    kc_ref[...] = compress(xk_ref, posk_ref, w1k_ref, w2k_ref)
    vc_ref[...] = compress(xv_ref, posv_ref, w1v_ref, w2v_ref)

    low_s = lax.broadcasted_iota(jnp.int32, (seq, LANES), 1) < HEAD_DIM
    kks_ref[:, 0:LANES] = ks_ref[...]
    kks_ref[:, LANES:2 * LANES] = kaugs_ref[...]
    kkw_ref[:, 0:LANES] = kw_ref[...]
    kkw_ref[:, LANES:2 * LANES] = kaugw_ref[...]
    vs = vs_ref[...]
    vw = vw_ref[...]
    one = jnp.ones_like(vs)
    vse_ref[...] = jnp.where(low_s, vs, one)
    vso_ref[...] = jnp.where(low_s, one, vs)
    vwe_ref[...] = jnp.where(low_s, vw, one)
    vwo_ref[...] = jnp.where(low_s, one, vw)

    low = lax.broadcasted_iota(jnp.int32, (blk, LANES), 1) < HEAD_DIM
    row4 = lax.broadcasted_iota(jnp.int32, (rows4, 1), 0)
    grp = row4 // blk
    r4 = row4 - grp * blk
    gslope = [NSA_SLOPES[g] for g in NSA_HEAD_ORDER]
    slope = jnp.where(grp == 0, gslope[0], jnp.where(grp == 1, gslope[1],
                      jnp.where(grp == 2, gslope[2], gslope[3]))).astype(F32)
    col_b = lax.broadcasted_iota(jnp.int32, (1, blk), 1)
    causal4 = col_b <= r4
    after4 = col_b > r4
    lane1 = lax.broadcasted_iota(jnp.int32, (1, LANES), 1)
    cmp_end = (lane1 * CMP_STRIDE + (CMP_BLOCK - 1)).astype(F32)
    jrow = lax.broadcasted_iota(jnp.int32, (n_sel, blk), 0)
    tcol = lax.broadcasted_iota(jnp.int32, (n_sel, blk), 1)
    eye = jnp.where(lax.broadcasted_iota(jnp.int32, (LANES, LANES), 0)
                    == lax.broadcasted_iota(jnp.int32, (LANES, LANES), 1), 1.0, 0.0).astype(BF16)
    sel_lane = lax.broadcasted_iota(jnp.int32, (rows4, LANES), 1) < SEL_LANES
    qslope = qslope_ref[...]

    def softmax_pv(parts, ve_ref, vo_ref):
        m = None
        for s, _ in parts:
            mp = jnp.max(s, axis=1, keepdims=True)
            m = mp if m is None else jnp.maximum(m, mp)
        acc_e = acc_o = None
        for s, ksl in parts:
            p = jnp.exp(s - m).astype(BF16)
            pe = _dot(p[0:half], ve_ref[ksl, :])
            po = _dot(p[half:rows4], vo_ref[ksl, :])
            acc_e = pe if acc_e is None else acc_e + pe
            acc_o = po if acc_o is None else acc_o + po
        return (acc_e / pltpu.roll(acc_e, HEAD_DIM, axis=1),
                acc_o / pltpu.roll(acc_o, HEAD_DIM, axis=1))

    for i in range(seq // blk):
        qs = i * blk
        rows = slice(qs, qs + blk)
        q01 = q_ref[rows, 0:LANES]
        q23 = q_ref[rows, LANES:2 * LANES]
        zero = jnp.zeros_like(q01)
        qst = jnp.concatenate([jnp.where(low, q01, zero), jnp.where(low, q23, zero),
                               jnp.where(low, zero, q01), jnp.where(low, zero, q23)], axis=0)
        tpos4 = (qs + r4).astype(F32)

        dist_c = tpos4 - cmp_end
        valid_c = (dist_c >= 0.0) & (lane1 < n_cmp)
        sc = _dot_nt(qst, kc_ref[...]) - slope * dist_c
        sc = jnp.where(valid_c, sc, -jnp.inf)
        mc = jnp.max(sc, axis=1, keepdims=True)
        mc = jnp.where(mc == -jnp.inf, 0.0, mc)
        pc = jnp.exp(sc - mc)
        ssum = jnp.sum(pc, axis=1, keepdims=True)
        pc = pc / jnp.where(ssum > 0.0, ssum, 1.0)
        oc = _dot(pc.astype(BF16), vc_ref[...])

        pcs = pc[0:blk] + pc[blk:2 * blk] + pc[2 * blk:3 * blk] + pc[3 * blk:4 * blk]
        hi, lo = _split2(pcs)
        imp = _dot_nt(mselt_ref[...], jnp.concatenate([hi, lo], axis=1))
        back = (qs + tcol) // SEL_BLOCK - jrow
        imp = jnp.where(back < 0, -jnp.inf,
                        jnp.where(back < SEL_N_LOCAL, jnp.inf,
                                  jnp.where(jrow == 0, jnp.inf, imp)))
        rank = jnp.zeros((n_sel, blk), F32)
        for c in range(n_sel):
            rowc = imp[c:c + 1, :]
            tie = jnp.where(jrow > c, 1.0, 0.0)
            rank = rank + jnp.where(rowc > imp, 1.0, jnp.where(rowc == imp, tie, 0.0))
        unsel_t = jnp.where(rank >= float(SEL_TOPK), 1.0, 0.0).astype(BF16)
        unsel_t = jnp.concatenate([unsel_t, jnp.zeros((LANES - n_sel, blk), BF16)], axis=0)
        unsel = _dot_nt(eye, unsel_t).astype(BF16)

        qaug = jnp.where(sel_lane, jnp.concatenate([unsel] * NSA_HEADS, axis=0), qslope)
        qop = jnp.concatenate([qst, qaug], axis=1)

        parts = []
        if i > 0:
            parts.append((_dot_nt(qop, kks_ref[0:qs, :]), slice(0, qs)))
        parts.append((jnp.where(causal4, _dot_nt(qop, kks_ref[rows, :]), -jnp.inf), rows))
        sel_e, sel_o = softmax_pv(parts, vse_ref, vso_ref)

        parts = []
        if i >= nwin:
            lsl = slice(qs - WINDOW, qs - WINDOW + blk)
            parts.append((jnp.where(after4, _dot_nt(qop, kkw_ref[lsl, :]), -jnp.inf), lsl))
        w0 = max(qs - WINDOW + blk, 0)
        if qs > w0:
            parts.append((_dot_nt(qop, kkw_ref[w0:qs, :]), slice(w0, qs)))
        parts.append((jnp.where(causal4, _dot_nt(qop, kkw_ref[rows, :]), -jnp.inf), rows))
        win_e, win_o = softmax_pv(parts, vwe_ref, vwo_ref)

        sg = _sigmoid(misc_ref[rows, :])

        def head_out(k):
            c = MISC_NG + 3 * NSA_HEAD_ORDER[k]
            sel, win = (sel_e, win_e) if k < 2 else (sel_o, win_o)
            r = slice((k % 2) * blk, (k % 2 + 1) * blk)
            return (oc[k * blk:(k + 1) * blk] * sg[:, c:c + 1] + sel[r] * sg[:, c + 1:c + 2]
                    + win[r] * sg[:, c + 2:c + 3])

        outs = {NSA_HEAD_ORDER[k]: head_out(k) for k in range(NSA_HEADS)}
        z = z_ref[rows, :]
        zz = z * _sigmoid(z)
        o_ref[rows, 0:LANES] = (jnp.where(low, outs[0], outs[1]) * zz[:, 0:LANES]).astype(BF16)
        o_ref[rows, LANES:2 * LANES] = (jnp.where(low, outs[2], outs[3])
                                        * zz[:, LANES:2 * LANES]).astype(BF16)


def _nsa_constants(seq, blk):
    n_cmp = (seq - CMP_BLOCK) // CMP_STRIDE + 1
    n_sel = seq // SEL_BLOCK
    cs = np.arange(n_cmp) * CMP_STRIDE
    ce = cs + CMP_BLOCK - 1
    ss = np.arange(n_sel) * SEL_BLOCK
    se = ss + SEL_BLOCK - 1
    msel_t = np.zeros((n_sel, LANES), np.float32)
    msel_t[:, :n_cmp] = ((cs[:, None] <= se[None, :]) & (ce[:, None] >= ss[None, :])).T
    msel_t = np.concatenate([msel_t, msel_t], axis=1)
    key = np.arange(seq)
    kaug_w = np.zeros((seq, LANES), np.float32)
    kaug_w[:, POS_HI_LANE] = key // 16
    kaug_w[:, POS_LO_LANE] = key % 16
    kaug_s = kaug_w.copy()
    kaug_s[key, key // SEL_BLOCK] = PEN
    qslope = np.zeros((NSA_HEADS * blk, LANES), np.float32)
    for k, g in enumerate(NSA_HEAD_ORDER):
        qslope[k * blk:(k + 1) * blk, POS_HI_LANE] = 16.0 * NSA_SLOPES[g]
        qslope[k * blk:(k + 1) * blk, POS_LO_LANE] = NSA_SLOPES[g]
    return (jnp.asarray(msel_t, BF16), jnp.asarray(kaug_s, BF16), jnp.asarray(kaug_w, BF16),
            jnp.asarray(qslope, BF16))


def _nsa(pb3, pf3, pos_k, w1_k, w2_k, pos_v, w1_v, w2_v, blk=128):
    bsz, seq, _ = pb3.shape
    nchunk = seq // CMP_STRIDE
    cw = CMP_STRIDE * HEAD_DIM
    kvc = pf3[:, :, PF_KVC * LANES:(PF_KVC + 1) * LANES]
    xk = kvc[:, :, :HEAD_DIM].reshape(bsz, nchunk, cw)
    xv = kvc[:, :, HEAD_DIM:].reshape(bsz, nchunk, cw)

    def prep(pos, w1, w2):
        w1d = jnp.concatenate([w1, w1], axis=1).astype(BF16).reshape(2, cw, LANES)
        w2p = jnp.zeros((LANES, LANES), F32).at[:HEAD_DIM, :].set(
            jnp.concatenate([w2, w2], axis=1)).astype(BF16)
        return pos.reshape(2, cw), w1d, w2p

    consts = _nsa_constants(seq, blk)

    def col(base, nblk=1):
        return pl.BlockSpec((None, seq, nblk * LANES), lambda b: (b, 0, base // nblk))

    def whole(shape):
        return pl.BlockSpec(shape, lambda b: (0,) * len(shape))

    return pl.pallas_call(
        functools.partial(_nsa_kernel, seq=seq, blk=blk),
        grid=(bsz,),
        in_specs=[
            col(PB_NQ, 2), col(PB_NKS), col(PB_NVS), col(PB_NKW), col(PB_NVW),
            col(PF_NZ, 2), col(PF_MISC),
            pl.BlockSpec((None, nchunk, cw), lambda b: (b, 0, 0)),
            pl.BlockSpec((None, nchunk, cw), lambda b: (b, 0, 0)),
            whole((2, cw)), whole((2, cw, LANES)), whole((LANES, LANES)),
            whole((2, cw)), whole((2, cw, LANES)), whole((LANES, LANES)),
        ] + [whole(c.shape) for c in consts],
        out_specs=pl.BlockSpec((None, seq, NSA_W), lambda b: (b, 0, 0)),
        out_shape=jax.ShapeDtypeStruct((bsz, seq, NSA_W), BF16),
        scratch_shapes=[
            pltpu.VMEM((nchunk, LANES), BF16),
            pltpu.VMEM((nchunk, LANES), BF16),
            pltpu.VMEM((seq, 2 * LANES), BF16),
            pltpu.VMEM((seq, 2 * LANES), BF16),
            pltpu.VMEM((seq, LANES), BF16),
            pltpu.VMEM((seq, LANES), BF16),
            pltpu.VMEM((seq, LANES), BF16),
            pltpu.VMEM((seq, LANES), BF16),
        ],
        compiler_params=pltpu.CompilerParams(
            dimension_semantics=("arbitrary",), vmem_limit_bytes=VMEM_LIMIT),
        name="nsa_attn",
    )(pb3, pb3, pb3, pb3, pb3, pf3, pf3, xk, xv,
      *prep(pos_k, w1_k, w2_k), *prep(pos_v, w1_v, w2_v), *consts)


def kernel(x, norm_g, w_in, b_f, cmp_pos_k, cmp_w1_k, cmp_w2_k,
           cmp_pos_v, cmp_w1_v, cmp_w2_v, w_out, final_g):
    bsz, seq, d = x.shape
    xf = x.reshape(bsz * seq, d)
    for l in range(DEPTH):
        pb, pf = _inproj(xf, norm_g[l], _pack_w_in(w_in[l]))
        pb3 = pb.reshape(bsz, seq, NB_COLS)
        pf3 = pf.reshape(bsz, seq, NF_COLS)
        o_fox = _fox(pb3, pf3, b_f[l])
        o_sb = _sb(pb3, pf3)
        o_nsa = _nsa(pb3, pf3, cmp_pos_k[l], cmp_w1_k[l], cmp_w2_k[l],
                     cmp_pos_v[l], cmp_w1_v[l], cmp_w2_v[l])
        xf = _outproj(o_fox.reshape(bsz * seq, FOX_W), o_sb.reshape(bsz * seq, SB_W),
                      o_nsa.reshape(bsz * seq, NSA_W), xf, w_out[l].astype(BF16),
                      final_g, final=(l == DEPTH - 1))
    return xf.reshape(bsz, seq, d)
```

```python
import functools

import numpy as np
import jax
import jax.numpy as jnp
from jax import lax
from jax.experimental import pallas as pl
from jax.experimental.pallas import tpu as pltpu

F32 = jnp.float32
BF16 = jnp.bfloat16

D_MODEL = 1024
DEPTH = 2
HEAD_DIM = 64
LANES = 128
FOX_HEADS = 6
SB_HEADS = 6
NSA_HEADS = 4
FOX_W = FOX_HEADS * HEAD_DIM
SB_W = SB_HEADS * HEAD_DIM
NSA_W = NSA_HEADS * HEAD_DIM
CMP_BLOCK = 32
CMP_STRIDE = 16
SEL_BLOCK = 64
SEL_TOPK = 8
SEL_N_LOCAL = 2
WINDOW = 512
NORM_EPS = 1e-6
QK_SCALE = HEAD_DIM ** -0.5
NEG_BIG = -1e30
PEN = -(2.0 ** 100)

PB_FQ, PB_FK, PB_FV, PB_SQ, PB_SK, PB_SV, PB_NQ, PB_NKS, PB_NVS, PB_NKW, PB_NVW = (
    0, 3, 6, 9, 12, 15, 18, 20, 21, 22, 23)
PB_BLOCKS = 24
PF_FZ, PF_SZ, PF_NZ, PF_KVC, PF_MISC = 0, 3, 6, 8, 9
PF_BLOCKS = 10
NB_COLS = PB_BLOCKS * LANES
NF_COLS = PF_BLOCKS * LANES
MISC_NG = FOX_HEADS

VMEM_LIMIT = 56 * 1024 * 1024

_NT = (((1,), (1,)), ((), ()))


def _dot(a, b):
    return jnp.dot(a, b, preferred_element_type=F32)


def _dot_nt(a, b):
    return lax.dot_general(a, b, _NT, preferred_element_type=F32)


def _sigmoid(x):
    return 1.0 / (1.0 + jnp.exp(-x))


def _log_sigmoid(x):
    return -(jnp.maximum(-x, 0.0) + jnp.log1p(jnp.exp(-jnp.abs(x))))


def _split3(x):
    hi = x.astype(BF16)
    r = x - hi.astype(F32)
    mid = r.astype(BF16)
    lo = (r - mid.astype(F32)).astype(BF16)
    return hi, mid, lo


def _split2(x):
    hi = x.astype(BF16)
    lo = (x - hi.astype(F32)).astype(BF16)
    return hi, lo


def _rep(x, n):
    return x if n == 1 else jnp.concatenate([x] * n, axis=1)


def _inproj_kernel(x_ref, g_ref, w_ref, pb_ref, pf_ref):
    x = x_ref[...]
    ms = jnp.mean(x * x, axis=-1, keepdims=True)
    h = (x * lax.rsqrt(ms + NORM_EPS) * g_ref[...]).astype(BF16)
    cb = 512
    for c in range(0, NB_COLS, cb):
        pb_ref[:, c:c + cb] = _dot(h, w_ref[:, c:c + cb]).astype(BF16)
    cf = 640
    for c in range(0, NF_COLS, cf):
        pf_ref[:, c:c + cf] = _dot(h, w_ref[:, NB_COLS + c:NB_COLS + c + cf])


def _inproj(xf, g, w_all, tm=512):
    m, d = xf.shape
    return pl.pallas_call(
        _inproj_kernel,
        grid=(m // tm,),
        in_specs=[
            pl.BlockSpec((tm, d), lambda i: (i, 0)),
            pl.BlockSpec((1, d), lambda i: (0, 0)),
            pl.BlockSpec((d, NB_COLS + NF_COLS), lambda i: (0, 0)),
        ],
        out_specs=[
            pl.BlockSpec((tm, NB_COLS), lambda i: (i, 0)),
            pl.BlockSpec((tm, NF_COLS), lambda i: (i, 0)),
        ],
        out_shape=[
            jax.ShapeDtypeStruct((m, NB_COLS), BF16),
            jax.ShapeDtypeStruct((m, NF_COLS), F32),
        ],
        compiler_params=pltpu.CompilerParams(
            dimension_semantics=("arbitrary",), vmem_limit_bytes=VMEM_LIMIT),
        name="inproj",
    )(xf, g.reshape(1, d), w_all)


def _pack_w_in(w):
    sizes = (FOX_W, FOX_W, FOX_W, FOX_HEADS, FOX_W, SB_W, SB_W, SB_W, SB_W,
             NSA_W, HEAD_DIM, HEAD_DIM, HEAD_DIM, HEAD_DIM, HEAD_DIM, HEAD_DIM,
             3 * NSA_HEADS, NSA_W)
    offs = np.concatenate([[0], np.cumsum(sizes)])
    (fq, fk, fv, ff, fz, sq, sk, sv, sz, nq, nkc, nvc, nks, nvs, nkw, nvw, ng, nz) = [
        w[:, offs[i]:offs[i + 1]] for i in range(len(sizes))]
    pad = jnp.zeros((w.shape[0], LANES - FOX_HEADS - 3 * NSA_HEADS), w.dtype)
    cols = [fq * QK_SCALE, fk, fv, sq * QK_SCALE, sk, sv, nq * QK_SCALE,
            nks, nks, nvs, nvs, nkw, nkw, nvw, nvw,
            fz, sz, nz, nkc, nvc, ff, ng, pad]
    return jnp.concatenate(cols, axis=1).astype(BF16)


def _outproj_kernel(of_ref, os_ref, on_ref, x_ref, w_ref, g_ref, o_ref, *, final):
    y = (x_ref[...]
         + _dot(of_ref[...], w_ref[0:FOX_W, :])
         + _dot(os_ref[...], w_ref[FOX_W:FOX_W + SB_W, :])
         + _dot(on_ref[...], w_ref[FOX_W + SB_W:, :]))
    if final:
        ms = jnp.mean(y * y, axis=-1, keepdims=True)
        y = y * lax.rsqrt(ms + NORM_EPS) * g_ref[...]
    o_ref[...] = y


def _outproj(o_fox, o_sb, o_nsa, xf, w, g, final, tm=512):
    m, d = xf.shape
    return pl.pallas_call(
        functools.partial(_outproj_kernel, final=final),
        grid=(m // tm,),
        in_specs=[
            pl.BlockSpec((tm, FOX_W), lambda i: (i, 0)),
            pl.BlockSpec((tm, SB_W), lambda i: (i, 0)),
            pl.BlockSpec((tm, NSA_W), lambda i: (i, 0)),
            pl.BlockSpec((tm, d), lambda i: (i, 0)),
            pl.BlockSpec((d, d), lambda i: (0, 0)),
            pl.BlockSpec((1, d), lambda i: (0, 0)),
        ],
        out_specs=pl.BlockSpec((tm, d), lambda i: (i, 0)),
        out_shape=jax.ShapeDtypeStruct((m, d), F32),
        compiler_params=pltpu.CompilerParams(
            dimension_semantics=("arbitrary",), vmem_limit_bytes=VMEM_LIMIT),
        name="outproj_final" if final else "outproj",
    )(o_fox, o_sb, o_nsa, xf, w, g.reshape(1, d))


def _fox_place():
    place = np.zeros((FOX_HEADS // 2, 3 * LANES, 3 * LANES), np.float32)
    for p in range(FOX_HEADS // 2):
        for t in range(3):
            place[p, t * LANES + 2 * p, 3 + t] = -1.0
            place[p, t * LANES + 2 * p + 1, 9 + t] = -1.0
            place[p, t * LANES + 2 * p, LANES + t] = 1.0
            place[p, t * LANES + 2 * p + 1, 2 * LANES + 6 + t] = 1.0
    ones = np.zeros((1, 3 * LANES), np.float32)
    ones[0, [0, 1, 2, 6, 7, 8]] = 1.0
    ones[0, [LANES + 3, LANES + 4, LANES + 5]] = 1.0
    ones[0, [2 * LANES + 9, 2 * LANES + 10, 2 * LANES + 11]] = 1.0
    return jnp.asarray(place, BF16), jnp.asarray(ones, F32)


def _fox_kernel(q_ref, k_ref, v_ref, z_ref, misc_ref, bf_ref, tri_ref, place_ref, ones_ref,
                o_ref, qq_ref, kk_ref, vp_ref, cparts_ref, *, seq, blk):
    pair = pl.program_id(1)
    low = lax.broadcasted_iota(jnp.int32, (blk, LANES), 1) < HEAD_DIM
    nblk = seq // blk

    @pl.when(pair == 0)
    def _():
        tri = tri_ref[...]
        carry = jnp.zeros((1, LANES), F32)
        for b in range(nblk):
            rows = slice(b * blk, (b + 1) * blk)
            ls = _log_sigmoid(misc_ref[rows, :] + bf_ref[...])
            c3 = _dot(tri, jnp.concatenate(_split3(ls), axis=1))
            cb = c3[:, 0:LANES] + c3[:, LANES:2 * LANES] + c3[:, 2 * LANES:3 * LANES] + carry
            carry = cb[blk - 1:blk, :]
            cparts_ref[rows, :] = jnp.concatenate(_split3(cb), axis=1)

    place = place_ref[pair]
    for b in range(nblk):
        rows = slice(b * blk, (b + 1) * blk)
        aug = (_dot(cparts_ref[rows, :], place) + ones_ref[...]).astype(BF16)

        q2 = q_ref[rows, :]
        v2 = v_ref[rows, :]
        zero = jnp.zeros_like(q2)
        one = jnp.ones_like(v2)
        kk_ref[rows, 0:LANES] = k_ref[rows, :]
        kk_ref[rows, LANES:2 * LANES] = aug[:, 0:LANES]
        qq_ref[0, rows, 0:LANES] = jnp.where(low, q2, zero)
        qq_ref[0, rows, LANES:2 * LANES] = aug[:, LANES:2 * LANES]
        qq_ref[1, rows, 0:LANES] = jnp.where(low, zero, q2)
        qq_ref[1, rows, LANES:2 * LANES] = aug[:, 2 * LANES:3 * LANES]
        vp_ref[0, rows, :] = jnp.where(low, v2, one)
        vp_ref[1, rows, :] = jnp.where(low, one, v2)

    causal = (lax.broadcasted_iota(jnp.int32, (blk, blk), 1)
              <= lax.broadcasted_iota(jnp.int32, (blk, blk), 0))

    for i in range(nblk):
        r0 = i * blk
        rows = slice(r0, r0 + blk)
        acc = []
        for h in range(2):
            qa = qq_ref[h, rows, :]
            sd = jnp.where(causal, _dot_nt(qa, kk_ref[rows, :]), -jnp.inf)
            m = jnp.max(sd, axis=1, keepdims=True)
            if i > 0:
                sm = _dot_nt(qa, kk_ref[0:r0, :])
                m = jnp.maximum(m, jnp.max(sm, axis=1, keepdims=True))
                a = (_dot(jnp.exp(sm - m).astype(BF16), vp_ref[h, 0:r0, :])
                     + _dot(jnp.exp(sd - m).astype(BF16), vp_ref[h, rows, :]))
            else:
                a = _dot(jnp.exp(sd - m).astype(BF16), vp_ref[h, rows, :])
            acc.append(a / pltpu.roll(a, HEAD_DIM, axis=1))
        o = jnp.where(low, acc[0], acc[1])
        z = z_ref[rows, :]
        o_ref[rows, :] = (o * (z * _sigmoid(z))).astype(BF16)


def _fox(pb3, pf3, b_f, blk=256):
    bsz, seq, _ = pb3.shape
    npair = FOX_HEADS // 2
    bias = jnp.zeros((1, LANES), F32).at[0, :FOX_HEADS].set(b_f)
    tri = jnp.asarray(np.tril(np.ones((blk, blk), np.float32)), BF16)
    place, ones = _fox_place()

    def col(base):
        return pl.BlockSpec((None, seq, LANES), lambda b, p: (b, 0, base + p))

    def whole(shape):
        return pl.BlockSpec(shape, lambda b, p: (0,) * len(shape))

    return pl.pallas_call(
        functools.partial(_fox_kernel, seq=seq, blk=blk),
        grid=(bsz, npair),
        in_specs=[
            col(PB_FQ), col(PB_FK), col(PB_FV), col(PF_FZ),
            pl.BlockSpec((None, seq, LANES), lambda b, p: (b, 0, PF_MISC)),
            whole((1, LANES)), whole((blk, blk)), whole(place.shape), whole(ones.shape),
        ],
        out_specs=pl.BlockSpec((None, seq, LANES), lambda b, p: (b, 0, p)),
        out_shape=jax.ShapeDtypeStruct((bsz, seq, FOX_W), BF16),
        scratch_shapes=[
            pltpu.VMEM((2, seq, 2 * LANES), BF16),
            pltpu.VMEM((seq, 2 * LANES), BF16),
            pltpu.VMEM((2, seq, LANES), BF16),
            pltpu.VMEM((seq, 3 * LANES), BF16),
        ],
        compiler_params=pltpu.CompilerParams(
            dimension_semantics=("arbitrary", "arbitrary"), vmem_limit_bytes=VMEM_LIMIT),
        name="fox_attn",
    )(pb3, pb3, pb3, pf3, pf3, bias, tri, place, ones)


SB_NEAR_TILES = 2
SB_DEAD = -104.0


def _sb_kernel(q_ref, k_ref, v_ref, z_ref, suf_ref, o_ref, acc_ref, carry_ref, *, seq, blk):
    low = lax.broadcasted_iota(jnp.int32, (blk, LANES), 1) < HEAD_DIM
    strict = (lax.broadcasted_iota(jnp.int32, (blk, blk), 1)
              < lax.broadcasted_iota(jnp.int32, (blk, blk), 0))
    suf = suf_ref[...]

    def tiles(qh, i, j_hi, j_lo, carry):
        k0 = j_lo * blk
        z = _dot_nt(qh, k_ref[k0:(j_hi + 1) * blk, :])
        lsz = jnp.minimum(z, 0.0) - jnp.log(1.0 + jnp.exp(-jnp.abs(z)))
        l1m = lsz - z
        out = None
        for j in range(j_hi, j_lo - 1, -1):
            cs = slice(j * blk - k0, (j + 1) * blk - k0)
            l1 = l1m[:, cs]
            if j == i:
                l1 = jnp.where(strict, l1, 0.0)
            r = _dot(l1.astype(BF16), suf)
            a = jnp.exp(lsz[:, cs] + (r + carry))
            if j == i:
                a = jnp.where(strict, a, 0.0)
            pv = _dot(a.astype(BF16), v_ref[j * blk:(j + 1) * blk, :])
            out = pv if out is None else out + pv
            carry = carry + (r[:, 0:1] + l1[:, 0:1])
        return out, carry

    nblk = seq // blk

    def q_head(i, h):
        q2 = q_ref[i * blk:(i + 1) * blk, :]
        zero = jnp.zeros_like(q2)
        return jnp.where(low, q2, zero) if h == 0 else jnp.where(low, zero, q2)

    alive = {}
    for i in range(nblk):
        j_near = max(i - SB_NEAR_TILES + 1, 0)
        for h in range(2):
            out, carry = tiles(q_head(i, h), i, i, j_near, jnp.zeros((blk, 1), F32))
            acc_ref[i, h] = out
            if j_near > 0:
                carry_ref[i, h] = jnp.broadcast_to(carry, (blk, LANES))
                alive[i, h] = jnp.max(carry) >= SB_DEAD

    for (i, h), flag in alive.items():
        @pl.when(flag)
        def _(i=i, h=h):
            j_near = max(i - SB_NEAR_TILES + 1, 0)
            far, _ = tiles(q_head(i, h), i, j_near - 1, 0, carry_ref[i, h][:, 0:1])
            acc_ref[i, h] = acc_ref[i, h] + far

    for i in range(nblk):
        rows = slice(i * blk, (i + 1) * blk)
        o = jnp.where(low, acc_ref[i, 0], acc_ref[i, 1])
        zg = z_ref[rows, :]
        o_ref[rows, :] = (o * (zg * _sigmoid(zg))).astype(BF16)


def _sb(pb3, pf3, blk=256):
    bsz, seq, _ = pb3.shape
    npair = SB_HEADS // 2
    suf = jnp.asarray(np.tril(np.ones((blk, blk), np.float32), -1), BF16)

    def col(base):
        return pl.BlockSpec((None, seq, LANES), lambda b, p: (b, 0, base + p))

    return pl.pallas_call(
        functools.partial(_sb_kernel, seq=seq, blk=blk),
        grid=(bsz, npair),
        in_specs=[
            col(PB_SQ), col(PB_SK), col(PB_SV), col(PF_SZ),
            pl.BlockSpec(suf.shape, lambda b, p: (0, 0)),
        ],
        out_specs=pl.BlockSpec((None, seq, LANES), lambda b, p: (b, 0, p)),
        out_shape=jax.ShapeDtypeStruct((bsz, seq, SB_W), BF16),
        scratch_shapes=[pltpu.VMEM((seq // blk, 2, blk, LANES), F32),
                        pltpu.VMEM((seq // blk, 2, blk, LANES), F32)],
        compiler_params=pltpu.CompilerParams(
            dimension_semantics=("arbitrary", "arbitrary"), vmem_limit_bytes=VMEM_LIMIT),
        name="sb_attn",
    )(pb3, pb3, pb3, pf3, suf)


NSA_HEAD_ORDER = (0, 2, 1, 3)
SEL_LANES = 32
POS_HI_LANE = 32
POS_LO_LANE = 33
NSA_SLOPES = tuple(2.0 ** (-8.0 * (g + 1) / NSA_HEADS) for g in range(NSA_HEADS))


def _nsa_kernel(q_ref, ks_ref, vs_ref, kw_ref, vw_ref, z_ref, misc_ref, xk_ref, xv_ref,
                posk_ref, w1k_ref, w2k_ref, posv_ref, w1v_ref, w2v_ref,
                mselt_ref, kaugs_ref, kaugw_ref, qslope_ref,
                o_ref, kc_ref, vc_ref, kks_ref, kkw_ref, vse_ref, vso_ref, vwe_ref, vwo_ref,
                *, seq, blk):
    rows4 = NSA_HEADS * blk
    half = rows4 // 2
    n_cmp = (seq - CMP_BLOCK) // CMP_STRIDE + 1
    n_sel = seq // SEL_BLOCK
    nwin = WINDOW // blk

    def compress(x_ref, pos_ref, w1_ref, w2_ref):
        x = x_ref[...]
        a = _dot((x + pos_ref[0:1, :]).astype(BF16), w1_ref[0])
        b = _dot((x + pos_ref[1:2, :]).astype(BF16), w1_ref[1])
        hid = a + pltpu.roll(b, b.shape[0] - 1, axis=0)
        hid = hid * _sigmoid(hid)
        return _dot(hid.astype(BF16), w2_ref[...]).astype(BF16)

    kc_ref[...] = compress(xk_ref, posk_ref, w1k_ref, w2k_ref)
    vc_ref[...] = compress(xv_ref, posv_ref, w1v_ref, w2v_ref)

    low_s = lax.broadcasted_iota(jnp.int32, (seq, LANES), 1) < HEAD_DIM
    kks_ref[:, 0:LANES] = ks_ref[...]
    kks_ref[:, LANES:2 * LANES] = kaugs_ref[...]
    kkw_ref[:, 0:LANES] = kw_ref[...]
    kkw_ref[:, LANES:2 * LANES] = kaugw_ref[...]
    vs = vs_ref[...]
    vw = vw_ref[...]
    one = jnp.ones_like(vs)
    vse_ref[...] = jnp.where(low_s, vs, one)
    vso_ref[...] = jnp.where(low_s, one, vs)
    vwe_ref[...] = jnp.where(low_s, vw, one)
    vwo_ref[...] = jnp.where(low_s, one, vw)

    low = lax.broadcasted_iota(jnp.int32, (blk, LANES), 1) < HEAD_DIM
    row4 = lax.broadcasted_iota(jnp.int32, (rows4, 1), 0)
    grp = row4 // blk
    r4 = row4 - grp * blk
    gslope = [NSA_SLOPES[g] for g in NSA_HEAD_ORDER]
    slope = jnp.where(grp == 0, gslope[0], jnp.where(grp == 1, gslope[1],
                      jnp.where(grp == 2, gslope[2], gslope[3]))).astype(F32)
    col_b = lax.broadcasted_iota(jnp.int32, (1, blk), 1)
    causal4 = col_b <= r4
    after4 = col_b > r4
    lane1 = lax.broadcasted_iota(jnp.int32, (1, LANES), 1)
    cmp_end = (lane1 * CMP_STRIDE + (CMP_BLOCK - 1)).astype(F32)
    jrow = lax.broadcasted_iota(jnp.int32, (n_sel, blk), 0)
    tcol = lax.broadcasted_iota(jnp.int32, (n_sel, blk), 1)
    eye = jnp.where(lax.broadcasted_iota(jnp.int32, (LANES, LANES), 0)
                    == lax.broadcasted_iota(jnp.int32, (LANES, LANES), 1), 1.0, 0.0).astype(BF16)
    sel_lane = lax.broadcasted_iota(jnp.int32, (rows4, LANES), 1) < SEL_LANES
    qslope = qslope_ref[...]

    def softmax_pv(parts, ve_ref, vo_ref):
        m = None
        for s, _ in parts:
            mp = jnp.max(s, axis=1, keepdims=True)
            m = mp if m is None else jnp.maximum(m, mp)
        acc_e = acc_o = None
        for s, ksl in parts:
            p = jnp.exp(s - m).astype(BF16)
            pe = _dot(p[0:half], ve_ref[ksl, :])
            po = _dot(p[half:rows4], vo_ref[ksl, :])
            acc_e = pe if acc_e is None else acc_e + pe
            acc_o = po if acc_o is None else acc_o + po
        return (acc_e / pltpu.roll(acc_e, HEAD_DIM, axis=1),
                acc_o / pltpu.roll(acc_o, HEAD_DIM, axis=1))

    for i in range(seq // blk):
        qs = i * blk
        rows = slice(qs, qs + blk)
        q01 = q_ref[rows, 0:LANES]
        q23 = q_ref[rows, LANES:2 * LANES]
        zero = jnp.zeros_like(q01)
        qst = jnp.concatenate([jnp.where(low, q01, zero), jnp.where(low, q23, zero),
                               jnp.where(low, zero, q01), jnp.where(low, zero, q23)], axis=0)
        tpos4 = (qs + r4).astype(F32)

        dist_c = tpos4 - cmp_end
        valid_c = (dist_c >= 0.0) & (lane1 < n_cmp)
        sc = _dot_nt(qst, kc_ref[...]) - slope * dist_c
        sc = jnp.where(valid_c, sc, -jnp.inf)
        mc = jnp.max(sc, axis=1, keepdims=True)
        mc = jnp.where(mc == -jnp.inf, 0.0, mc)
        pc = jnp.exp(sc - mc)
        ssum = jnp.sum(pc, axis=1, keepdims=True)
        pc = pc / jnp.where(ssum > 0.0, ssum, 1.0)
        oc = _dot(pc.astype(BF16), vc_ref[...])

        pcs = pc[0:blk] + pc[blk:2 * blk] + pc[2 * blk:3 * blk] + pc[3 * blk:4 * blk]
        hi, lo = _split2(pcs)
        imp = _dot_nt(mselt_ref[...], jnp.concatenate([hi, lo], axis=1))
        back = (qs + tcol) // SEL_BLOCK - jrow
        imp = jnp.where(back < 0, -jnp.inf,
                        jnp.where(back < SEL_N_LOCAL, jnp.inf,
                                  jnp.where(jrow == 0, jnp.inf, imp)))
        rank = jnp.zeros((n_sel, blk), F32)
        for c in range(n_sel):
            rowc = imp[c:c + 1, :]
            tie = jnp.where(jrow > c, 1.0, 0.0)
            rank = rank + jnp.where(rowc > imp, 1.0, jnp.where(rowc == imp, tie, 0.0))
        unsel_t = jnp.where(rank >= float(SEL_TOPK), 1.0, 0.0).astype(BF16)
        unsel_t = jnp.concatenate([unsel_t, jnp.zeros((LANES - n_sel, blk), BF16)], axis=0)
        unsel = _dot_nt(eye, unsel_t).astype(BF16)

        qaug = jnp.where(sel_lane, jnp.concatenate([unsel] * NSA_HEADS, axis=0), qslope)
        qop = jnp.concatenate([qst, qaug], axis=1)

        parts = []
        if i > 0:
            parts.append((_dot_nt(qop, kks_ref[0:qs, :]), slice(0, qs)))
        parts.append((jnp.where(causal4, _dot_nt(qop, kks_ref[rows, :]), -jnp.inf), rows))
        sel_e, sel_o = softmax_pv(parts, vse_ref, vso_ref)

        parts = []
        if i >= nwin:
            lsl = slice(qs - WINDOW, qs - WINDOW + blk)
            parts.append((jnp.where(after4, _dot_nt(qop, kkw_ref[lsl, :]), -jnp.inf), lsl))
        w0 = max(qs - WINDOW + blk, 0)
        if qs > w0:
            parts.append((_dot_nt(qop, kkw_ref[w0:qs, :]), slice(w0, qs)))
        parts.append((jnp.where(causal4, _dot_nt(qop, kkw_ref[rows, :]), -jnp.inf), rows))
        win_e, win_o = softmax_pv(parts, vwe_ref, vwo_ref)

        sg = _sigmoid(misc_ref[rows, :])

        def head_out(k):
            c = MISC_NG + 3 * NSA_HEAD_ORDER[k]
            sel, win = (sel_e, win_e) if k < 2 else (sel_o, win_o)
            r = slice((k % 2) * blk, (k % 2 + 1) * blk)
            return (oc[k * blk:(k + 1) * blk] * sg[:, c:c + 1] + sel[r] * sg[:, c + 1:c + 2]
                    + win[r] * sg[:, c + 2:c + 3])

        outs = {NSA_HEAD_ORDER[k]: head_out(k) for k in range(NSA_HEADS)}
        z = z_ref[rows, :]
        zz = z * _sigmoid(z)
        o_ref[rows, 0:LANES] = (jnp.where(low, outs[0], outs[1]) * zz[:, 0:LANES]).astype(BF16)
        o_ref[rows, LANES:2 * LANES] = (jnp.where(low, outs[2], outs[3])
                                        * zz[:, LANES:2 * LANES]).astype(BF16)


def _nsa_constants(seq, blk):
    n_cmp = (seq - CMP_BLOCK) // CMP_STRIDE + 1
    n_sel = seq // SEL_BLOCK
    cs = np.arange(n_cmp) * CMP_STRIDE
    ce = cs + CMP_BLOCK - 1
    ss = np.arange(n_sel) * SEL_BLOCK
    se = ss + SEL_BLOCK - 1
    msel_t = np.zeros((n_sel, LANES), np.float32)
    msel_t[:, :n_cmp] = ((cs[:, None] <= se[None, :]) & (ce[:, None] >= ss[None, :])).T
    msel_t = np.concatenate([msel_t, msel_t], axis=1)
    key = np.arange(seq)
    kaug_w = np.zeros((seq, LANES), np.float32)
    kaug_w[:, POS_HI_LANE] = key // 16
    kaug_w[:, POS_LO_LANE] = key % 16
    kaug_s = kaug_w.copy()
    kaug_s[key, key // SEL_BLOCK] = PEN
    qslope = np.zeros((NSA_HEADS * blk, LANES), np.float32)
    for k, g in enumerate(NSA_HEAD_ORDER):
        qslope[k * blk:(k + 1) * blk, POS_HI_LANE] = 16.0 * NSA_SLOPES[g]
        qslope[k * blk:(k + 1) * blk, POS_LO_LANE] = NSA_SLOPES[g]
    return (jnp.asarray(msel_t, BF16), jnp.asarray(kaug_s, BF16), jnp.asarray(kaug_w, BF16),
            jnp.asarray(qslope, BF16))


def _nsa(pb3, pf3, pos_k, w1_k, w2_k, pos_v, w1_v, w2_v, blk=128):
    bsz, seq, _ = pb3.shape
    nchunk = seq // CMP_STRIDE
    cw = CMP_STRIDE * HEAD_DIM
    kvc = pf3[:, :, PF_KVC * LANES:(PF_KVC + 1) * LANES]
    xk = kvc[:, :, :HEAD_DIM].reshape(bsz, nchunk, cw)
    xv = kvc[:, :, HEAD_DIM:].reshape(bsz, nchunk, cw)

    def prep(pos, w1, w2):
        w1d = jnp.concatenate([w1, w1], axis=1).astype(BF16).reshape(2, cw, LANES)
        w2p = jnp.zeros((LANES, LANES), F32).at[:HEAD_DIM, :].set(
            jnp.concatenate([w2, w2], axis=1)).astype(BF16)
        return pos.reshape(2, cw), w1d, w2p

    consts = _nsa_constants(seq, blk)

    def col(base, nblk=1):
        return pl.BlockSpec((None, seq, nblk * LANES), lambda b: (b, 0, base // nblk))

    def whole(shape):
        return pl.BlockSpec(shape, lambda b: (0,) * len(shape))

    return pl.pallas_call(
        functools.partial(_nsa_kernel, seq=seq, blk=blk),
        grid=(bsz,),
        in_specs=[
            col(PB_NQ, 2), col(PB_NKS), col(PB_NVS), col(PB_NKW), col(PB_NVW),
            col(PF_NZ, 2), col(PF_MISC),
            pl.BlockSpec((None, nchunk, cw), lambda b: (b, 0, 0)),
            pl.BlockSpec((None, nchunk, cw), lambda b: (b, 0, 0)),
            whole((2, cw)), whole((2, cw, LANES)), whole((LANES, LANES)),
            whole((2, cw)), whole((2, cw, LANES)), whole((LANES, LANES)),
        ] + [whole(c.shape) for c in consts],
        out_specs=pl.BlockSpec((None, seq, NSA_W), lambda b: (b, 0, 0)),
        out_shape=jax.ShapeDtypeStruct((bsz, seq, NSA_W), BF16),
        scratch_shapes=[
            pltpu.VMEM((nchunk, LANES), BF16),
            pltpu.VMEM((nchunk, LANES), BF16),
            pltpu.VMEM((seq, 2 * LANES), BF16),
            pltpu.VMEM((seq, 2 * LANES), BF16),
            pltpu.VMEM((seq, LANES), BF16),
            pltpu.VMEM((seq, LANES), BF16),
            pltpu.VMEM((seq, LANES), BF16),
            pltpu.VMEM((seq, LANES), BF16),
        ],
        compiler_params=pltpu.CompilerParams(
            dimension_semantics=("arbitrary",), vmem_limit_bytes=VMEM_LIMIT),
        name="nsa_attn",
    )(pb3, pb3, pb3, pb3, pb3, pf3, pf3, xk, xv,
      *prep(pos_k, w1_k, w2_k), *prep(pos_v, w1_v, w2_v), *consts)


def kernel(x, norm_g, w_in, b_f, cmp_pos_k, cmp_w1_k, cmp_w2_k,
           cmp_pos_v, cmp_w1_v, cmp_w2_v, w_out, final_g):
    bsz, seq, d = x.shape
    xf = x.reshape(bsz * seq, d)
    for l in range(DEPTH):
        pb, pf = _inproj(xf, norm_g[l], _pack_w_in(w_in[l]))
        pb3 = pb.reshape(bsz, seq, NB_COLS)
        pf3 = pf.reshape(bsz, seq, NF_COLS)
        o_fox = _fox(pb3, pf3, b_f[l])
        o_sb = _sb(pb3, pf3)
        o_nsa = _nsa(pb3, pf3, cmp_pos_k[l], cmp_w1_k[l], cmp_w2_k[l],
                     cmp_pos_v[l], cmp_w1_v[l], cmp_w2_v[l])
        xf = _outproj(o_fox.reshape(bsz * seq, FOX_W), o_sb.reshape(bsz * seq, SB_W),
                      o_nsa.reshape(bsz * seq, NSA_W), xf, w_out[l].astype(BF16),
                      final_g, final=(l == DEPTH - 1))
    return xf.reshape(bsz, seq, d)
```

```python
import functools

import numpy as np
import jax
import jax.numpy as jnp
from jax import lax
from jax.experimental import pallas as pl
from jax.experimental.pallas import tpu as pltpu

F32 = jnp.float32
BF16 = jnp.bfloat16

D_MODEL = 1024
DEPTH = 2
HEAD_DIM = 64
LANES = 128
FOX_HEADS = 6
SB_HEADS = 6
NSA_HEADS = 4
FOX_W = FOX_HEADS * HEAD_DIM
SB_W = SB_HEADS * HEAD_DIM
NSA_W = NSA_HEADS * HEAD_DIM
CMP_BLOCK = 32
CMP_STRIDE = 16
SEL_BLOCK = 64
SEL_TOPK = 8
SEL_N_LOCAL = 2
WINDOW = 512
NORM_EPS = 1e-6
QK_SCALE = HEAD_DIM ** -0.5
NEG_BIG = -1e30
PEN = -(2.0 ** 100)

PB_FQ, PB_FK, PB_FV, PB_SQ, PB_SK, PB_SV, PB_NQ, PB_NKS, PB_NVS, PB_NKW, PB_NVW = (
    0, 3, 6, 9, 12, 15, 18, 20, 21, 22, 23)
PB_BLOCKS = 24
PF_FZ, PF_SZ, PF_NZ, PF_KVC, PF_MISC = 0, 3, 6, 8, 9
PF_BLOCKS = 10
NB_COLS = PB_BLOCKS * LANES
NF_COLS = PF_BLOCKS * LANES
MISC_NG = FOX_HEADS

VMEM_PHYSICAL = 64 * 1024 * 1024
VMEM_LIMIT = VMEM_PHYSICAL - 4 * 1024 * 1024

_NT = (((1,), (1,)), ((), ()))


def _dot(a, b):
    return jnp.dot(a, b, preferred_element_type=F32)


def _dot_nt(a, b):
    return lax.dot_general(a, b, _NT, preferred_element_type=F32)


def _sigmoid(x):
    return 1.0 / (1.0 + jnp.exp(-x))


def _log_sigmoid(x):
    return -(jnp.maximum(-x, 0.0) + jnp.log1p(jnp.exp(-jnp.abs(x))))


def _split3(x):
    hi = x.astype(BF16)
    r = x - hi.astype(F32)
    mid = r.astype(BF16)
    lo = (r - mid.astype(F32)).astype(BF16)
    return hi, mid, lo


def _split2(x):
    hi = x.astype(BF16)
    lo = (x - hi.astype(F32)).astype(BF16)
    return hi, lo


def _rep(x, n):
    return x if n == 1 else jnp.concatenate([x] * n, axis=1)


PIPELINE_DEPTH = 4


def _emit_pipelined(items, first, second, depth=PIPELINE_DEPTH):
    pending = {}
    for n, item in enumerate(items):
        pending[item] = first(*item)
        if n >= depth:
            prev = items[n - depth]
            second(*prev, pending.pop(prev))
    for prev in items[max(len(items) - depth, 0):]:
        second(*prev, pending.pop(prev))


def _inproj_kernel(x_ref, g_ref, w_ref, pb_ref, pf_ref):
    x = x_ref[...]
    ms = jnp.mean(x * x, axis=-1, keepdims=True)
    h = (x * lax.rsqrt(ms + NORM_EPS) * g_ref[...]).astype(BF16)
    cb = 512
    for c in range(0, NB_COLS, cb):
        pb_ref[:, c:c + cb] = _dot(h, w_ref[:, c:c + cb]).astype(BF16)
    cf = 640
    for c in range(0, NF_COLS, cf):
        pf_ref[:, c:c + cf] = _dot(h, w_ref[:, NB_COLS + c:NB_COLS + c + cf])


def _inproj(xf, g, w_all, tm=512):
    m, d = xf.shape
    return pl.pallas_call(
        _inproj_kernel,
        grid=(m // tm,),
        in_specs=[
            pl.BlockSpec((tm, d), lambda i: (i, 0)),
            pl.BlockSpec((1, d), lambda i: (0, 0)),
            pl.BlockSpec((d, NB_COLS + NF_COLS), lambda i: (0, 0)),
        ],
        out_specs=[
            pl.BlockSpec((tm, NB_COLS), lambda i: (i, 0)),
            pl.BlockSpec((tm, NF_COLS), lambda i: (i, 0)),
        ],
        out_shape=[
            jax.ShapeDtypeStruct((m, NB_COLS), BF16),
            jax.ShapeDtypeStruct((m, NF_COLS), F32),
        ],
        compiler_params=pltpu.CompilerParams(
            dimension_semantics=("arbitrary",), vmem_limit_bytes=VMEM_LIMIT),
        name="inproj",
    )(xf, g.reshape(1, d), w_all)


def _pack_w_in(w):
    sizes = (FOX_W, FOX_W, FOX_W, FOX_HEADS, FOX_W, SB_W, SB_W, SB_W, SB_W,
             NSA_W, HEAD_DIM, HEAD_DIM, HEAD_DIM, HEAD_DIM, HEAD_DIM, HEAD_DIM,
             3 * NSA_HEADS, NSA_W)
    offs = np.concatenate([[0], np.cumsum(sizes)])
    (fq, fk, fv, ff, fz, sq, sk, sv, sz, nq, nkc, nvc, nks, nvs, nkw, nvw, ng, nz) = [
        w[:, offs[i]:offs[i + 1]] for i in range(len(sizes))]
    pad = jnp.zeros((w.shape[0], LANES - FOX_HEADS - 3 * NSA_HEADS), w.dtype)
    cols = [fq * QK_SCALE, fk, fv, sq * QK_SCALE, sk, sv, nq * QK_SCALE,
            nks, nks, nvs, nvs, nkw, nkw, nvw, nvw,
            fz, sz, nz, nkc, nvc, ff, ng, pad]
    return jnp.concatenate(cols, axis=1).astype(BF16)


def _outproj_kernel(of_ref, os_ref, on_ref, x_ref, w_ref, g_ref, o_ref, *, final):
    y = (x_ref[...]
         + _dot(of_ref[...], w_ref[0:FOX_W, :])
         + _dot(os_ref[...], w_ref[FOX_W:FOX_W + SB_W, :])
         + _dot(on_ref[...], w_ref[FOX_W + SB_W:, :]))
    if final:
        ms = jnp.mean(y * y, axis=-1, keepdims=True)
        y = y * lax.rsqrt(ms + NORM_EPS) * g_ref[...]
    o_ref[...] = y


def _outproj(o_fox, o_sb, o_nsa, xf, w, g, final, tm=512):
    m, d = xf.shape
    return pl.pallas_call(
        functools.partial(_outproj_kernel, final=final),
        grid=(m // tm,),
        in_specs=[
            pl.BlockSpec((tm, FOX_W), lambda i: (i, 0)),
            pl.BlockSpec((tm, SB_W), lambda i: (i, 0)),
            pl.BlockSpec((tm, NSA_W), lambda i: (i, 0)),
            pl.BlockSpec((tm, d), lambda i: (i, 0)),
            pl.BlockSpec((d, d), lambda i: (0, 0)),
            pl.BlockSpec((1, d), lambda i: (0, 0)),
        ],
        out_specs=pl.BlockSpec((tm, d), lambda i: (i, 0)),
        out_shape=jax.ShapeDtypeStruct((m, d), F32),
        compiler_params=pltpu.CompilerParams(
            dimension_semantics=("arbitrary",), vmem_limit_bytes=VMEM_LIMIT),
        name="outproj_final" if final else "outproj",
    )(o_fox, o_sb, o_nsa, xf, w, g.reshape(1, d))


def _fox_place():
    place = np.zeros((FOX_HEADS // 2, 3 * LANES, 3 * LANES), np.float32)
    for p in range(FOX_HEADS // 2):
        for t in range(3):
            place[p, t * LANES + 2 * p, 3 + t] = -1.0
            place[p, t * LANES + 2 * p + 1, 9 + t] = -1.0
            place[p, t * LANES + 2 * p, LANES + t] = 1.0
            place[p, t * LANES + 2 * p + 1, 2 * LANES + 6 + t] = 1.0
    ones = np.zeros((1, 3 * LANES), np.float32)
    ones[0, [0, 1, 2, 6, 7, 8]] = 1.0
    ones[0, [LANES + 3, LANES + 4, LANES + 5]] = 1.0
    ones[0, [2 * LANES + 9, 2 * LANES + 10, 2 * LANES + 11]] = 1.0
    return jnp.asarray(place, BF16), jnp.asarray(ones, F32)


def _fox_kernel(q_ref, k_ref, v_ref, z_ref, misc_ref, bf_ref, tri_ref, place_ref, ones_ref,
                o_ref, qq_ref, kk_ref, vp_ref, cparts_ref, *, seq, blk):
    pair = pl.program_id(1)
    low = lax.broadcasted_iota(jnp.int32, (blk, LANES), 1) < HEAD_DIM
    nblk = seq // blk

    @pl.when(pair == 0)
    def _():
        tri = tri_ref[...]
        carry = jnp.zeros((1, LANES), F32)
        for b in range(nblk):
            rows = slice(b * blk, (b + 1) * blk)
            ls = _log_sigmoid(misc_ref[rows, :] + bf_ref[...])
            c3 = _dot(tri, jnp.concatenate(_split3(ls), axis=1))
            cb = c3[:, 0:LANES] + c3[:, LANES:2 * LANES] + c3[:, 2 * LANES:3 * LANES] + carry
            carry = cb[blk - 1:blk, :]
            cparts_ref[rows, :] = jnp.concatenate(_split3(cb), axis=1)

    place = place_ref[pair]
    for b in range(nblk):
        rows = slice(b * blk, (b + 1) * blk)
        aug = (_dot(cparts_ref[rows, :], place) + ones_ref[...]).astype(BF16)

        q2 = q_ref[rows, :]
        v2 = v_ref[rows, :]
        zero = jnp.zeros_like(q2)
        one = jnp.ones_like(v2)
        kk_ref[rows, 0:LANES] = k_ref[rows, :]
        kk_ref[rows, LANES:2 * LANES] = aug[:, 0:LANES]
        qq_ref[0, rows, 0:LANES] = jnp.where(low, q2, zero)
        qq_ref[0, rows, LANES:2 * LANES] = aug[:, LANES:2 * LANES]
        qq_ref[1, rows, 0:LANES] = jnp.where(low, zero, q2)
        qq_ref[1, rows, LANES:2 * LANES] = aug[:, 2 * LANES:3 * LANES]
        vp_ref[0, rows, :] = jnp.where(low, v2, one)
        vp_ref[1, rows, :] = jnp.where(low, one, v2)

    causal = (lax.broadcasted_iota(jnp.int32, (blk, blk), 1)
              <= lax.broadcasted_iota(jnp.int32, (blk, blk), 0))

    def scores(i, h):
        r0 = i * blk
        rows = slice(r0, r0 + blk)
        qa = qq_ref[h, rows, :]
        sd = jnp.where(causal, _dot_nt(qa, kk_ref[rows, :]), -jnp.inf)
        m = jnp.max(sd, axis=1, keepdims=True)
        sm = None
        if i > 0:
            sm = _dot_nt(qa, kk_ref[0:r0, :])
            m = jnp.maximum(m, jnp.max(sm, axis=1, keepdims=True))
        return sd, sm, m

    outs = {}

    def weighted_sum(i, h, st):
        sd, sm, m = st
        rows = slice(i * blk, (i + 1) * blk)
        a = _dot(jnp.exp(sd - m).astype(BF16), vp_ref[h, rows, :])
        if i > 0:
            a = a + _dot(jnp.exp(sm - m).astype(BF16), vp_ref[h, 0:i * blk, :])
        outs[i, h] = a / pltpu.roll(a, HEAD_DIM, axis=1)
        if h == 1:
            o = jnp.where(low, outs.pop((i, 0)), outs.pop((i, 1)))
            z = z_ref[rows, :]
            o_ref[rows, :] = (o * (z * _sigmoid(z))).astype(BF16)

    _emit_pipelined([(i, h) for i in range(nblk) for h in range(2)], scores, weighted_sum)


def _fox(pb3, pf3, b_f, blk=256):
    bsz, seq, _ = pb3.shape
    npair = FOX_HEADS // 2
    bias = jnp.zeros((1, LANES), F32).at[0, :FOX_HEADS].set(b_f)
    tri = jnp.asarray(np.tril(np.ones((blk, blk), np.float32)), BF16)
    place, ones = _fox_place()

    def col(base):
        return pl.BlockSpec((None, seq, LANES), lambda b, p: (b, 0, base + p))

    def whole(shape):
        return pl.BlockSpec(shape, lambda b, p: (0,) * len(shape))

    return pl.pallas_call(
        functools.partial(_fox_kernel, seq=seq, blk=blk),
        grid=(bsz, npair),
        in_specs=[
            col(PB_FQ), col(PB_FK), col(PB_FV), col(PF_FZ),
            pl.BlockSpec((None, seq, LANES), lambda b, p: (b, 0, PF_MISC)),
            whole((1, LANES)), whole((blk, blk)), whole(place.shape), whole(ones.shape),
        ],
        out_specs=pl.BlockSpec((None, seq, LANES), lambda b, p: (b, 0, p)),
        out_shape=jax.ShapeDtypeStruct((bsz, seq, FOX_W), BF16),
        scratch_shapes=[
            pltpu.VMEM((2, seq, 2 * LANES), BF16),
            pltpu.VMEM((seq, 2 * LANES), BF16),
            pltpu.VMEM((2, seq, LANES), BF16),
            pltpu.VMEM((seq, 3 * LANES), BF16),
        ],
        compiler_params=pltpu.CompilerParams(
            dimension_semantics=("arbitrary", "arbitrary"), vmem_limit_bytes=VMEM_LIMIT),
        name="fox_attn",
    )(pb3, pb3, pb3, pf3, pf3, bias, tri, place, ones)


SB_NEAR_TILES = 2
SB_DEAD = -104.0


def _sb_kernel(q_ref, k_ref, v_ref, z_ref, suf_ref, o_ref, acc_ref, carry_ref, *, seq, blk):
    low = lax.broadcasted_iota(jnp.int32, (blk, LANES), 1) < HEAD_DIM
    strict = (lax.broadcasted_iota(jnp.int32, (blk, blk), 1)
              < lax.broadcasted_iota(jnp.int32, (blk, blk), 0))
    suf = suf_ref[...]

    def log_terms(qh, j_hi, j_lo):
        z = _dot_nt(qh, k_ref[j_lo * blk:(j_hi + 1) * blk, :])
        lsz = jnp.minimum(z, 0.0) - jnp.log(1.0 + jnp.exp(-jnp.abs(z)))
        return lsz, lsz - z

    def tiles(i, j_hi, j_lo, carry, terms):
        lsz, l1m = terms
        k0 = j_lo * blk
        out = None
        for j in range(j_hi, j_lo - 1, -1):
            cs = slice(j * blk - k0, (j + 1) * blk - k0)
            l1 = l1m[:, cs]
            if j == i:
                l1 = jnp.where(strict, l1, 0.0)
            r = _dot(l1.astype(BF16), suf)
            a = jnp.exp(lsz[:, cs] + (r + carry))
            if j == i:
                a = jnp.where(strict, a, 0.0)
            pv = _dot(a.astype(BF16), v_ref[j * blk:(j + 1) * blk, :])
            out = pv if out is None else out + pv
            carry = carry + (r[:, 0:1] + l1[:, 0:1])
        return out, carry

    nblk = seq // blk

    def q_head(i, h):
        q2 = q_ref[i * blk:(i + 1) * blk, :]
        zero = jnp.zeros_like(q2)
        return jnp.where(low, q2, zero) if h == 0 else jnp.where(low, zero, q2)

    def j_near(i):
        return max(i - SB_NEAR_TILES + 1, 0)

    alive = {}

    def near_terms(i, h):
        return log_terms(q_head(i, h), i, j_near(i))

    def near_tiles(i, h, terms):
        out, carry = tiles(i, i, j_near(i), jnp.zeros((blk, 1), F32), terms)
        acc_ref[i, h] = out
        if j_near(i) > 0:
            carry_ref[i, h] = jnp.broadcast_to(carry, (blk, LANES))
            alive[i, h] = jnp.max(carry) >= SB_DEAD

    _emit_pipelined([(i, h) for i in range(nblk) for h in range(2)], near_terms, near_tiles)

    for (i, h), flag in alive.items():
        @pl.when(flag)
        def _(i=i, h=h):
            far, _ = tiles(i, j_near(i) - 1, 0, carry_ref[i, h][:, 0:1],
                           log_terms(q_head(i, h), j_near(i) - 1, 0))
            acc_ref[i, h] = acc_ref[i, h] + far

    for i in range(nblk):
        rows = slice(i * blk, (i + 1) * blk)
        o = jnp.where(low, acc_ref[i, 0], acc_ref[i, 1])
        zg = z_ref[rows, :]
        o_ref[rows, :] = (o * (zg * _sigmoid(zg))).astype(BF16)


def _sb(pb3, pf3, blk=256):
    bsz, seq, _ = pb3.shape
    npair = SB_HEADS // 2
    suf = jnp.asarray(np.tril(np.ones((blk, blk), np.float32), -1), BF16)

    def col(base):
        return pl.BlockSpec((None, seq, LANES), lambda b, p: (b, 0, base + p))

    return pl.pallas_call(
        functools.partial(_sb_kernel, seq=seq, blk=blk),
        grid=(bsz, npair),
        in_specs=[
            col(PB_SQ), col(PB_SK), col(PB_SV), col(PF_SZ),
            pl.BlockSpec(suf.shape, lambda b, p: (0, 0)),
        ],
        out_specs=pl.BlockSpec((None, seq, LANES), lambda b, p: (b, 0, p)),
        out_shape=jax.ShapeDtypeStruct((bsz, seq, SB_W), BF16),
        scratch_shapes=[pltpu.VMEM((seq // blk, 2, blk, LANES), F32),
                        pltpu.VMEM((seq // blk, 2, blk, LANES), F32)],
        compiler_params=pltpu.CompilerParams(
            dimension_semantics=("arbitrary", "arbitrary"), vmem_limit_bytes=VMEM_LIMIT),
        name="sb_attn",
    )(pb3, pb3, pb3, pf3, suf)


NSA_HEAD_ORDER = (0, 2, 1, 3)
SEL_LANES = 32
POS_HI_LANE = 32
POS_LO_LANE = 33
NSA_SLOPES = tuple(2.0 ** (-8.0 * (g + 1) / NSA_HEADS) for g in range(NSA_HEADS))


def _nsa_kernel(q_ref, ks_ref, vs_ref, kw_ref, vw_ref, z_ref, misc_ref, xk_ref, xv_ref,
                posk_ref, w1k_ref, w2k_ref, posv_ref, w1v_ref, w2v_ref,
                mselt_ref, kaugs_ref, kaugw_ref, qslope_ref,
                o_ref, kc_ref, vc_ref, kks_ref, kkw_ref, vse_ref, vso_ref, vwe_ref, vwo_ref,
                *, seq, blk):
    rows4 = NSA_HEADS * blk
    half = rows4 // 2
    n_cmp = (seq - CMP_BLOCK) // CMP_STRIDE + 1
    n_sel = seq // SEL_BLOCK
    nwin = WINDOW // blk

    def compress(x_ref, pos_ref, w1_ref, w2_ref):
        x = x_ref[...]
        a = _dot((x + pos_ref[0:1, :]).astype(BF16), w1_ref[0])
        b = _dot((x + pos_ref[1:2, :]).astype(BF16), w1_ref[1])
        hid = a + pltpu.roll(b, b.shape[0] - 1, axis=0)
        hid = hid * _sigmoid(hid)
        return _dot(hid.astype(BF16), w2_ref[...]).astype(BF16)

    kc_ref[...] = compress(xk_ref, posk_ref, w1k_ref, w2k_ref)
    vc_ref[...] = compress(xv_ref, posv_ref, w1v_ref, w2v_ref)

    low_s = lax.broadcasted_iota(jnp.int32, (seq, LANES), 1) < HEAD_DIM
    kks_ref[:, 0:LANES] = ks_ref[...]
    kks_ref[:, LANES:2 * LANES] = kaugs_ref[...]
    kkw_ref[:, 0:LANES] = kw_ref[...]
    kkw_ref[:, LANES:2 * LANES] = kaugw_ref[...]
    vs = vs_ref[...]
    vw = vw_ref[...]
    one = jnp.ones_like(vs)
    vse_ref[...] = jnp.where(low_s, vs, one)
    vso_ref[...] = jnp.where(low_s, one, vs)
    vwe_ref[...] = jnp.where(low_s, vw, one)
    vwo_ref[...] = jnp.where(low_s, one, vw)

    low = lax.broadcasted_iota(jnp.int32, (blk, LANES), 1) < HEAD_DIM
    row4 = lax.broadcasted_iota(jnp.int32, (rows4, 1), 0)
    grp = row4 // blk
    r4 = row4 - grp * blk
    gslope = [NSA_SLOPES[g] for g in NSA_HEAD_ORDER]
    slope = jnp.where(grp == 0, gslope[0], jnp.where(grp == 1, gslope[1],
                      jnp.where(grp == 2, gslope[2], gslope[3]))).astype(F32)
    col_b = lax.broadcasted_iota(jnp.int32, (1, blk), 1)
    causal4 = col_b <= r4
    after4 = col_b > r4
    lane1 = lax.broadcasted_iota(jnp.int32, (1, LANES), 1)
    cmp_end = (lane1 * CMP_STRIDE + (CMP_BLOCK - 1)).astype(F32)
    jrow = lax.broadcasted_iota(jnp.int32, (n_sel, blk), 0)
    tcol = lax.broadcasted_iota(jnp.int32, (n_sel, blk), 1)
    eye = jnp.where(lax.broadcasted_iota(jnp.int32, (LANES, LANES), 0)
                    == lax.broadcasted_iota(jnp.int32, (LANES, LANES), 1), 1.0, 0.0).astype(BF16)
    sel_lane = lax.broadcasted_iota(jnp.int32, (rows4, LANES), 1) < SEL_LANES
    qslope = qslope_ref[...]

    def row_max(parts):
        m = None
        for s in parts:
            mp = jnp.max(s, axis=1, keepdims=True)
            m = mp if m is None else jnp.maximum(m, mp)
        return m

    def softmax_pv(parts, m, keys, ve_ref, vo_ref):
        p = [jnp.exp(s - m).astype(BF16) for s in parts]
        p = p[0] if len(p) == 1 else jnp.concatenate(p, axis=1)
        acc_e = _dot(p[0:half], ve_ref[keys, :])
        acc_o = _dot(p[half:rows4], vo_ref[keys, :])
        return (acc_e / pltpu.roll(acc_e, HEAD_DIM, axis=1),
                acc_o / pltpu.roll(acc_o, HEAD_DIM, axis=1))

    def scores(i):
        qs = i * blk
        rows = slice(qs, qs + blk)
        q01 = q_ref[rows, 0:LANES]
        q23 = q_ref[rows, LANES:2 * LANES]
        zero = jnp.zeros_like(q01)
        qst = jnp.concatenate([jnp.where(low, q01, zero), jnp.where(low, q23, zero),
                               jnp.where(low, zero, q01), jnp.where(low, zero, q23)], axis=0)
        tpos4 = (qs + r4).astype(F32)

        dist_c = tpos4 - cmp_end
        valid_c = (dist_c >= 0.0) & (lane1 < n_cmp)
        sc = _dot_nt(qst, kc_ref[...]) - slope * dist_c
        sc = jnp.where(valid_c, sc, -jnp.inf)
        mc = jnp.max(sc, axis=1, keepdims=True)
        mc = jnp.where(mc == -jnp.inf, 0.0, mc)
        pc = jnp.exp(sc - mc)
        ssum = jnp.sum(pc, axis=1, keepdims=True)
        pc = pc / jnp.where(ssum > 0.0, ssum, 1.0)
        oc = _dot(pc.astype(BF16), vc_ref[...])

        pcs = pc[0:blk] + pc[blk:2 * blk] + pc[2 * blk:3 * blk] + pc[3 * blk:4 * blk]
        hi, lo = _split2(pcs)
        imp = _dot_nt(mselt_ref[...], jnp.concatenate([hi, lo], axis=1))
        back = (qs + tcol) // SEL_BLOCK - jrow
        imp = jnp.where(back < 0, -jnp.inf,
                        jnp.where(back < SEL_N_LOCAL, jnp.inf,
                                  jnp.where(jrow == 0, jnp.inf, imp)))
        rank = jnp.zeros((n_sel, blk), F32)
        for c in range(n_sel):
            rowc = imp[c:c + 1, :]
            tie = jnp.where(jrow > c, 1.0, 0.0)
            rank = rank + jnp.where(rowc > imp, 1.0, jnp.where(rowc == imp, tie, 0.0))
        unsel_t = jnp.where(rank >= float(SEL_TOPK), 1.0, 0.0).astype(BF16)
        unsel_t = jnp.concatenate([unsel_t, jnp.zeros((LANES - n_sel, blk), BF16)], axis=0)
        unsel = _dot_nt(eye, unsel_t).astype(BF16)

        qop_w = jnp.concatenate([qst, qslope], axis=1)
        qaug = jnp.where(sel_lane, jnp.concatenate([unsel] * NSA_HEADS, axis=0), qslope)
        qop_s = jnp.concatenate([qst, qaug], axis=1)

        win = []
        w0 = max(qs - WINDOW, 0)
        if i >= nwin:
            win.append(jnp.where(after4, _dot_nt(qop_w, kkw_ref[w0:w0 + blk, :]), -jnp.inf))
        wm = max(qs - WINDOW + blk, 0)
        if qs > wm:
            win.append(_dot_nt(qop_w, kkw_ref[wm:qs, :]))
        win.append(jnp.where(causal4, _dot_nt(qop_w, kkw_ref[rows, :]), -jnp.inf))

        sel = []
        if i > 0:
            sel.append(_dot_nt(qop_s, kks_ref[0:qs, :]))
        sel.append(jnp.where(causal4, _dot_nt(qop_s, kks_ref[rows, :]), -jnp.inf))
        return oc, win, row_max(win), sel, row_max(sel)

    def outputs(i, st):
        oc, win, m_win, sel, m_sel = st
        qs = i * blk
        rows = slice(qs, qs + blk)
        win_e, win_o = softmax_pv(win, m_win, slice(max(qs - WINDOW, 0), qs + blk),
                                  vwe_ref, vwo_ref)
        sel_e, sel_o = softmax_pv(sel, m_sel, slice(0, qs + blk), vse_ref, vso_ref)

        sg = _sigmoid(misc_ref[rows, :])

        def head_out(k):
            c = MISC_NG + 3 * NSA_HEAD_ORDER[k]
            bs, bw = (sel_e, win_e) if k < 2 else (sel_o, win_o)
            r = slice((k % 2) * blk, (k % 2 + 1) * blk)
            return (oc[k * blk:(k + 1) * blk] * sg[:, c:c + 1] + bs[r] * sg[:, c + 1:c + 2]
                    + bw[r] * sg[:, c + 2:c + 3])

        outs = {NSA_HEAD_ORDER[k]: head_out(k) for k in range(NSA_HEADS)}
        z = z_ref[rows, :]
        zz = z * _sigmoid(z)
        o_ref[rows, 0:LANES] = (jnp.where(low, outs[0], outs[1]) * zz[:, 0:LANES]).astype(BF16)
        o_ref[rows, LANES:2 * LANES] = (jnp.where(low, outs[2], outs[3])
                                        * zz[:, LANES:2 * LANES]).astype(BF16)

    _emit_pipelined([(i,) for i in range(seq // blk)], scores, outputs, depth=1)


def _nsa_constants(seq, blk):
    n_cmp = (seq - CMP_BLOCK) // CMP_STRIDE + 1
    n_sel = seq // SEL_BLOCK
    cs = np.arange(n_cmp) * CMP_STRIDE
    ce = cs + CMP_BLOCK - 1
    ss = np.arange(n_sel) * SEL_BLOCK
    se = ss + SEL_BLOCK - 1
    msel_t = np.zeros((n_sel, LANES), np.float32)
    msel_t[:, :n_cmp] = ((cs[:, None] <= se[None, :]) & (ce[:, None] >= ss[None, :])).T
    msel_t = np.concatenate([msel_t, msel_t], axis=1)
    key = np.arange(seq)
    kaug_w = np.zeros((seq, LANES), np.float32)
    kaug_w[:, POS_HI_LANE] = key // 16
    kaug_w[:, POS_LO_LANE] = key % 16
    kaug_s = kaug_w.copy()
    kaug_s[key, key // SEL_BLOCK] = PEN
    qslope = np.zeros((NSA_HEADS * blk, LANES), np.float32)
    for k, g in enumerate(NSA_HEAD_ORDER):
        qslope[k * blk:(k + 1) * blk, POS_HI_LANE] = 16.0 * NSA_SLOPES[g]
        qslope[k * blk:(k + 1) * blk, POS_LO_LANE] = NSA_SLOPES[g]
    return (jnp.asarray(msel_t, BF16), jnp.asarray(kaug_s, BF16), jnp.asarray(kaug_w, BF16),
            jnp.asarray(qslope, BF16))


def _nsa(pb3, pf3, pos_k, w1_k, w2_k, pos_v, w1_v, w2_v, blk=128):
    bsz, seq, _ = pb3.shape
    nchunk = seq // CMP_STRIDE
    cw = CMP_STRIDE * HEAD_DIM
    kvc = pf3[:, :, PF_KVC * LANES:(PF_KVC + 1) * LANES]
    xk = kvc[:, :, :HEAD_DIM].reshape(bsz, nchunk, cw)
    xv = kvc[:, :, HEAD_DIM:].reshape(bsz, nchunk, cw)

    def prep(pos, w1, w2):
        w1d = jnp.concatenate([w1, w1], axis=1).astype(BF16).reshape(2, cw, LANES)
        w2p = jnp.zeros((LANES, LANES), F32).at[:HEAD_DIM, :].set(
            jnp.concatenate([w2, w2], axis=1)).astype(BF16)
        return pos.reshape(2, cw), w1d, w2p

    consts = _nsa_constants(seq, blk)

    def col(base, nblk=1):
        return pl.BlockSpec((None, seq, nblk * LANES), lambda b: (b, 0, base // nblk))

    def whole(shape):
        return pl.BlockSpec(shape, lambda b: (0,) * len(shape))

    return pl.pallas_call(
        functools.partial(_nsa_kernel, seq=seq, blk=blk),
        grid=(bsz,),
        in_specs=[
            col(PB_NQ, 2), col(PB_NKS), col(PB_NVS), col(PB_NKW), col(PB_NVW),
            col(PF_NZ, 2), col(PF_MISC),
            pl.BlockSpec((None, nchunk, cw), lambda b: (b, 0, 0)),
            pl.BlockSpec((None, nchunk, cw), lambda b: (b, 0, 0)),
            whole((2, cw)), whole((2, cw, LANES)), whole((LANES, LANES)),
            whole((2, cw)), whole((2, cw, LANES)), whole((LANES, LANES)),
        ] + [whole(c.shape) for c in consts],
        out_specs=pl.BlockSpec((None, seq, NSA_W), lambda b: (b, 0, 0)),
        out_shape=jax.ShapeDtypeStruct((bsz, seq, NSA_W), BF16),
        scratch_shapes=[
            pltpu.VMEM((nchunk, LANES), BF16),
            pltpu.VMEM((nchunk, LANES), BF16),
            pltpu.VMEM((seq, 2 * LANES), BF16),
            pltpu.VMEM((seq, 2 * LANES), BF16),
            pltpu.VMEM((seq, LANES), BF16),
            pltpu.VMEM((seq, LANES), BF16),
            pltpu.VMEM((seq, LANES), BF16),
            pltpu.VMEM((seq, LANES), BF16),
        ],
        compiler_params=pltpu.CompilerParams(
            dimension_semantics=("arbitrary",), vmem_limit_bytes=VMEM_LIMIT),
        name="nsa_attn",
    )(pb3, pb3, pb3, pb3, pb3, pf3, pf3, xk, xv,
      *prep(pos_k, w1_k, w2_k), *prep(pos_v, w1_v, w2_v), *consts)


def kernel(x, norm_g, w_in, b_f, cmp_pos_k, cmp_w1_k, cmp_w2_k,
           cmp_pos_v, cmp_w1_v, cmp_w2_v, w_out, final_g):
    bsz, seq, d = x.shape
    xf = x.reshape(bsz * seq, d)
    for l in range(DEPTH):
        pb, pf = _inproj(xf, norm_g[l], _pack_w_in(w_in[l]))
        pb3 = pb.reshape(bsz, seq, NB_COLS)
        pf3 = pf.reshape(bsz, seq, NF_COLS)
        o_fox = _fox(pb3, pf3, b_f[l])
        o_sb = _sb(pb3, pf3)
        o_nsa = _nsa(pb3, pf3, cmp_pos_k[l], cmp_w1_k[l], cmp_w2_k[l],
                     cmp_pos_v[l], cmp_w1_v[l], cmp_w2_v[l])
        xf = _outproj(o_fox.reshape(bsz * seq, FOX_W), o_sb.reshape(bsz * seq, SB_W),
                      o_nsa.reshape(bsz * seq, NSA_W), xf, w_out[l].astype(BF16),
                      final_g, final=(l == DEPTH - 1))
    return xf.reshape(bsz, seq, d)
```

```python
import functools

import numpy as np
import jax
import jax.numpy as jnp
from jax import lax
from jax.experimental import pallas as pl
from jax.experimental.pallas import tpu as pltpu

F32 = jnp.float32
BF16 = jnp.bfloat16

D_MODEL = 1024
DEPTH = 2
HEAD_DIM = 64
LANES = 128
FOX_HEADS = 6
SB_HEADS = 6
NSA_HEADS = 4
FOX_W = FOX_HEADS * HEAD_DIM
SB_W = SB_HEADS * HEAD_DIM
NSA_W = NSA_HEADS * HEAD_DIM
CMP_BLOCK = 32
CMP_STRIDE = 16
SEL_BLOCK = 64
SEL_TOPK = 8
SEL_N_LOCAL = 2
WINDOW = 512
NORM_EPS = 1e-6
QK_SCALE = HEAD_DIM ** -0.5
NEG_BIG = -1e30
PEN = -(2.0 ** 100)

PB_FQ, PB_FK, PB_FV, PB_SQ, PB_SK, PB_SV, PB_NQ, PB_NKS, PB_NVS, PB_NKW, PB_NVW = (
    0, 3, 6, 9, 12, 15, 18, 20, 21, 22, 23)
PB_BLOCKS = 24
PF_FZ, PF_SZ, PF_NZ, PF_KVC, PF_MISC = 0, 3, 6, 8, 9
PF_BLOCKS = 10
NB_COLS = PB_BLOCKS * LANES
NF_COLS = PF_BLOCKS * LANES
MISC_NG = FOX_HEADS

VMEM_PHYSICAL = 64 * 1024 * 1024
VMEM_LIMIT = VMEM_PHYSICAL - 4 * 1024 * 1024

_NT = (((1,), (1,)), ((), ()))


def _dot(a, b):
    return jnp.dot(a, b, preferred_element_type=F32)


def _dot_nt(a, b):
    return lax.dot_general(a, b, _NT, preferred_element_type=F32)


def _sigmoid(x):
    return 1.0 / (1.0 + jnp.exp(-x))


def _log_sigmoid(x):
    return -(jnp.maximum(-x, 0.0) + jnp.log1p(jnp.exp(-jnp.abs(x))))


def _split3(x):
    hi = x.astype(BF16)
    r = x - hi.astype(F32)
    mid = r.astype(BF16)
    lo = (r - mid.astype(F32)).astype(BF16)
    return hi, mid, lo


def _split2(x):
    hi = x.astype(BF16)
    lo = (x - hi.astype(F32)).astype(BF16)
    return hi, lo


def _rep(x, n):
    return x if n == 1 else jnp.concatenate([x] * n, axis=1)


PIPELINE_DEPTH = 4


def _emit_pipelined(items, first, second, depth=PIPELINE_DEPTH):
    pending = {}
    for n, item in enumerate(items):
        pending[item] = first(*item)
        if n >= depth:
            prev = items[n - depth]
            second(*prev, pending.pop(prev))
    for prev in items[max(len(items) - depth, 0):]:
        second(*prev, pending.pop(prev))


def _inproj_kernel(x_ref, g_ref, w_ref, pb_ref, pf_ref):
    x = x_ref[...]
    ms = jnp.mean(x * x, axis=-1, keepdims=True)
    h = (x * lax.rsqrt(ms + NORM_EPS) * g_ref[...]).astype(BF16)
    cb = 512
    for c in range(0, NB_COLS, cb):
        pb_ref[:, c:c + cb] = _dot(h, w_ref[:, c:c + cb]).astype(BF16)
    cf = 640
    for c in range(0, NF_COLS, cf):
        pf_ref[:, c:c + cf] = _dot(h, w_ref[:, NB_COLS + c:NB_COLS + c + cf])


def _inproj(xf, g, w_all, tm=512):
    m, d = xf.shape
    return pl.pallas_call(
        _inproj_kernel,
        grid=(m // tm,),
        in_specs=[
            pl.BlockSpec((tm, d), lambda i: (i, 0)),
            pl.BlockSpec((1, d), lambda i: (0, 0)),
            pl.BlockSpec((d, NB_COLS + NF_COLS), lambda i: (0, 0)),
        ],
        out_specs=[
            pl.BlockSpec((tm, NB_COLS), lambda i: (i, 0)),
            pl.BlockSpec((tm, NF_COLS), lambda i: (i, 0)),
        ],
        out_shape=[
            jax.ShapeDtypeStruct((m, NB_COLS), BF16),
            jax.ShapeDtypeStruct((m, NF_COLS), F32),
        ],
        compiler_params=pltpu.CompilerParams(
            dimension_semantics=("arbitrary",), vmem_limit_bytes=VMEM_LIMIT),
        name="inproj",
    )(xf, g.reshape(1, d), w_all)


def _pack_w_in(w):
    sizes = (FOX_W, FOX_W, FOX_W, FOX_HEADS, FOX_W, SB_W, SB_W, SB_W, SB_W,
             NSA_W, HEAD_DIM, HEAD_DIM, HEAD_DIM, HEAD_DIM, HEAD_DIM, HEAD_DIM,
             3 * NSA_HEADS, NSA_W)
    offs = np.concatenate([[0], np.cumsum(sizes)])
    (fq, fk, fv, ff, fz, sq, sk, sv, sz, nq, nkc, nvc, nks, nvs, nkw, nvw, ng, nz) = [
        w[:, offs[i]:offs[i + 1]] for i in range(len(sizes))]
    pad = jnp.zeros((w.shape[0], LANES - FOX_HEADS - 3 * NSA_HEADS), w.dtype)
    cols = [fq * QK_SCALE, fk, fv, sq * QK_SCALE, sk, sv, nq * QK_SCALE,
            nks, nks, nvs, nvs, nkw, nkw, nvw, nvw,
            fz, sz, nz, nkc, nvc, ff, ng, pad]
    return jnp.concatenate(cols, axis=1).astype(BF16)


def _outproj_kernel(of_ref, os_ref, on_ref, x_ref, w_ref, g_ref, o_ref, *, final):
    y = (x_ref[...]
         + _dot(of_ref[...], w_ref[0:FOX_W, :])
         + _dot(os_ref[...], w_ref[FOX_W:FOX_W + SB_W, :])
         + _dot(on_ref[...], w_ref[FOX_W + SB_W:, :]))
    if final:
        ms = jnp.mean(y * y, axis=-1, keepdims=True)
        y = y * lax.rsqrt(ms + NORM_EPS) * g_ref[...]
    o_ref[...] = y


def _outproj(o_fox, o_sb, o_nsa, xf, w, g, final, tm=512):
    m, d = xf.shape
    return pl.pallas_call(
        functools.partial(_outproj_kernel, final=final),
        grid=(m // tm,),
        in_specs=[
            pl.BlockSpec((tm, FOX_W), lambda i: (i, 0)),
            pl.BlockSpec((tm, SB_W), lambda i: (i, 0)),
            pl.BlockSpec((tm, NSA_W), lambda i: (i, 0)),
            pl.BlockSpec((tm, d), lambda i: (i, 0)),
            pl.BlockSpec((d, d), lambda i: (0, 0)),
            pl.BlockSpec((1, d), lambda i: (0, 0)),
        ],
        out_specs=pl.BlockSpec((tm, d), lambda i: (i, 0)),
        out_shape=jax.ShapeDtypeStruct((m, d), F32),
        compiler_params=pltpu.CompilerParams(
            dimension_semantics=("arbitrary",), vmem_limit_bytes=VMEM_LIMIT),
        name="outproj_final" if final else "outproj",
    )(o_fox, o_sb, o_nsa, xf, w, g.reshape(1, d))


def _fox_place():
    place = np.zeros((FOX_HEADS // 2, 3 * LANES, 3 * LANES), np.float32)
    for p in range(FOX_HEADS // 2):
        for t in range(3):
            place[p, t * LANES + 2 * p, 3 + t] = -1.0
            place[p, t * LANES + 2 * p + 1, 9 + t] = -1.0
            place[p, t * LANES + 2 * p, LANES + t] = 1.0
            place[p, t * LANES + 2 * p + 1, 2 * LANES + 6 + t] = 1.0
    ones = np.zeros((1, 3 * LANES), np.float32)
    ones[0, [0, 1, 2, 6, 7, 8]] = 1.0
    ones[0, [LANES + 3, LANES + 4, LANES + 5]] = 1.0
    ones[0, [2 * LANES + 9, 2 * LANES + 10, 2 * LANES + 11]] = 1.0
    return jnp.asarray(place, BF16), jnp.asarray(ones, F32)


def _fox_kernel(q_ref, k_ref, v_ref, z_ref, misc_ref, bf_ref, tri_ref, place_ref, ones_ref,
                o_ref, qq_ref, kk_ref, vp_ref, cparts_ref, *, seq, blk):
    pair = pl.program_id(1)
    low = lax.broadcasted_iota(jnp.int32, (blk, LANES), 1) < HEAD_DIM
    nblk = seq // blk

    @pl.when(pair == 0)
    def _():
        tri = tri_ref[...]
        carry = jnp.zeros((1, LANES), F32)
        for b in range(nblk):
            rows = slice(b * blk, (b + 1) * blk)
            ls = _log_sigmoid(misc_ref[rows, :] + bf_ref[...])
            c3 = _dot(tri, jnp.concatenate(_split3(ls), axis=1))
            cb = c3[:, 0:LANES] + c3[:, LANES:2 * LANES] + c3[:, 2 * LANES:3 * LANES] + carry
            carry = cb[blk - 1:blk, :]
            cparts_ref[rows, :] = jnp.concatenate(_split3(cb), axis=1)

    place = place_ref[pair]
    for b in range(nblk):
        rows = slice(b * blk, (b + 1) * blk)
        aug = (_dot(cparts_ref[rows, :], place) + ones_ref[...]).astype(BF16)

        q2 = q_ref[rows, :]
        v2 = v_ref[rows, :]
        zero = jnp.zeros_like(q2)
        one = jnp.ones_like(v2)
        kk_ref[rows, 0:LANES] = k_ref[rows, :]
        kk_ref[rows, LANES:2 * LANES] = aug[:, 0:LANES]
        qq_ref[0, rows, 0:LANES] = jnp.where(low, q2, zero)
        qq_ref[0, rows, LANES:2 * LANES] = aug[:, LANES:2 * LANES]
        qq_ref[1, rows, 0:LANES] = jnp.where(low, zero, q2)
        qq_ref[1, rows, LANES:2 * LANES] = aug[:, 2 * LANES:3 * LANES]
        vp_ref[0, rows, :] = jnp.where(low, v2, one)
        vp_ref[1, rows, :] = jnp.where(low, one, v2)

    causal = (lax.broadcasted_iota(jnp.int32, (blk, blk), 1)
              <= lax.broadcasted_iota(jnp.int32, (blk, blk), 0))

    def scores(i, h):
        r0 = i * blk
        rows = slice(r0, r0 + blk)
        qa = qq_ref[h, rows, :]
        sd = jnp.where(causal, _dot_nt(qa, kk_ref[rows, :]), -jnp.inf)
        m = jnp.max(sd, axis=1, keepdims=True)
        sm = None
        if i > 0:
            sm = _dot_nt(qa, kk_ref[0:r0, :])
            m = jnp.maximum(m, jnp.max(sm, axis=1, keepdims=True))
        return sd, sm, m

    outs = {}

    def weighted_sum(i, h, st):
        sd, sm, m = st
        rows = slice(i * blk, (i + 1) * blk)
        a = _dot(jnp.exp(sd - m).astype(BF16), vp_ref[h, rows, :])
        if i > 0:
            a = a + _dot(jnp.exp(sm - m).astype(BF16), vp_ref[h, 0:i * blk, :])
        outs[i, h] = a / pltpu.roll(a, HEAD_DIM, axis=1)
        if h == 1:
            o = jnp.where(low, outs.pop((i, 0)), outs.pop((i, 1)))
            z = z_ref[rows, :]
            o_ref[rows, :] = (o * (z * _sigmoid(z))).astype(BF16)

    _emit_pipelined([(i, h) for i in range(nblk) for h in range(2)], scores, weighted_sum)


def _fox(pb3, pf3, b_f, blk=256):
    bsz, seq, _ = pb3.shape
    npair = FOX_HEADS // 2
    bias = jnp.zeros((1, LANES), F32).at[0, :FOX_HEADS].set(b_f)
    tri = jnp.asarray(np.tril(np.ones((blk, blk), np.float32)), BF16)
    place, ones = _fox_place()

    def col(base):
        return pl.BlockSpec((None, seq, LANES), lambda b, p: (b, 0, base + p))

    def whole(shape):
        return pl.BlockSpec(shape, lambda b, p: (0,) * len(shape))

    return pl.pallas_call(
        functools.partial(_fox_kernel, seq=seq, blk=blk),
        grid=(bsz, npair),
        in_specs=[
            col(PB_FQ), col(PB_FK), col(PB_FV), col(PF_FZ),
            pl.BlockSpec((None, seq, LANES), lambda b, p: (b, 0, PF_MISC)),
            whole((1, LANES)), whole((blk, blk)), whole(place.shape), whole(ones.shape),
        ],
        out_specs=pl.BlockSpec((None, seq, LANES), lambda b, p: (b, 0, p)),
        out_shape=jax.ShapeDtypeStruct((bsz, seq, FOX_W), BF16),
        scratch_shapes=[
            pltpu.VMEM((2, seq, 2 * LANES), BF16),
            pltpu.VMEM((seq, 2 * LANES), BF16),
            pltpu.VMEM((2, seq, LANES), BF16),
            pltpu.VMEM((seq, 3 * LANES), BF16),
        ],
        compiler_params=pltpu.CompilerParams(
            dimension_semantics=("arbitrary", "arbitrary"), vmem_limit_bytes=VMEM_LIMIT),
        name="fox_attn",
    )(pb3, pb3, pb3, pf3, pf3, bias, tri, place, ones)


SB_NEAR_TILES = 2
SB_DEAD = -104.0


def _sb_kernel(q_ref, k_ref, v_ref, z_ref, suf_ref, o_ref, acc_ref, carry_ref, *, seq, blk):
    low = lax.broadcasted_iota(jnp.int32, (blk, LANES), 1) < HEAD_DIM
    strict = (lax.broadcasted_iota(jnp.int32, (blk, blk), 1)
              < lax.broadcasted_iota(jnp.int32, (blk, blk), 0))
    suf = suf_ref[...]

    def log_terms(qh, j_hi, j_lo):
        z = _dot_nt(qh, k_ref[j_lo * blk:(j_hi + 1) * blk, :])
        lsz = jnp.minimum(z, 0.0) - jnp.log(1.0 + jnp.exp(-jnp.abs(z)))
        return lsz, lsz - z

    def tiles(i, j_hi, j_lo, carry, terms):
        lsz, l1m = terms
        k0 = j_lo * blk
        out = None
        for j in range(j_hi, j_lo - 1, -1):
            cs = slice(j * blk - k0, (j + 1) * blk - k0)
            l1 = l1m[:, cs]
            if j == i:
                l1 = jnp.where(strict, l1, 0.0)
            r = _dot(l1.astype(BF16), suf)
            a = jnp.exp(lsz[:, cs] + (r + carry))
            if j == i:
                a = jnp.where(strict, a, 0.0)
            pv = _dot(a.astype(BF16), v_ref[j * blk:(j + 1) * blk, :])
            out = pv if out is None else out + pv
            carry = carry + (r[:, 0:1] + l1[:, 0:1])
        return out, carry

    nblk = seq // blk

    def q_head(i, h):
        q2 = q_ref[i * blk:(i + 1) * blk, :]
        zero = jnp.zeros_like(q2)
        return jnp.where(low, q2, zero) if h == 0 else jnp.where(low, zero, q2)

    def j_near(i):
        return max(i - SB_NEAR_TILES + 1, 0)

    alive = {}

    def near_terms(i, h):
        return log_terms(q_head(i, h), i, j_near(i))

    def near_tiles(i, h, terms):
        out, carry = tiles(i, i, j_near(i), jnp.zeros((blk, 1), F32), terms)
        acc_ref[i, h] = out
        if j_near(i) > 0:
            carry_ref[i, h] = jnp.broadcast_to(carry, (blk, LANES))
            alive[i, h] = jnp.max(carry) >= SB_DEAD

    _emit_pipelined([(i, h) for i in range(nblk) for h in range(2)], near_terms, near_tiles)

    for (i, h), flag in alive.items():
        @pl.when(flag)
        def _(i=i, h=h):
            far, _ = tiles(i, j_near(i) - 1, 0, carry_ref[i, h][:, 0:1],
                           log_terms(q_head(i, h), j_near(i) - 1, 0))
            acc_ref[i, h] = acc_ref[i, h] + far

    for i in range(nblk):
        rows = slice(i * blk, (i + 1) * blk)
        o = jnp.where(low, acc_ref[i, 0], acc_ref[i, 1])
        zg = z_ref[rows, :]
        o_ref[rows, :] = (o * (zg * _sigmoid(zg))).astype(BF16)


def _sb(pb3, pf3, blk=256):
    bsz, seq, _ = pb3.shape
    npair = SB_HEADS // 2
    suf = jnp.asarray(np.tril(np.ones((blk, blk), np.float32), -1), BF16)

    def col(base):
        return pl.BlockSpec((None, seq, LANES), lambda b, p: (b, 0, base + p))

    return pl.pallas_call(
        functools.partial(_sb_kernel, seq=seq, blk=blk),
        grid=(bsz, npair),
        in_specs=[
            col(PB_SQ), col(PB_SK), col(PB_SV), col(PF_SZ),
            pl.BlockSpec(suf.shape, lambda b, p: (0, 0)),
        ],
        out_specs=pl.BlockSpec((None, seq, LANES), lambda b, p: (b, 0, p)),
        out_shape=jax.ShapeDtypeStruct((bsz, seq, SB_W), BF16),
        scratch_shapes=[pltpu.VMEM((seq // blk, 2, blk, LANES), F32),
                        pltpu.VMEM((seq // blk, 2, blk, LANES), F32)],
        compiler_params=pltpu.CompilerParams(
            dimension_semantics=("arbitrary", "arbitrary"), vmem_limit_bytes=VMEM_LIMIT),
        name="sb_attn",
    )(pb3, pb3, pb3, pf3, suf)


NSA_HEAD_ORDER = (0, 1, 2, 3)
SEL_LANES = 32
POS_HI_LANE = 32
POS_LO_LANE = 33
NSA_SLOPES = tuple(2.0 ** (-8.0 * (g + 1) / NSA_HEADS) for g in range(NSA_HEADS))


def _nsa_kernel(q_ref, ks_ref, vs_ref, kw_ref, vw_ref, z_ref, misc_ref, xk_ref, xv_ref,
                posk_ref, w1k_ref, w2k_ref, posv_ref, w1v_ref, w2v_ref,
                mselt_ref, kaugs_ref, kaugw_ref, qslope_ref,
                o_ref, kc_ref, vct_ref, kks_ref, kkw_ref, vst_ref, vwt_ref, *, seq, blk):
    cols4 = NSA_HEADS * blk
    n_cmp = (seq - CMP_BLOCK) // CMP_STRIDE + 1
    n_sel = seq // SEL_BLOCK
    nwin = WINDOW // blk
    eye = jnp.where(lax.broadcasted_iota(jnp.int32, (LANES, LANES), 0)
                    == lax.broadcasted_iota(jnp.int32, (LANES, LANES), 1), 1.0, 0.0).astype(BF16)

    def compress(x_ref, pos_ref, w1_ref, w2_ref):
        x = x_ref[...]
        a = _dot((x + pos_ref[0:1, :]).astype(BF16), w1_ref[0])
        b = _dot((x + pos_ref[1:2, :]).astype(BF16), w1_ref[1])
        hid = a + pltpu.roll(b, b.shape[0] - 1, axis=0)
        hid = hid * _sigmoid(hid)
        return _dot(hid.astype(BF16), w2_ref[...]).astype(BF16)

    kc_ref[...] = compress(xk_ref, posk_ref, w1k_ref, w2k_ref)
    vct_ref[...] = _dot_nt(eye, compress(xv_ref, posv_ref, w1v_ref, w2v_ref)).astype(BF16)

    kks_ref[:, 0:LANES] = ks_ref[...]
    kks_ref[:, LANES:2 * LANES] = kaugs_ref[...]
    kkw_ref[:, 0:LANES] = kw_ref[...]
    kkw_ref[:, LANES:2 * LANES] = kaugw_ref[...]
    tb = 2 * LANES
    low_t = lax.broadcasted_iota(jnp.int32, (tb, LANES), 1) < HEAD_DIM
    for b in range(seq // tb):
        rows = slice(b * tb, (b + 1) * tb)
        vs = vs_ref[rows, :]
        vw = vw_ref[rows, :]
        one = jnp.ones_like(vs)
        vst_ref[:, rows] = _dot_nt(eye, jnp.where(low_t, vs, one)).astype(BF16)
        vwt_ref[:, rows] = _dot_nt(eye, jnp.where(low_t, vw, one)).astype(BF16)

    low = lax.broadcasted_iota(jnp.int32, (blk, LANES), 1) < HEAD_DIM
    colt = lax.broadcasted_iota(jnp.int32, (1, cols4), 1)
    grp = colt // blk
    tok = colt - grp * blk
    gslope = [NSA_SLOPES[g] for g in NSA_HEAD_ORDER]
    slope = jnp.where(grp == 0, gslope[0], jnp.where(grp == 1, gslope[1],
                      jnp.where(grp == 2, gslope[2], gslope[3]))).astype(F32)
    keyr = lax.broadcasted_iota(jnp.int32, (blk, 1), 0)
    causal_t = keyr <= tok
    after_t = keyr > tok
    cmpr = lax.broadcasted_iota(jnp.int32, (LANES, 1), 0)
    cmp_end = (cmpr * CMP_STRIDE + (CMP_BLOCK - 1)).astype(F32)
    jrow = lax.broadcasted_iota(jnp.int32, (n_sel, blk), 0)
    tcol = lax.broadcasted_iota(jnp.int32, (n_sel, blk), 1)
    sel_lane = lax.broadcasted_iota(jnp.int32, (cols4, LANES), 1) < SEL_LANES
    qslope = qslope_ref[...]

    def col_max(parts):
        m = None
        for s in parts:
            mp = jnp.max(s, axis=0, keepdims=True)
            m = mp if m is None else jnp.maximum(m, mp)
        return m

    def softmax_pv(parts, m, keys, vt_ref):
        p = [jnp.exp(s - m).astype(BF16) for s in parts]
        p = p[0] if len(p) == 1 else jnp.concatenate(p, axis=0)
        acc = _dot(vt_ref[:, keys], p)
        return acc[0:HEAD_DIM] / acc[HEAD_DIM:LANES]

    def scores(i):
        qs = i * blk
        rows = slice(qs, qs + blk)
        q01 = q_ref[rows, 0:LANES]
        q23 = q_ref[rows, LANES:2 * LANES]
        zero = jnp.zeros_like(q01)
        qst = jnp.concatenate([jnp.where(low, q01, zero), jnp.where(low, zero, q01),
                               jnp.where(low, q23, zero), jnp.where(low, zero, q23)], axis=0)

        dist_c = (qs + tok).astype(F32) - cmp_end
        valid_c = (dist_c >= 0.0) & (cmpr < n_cmp)
        sc = _dot_nt(kc_ref[...], qst) - slope * dist_c
        sc = jnp.where(valid_c, sc, -jnp.inf)
        mc = jnp.max(sc, axis=0, keepdims=True)
        mc = jnp.where(mc == -jnp.inf, 0.0, mc)
        pc = jnp.exp(sc - mc)
        ssum = jnp.sum(pc, axis=0, keepdims=True)
        pc = pc / jnp.where(ssum > 0.0, ssum, 1.0)
        oc = _dot(vct_ref[...], pc.astype(BF16))[0:HEAD_DIM]

        pcs = (pc[:, 0:blk] + pc[:, blk:2 * blk] + pc[:, 2 * blk:3 * blk]
               + pc[:, 3 * blk:4 * blk])
        hi, lo = _split2(pcs)
        imp = _dot(mselt_ref[...], jnp.concatenate([hi, lo], axis=0))
        back = (qs + tcol) // SEL_BLOCK - jrow
        imp = jnp.where(back < 0, -jnp.inf,
                        jnp.where(back < SEL_N_LOCAL, jnp.inf,
                                  jnp.where(jrow == 0, jnp.inf, imp)))
        rank = jnp.zeros((n_sel, blk), F32)
        for c in range(n_sel):
            rowc = imp[c:c + 1, :]
            tie = jnp.where(jrow > c, 1.0, 0.0)
            rank = rank + jnp.where(rowc > imp, 1.0, jnp.where(rowc == imp, tie, 0.0))
        unsel_t = jnp.where(rank >= float(SEL_TOPK), 1.0, 0.0).astype(BF16)
        unsel_t = jnp.concatenate([unsel_t, jnp.zeros((LANES - n_sel, blk), BF16)], axis=0)
        unsel = _dot_nt(eye, unsel_t).astype(BF16)

        qop_w = jnp.concatenate([qst, qslope], axis=1)
        qaug = jnp.where(sel_lane, jnp.concatenate([unsel] * NSA_HEADS, axis=0), qslope)
        qop_s = jnp.concatenate([qst, qaug], axis=1)

        win = []
        w0 = max(qs - WINDOW, 0)
        if i >= nwin:
            win.append(jnp.where(after_t, _dot_nt(kkw_ref[w0:w0 + blk, :], qop_w), -jnp.inf))
        wm = max(qs - WINDOW + blk, 0)
        if qs > wm:
            win.append(_dot_nt(kkw_ref[wm:qs, :], qop_w))
        win.append(jnp.where(causal_t, _dot_nt(kkw_ref[rows, :], qop_w), -jnp.inf))

        sel = []
        if i > 0:
            sel.append(_dot_nt(kks_ref[0:qs, :], qop_s))
        sel.append(jnp.where(causal_t, _dot_nt(kks_ref[rows, :], qop_s), -jnp.inf))
        return oc, win, col_max(win), sel, col_max(sel)

    def outputs(i, st):
        oc, win, m_win, sel, m_sel = st
        qs = i * blk
        rows = slice(qs, qs + blk)
        o_win = softmax_pv(win, m_win, slice(max(qs - WINDOW, 0), qs + blk), vwt_ref)
        o_sel = softmax_pv(sel, m_sel, slice(0, qs + blk), vst_ref)

        sg_t = _sigmoid(misc_ref[rows, :]).T

        def gate(branch):
            return jnp.concatenate(
                [sg_t[MISC_NG + 3 * g + branch:MISC_NG + 3 * g + branch + 1, :]
                 for g in NSA_HEAD_ORDER], axis=1)

        tot = oc * gate(0) + o_sel * gate(1) + o_win * gate(2)
        by_head = {g: tot[:, k * blk:(k + 1) * blk] for k, g in enumerate(NSA_HEAD_ORDER)}
        out = jnp.concatenate([by_head[g] for g in range(NSA_HEADS)], axis=0).T
        z = z_ref[rows, :]
        o_ref[rows, :] = (out * (z * _sigmoid(z))).astype(BF16)

    _emit_pipelined([(i,) for i in range(seq // blk)], scores, outputs, depth=1)


def _nsa_constants(seq, blk):
    n_cmp = (seq - CMP_BLOCK) // CMP_STRIDE + 1
    n_sel = seq // SEL_BLOCK
    cs = np.arange(n_cmp) * CMP_STRIDE
    ce = cs + CMP_BLOCK - 1
    ss = np.arange(n_sel) * SEL_BLOCK
    se = ss + SEL_BLOCK - 1
    msel_t = np.zeros((n_sel, LANES), np.float32)
    msel_t[:, :n_cmp] = ((cs[:, None] <= se[None, :]) & (ce[:, None] >= ss[None, :])).T
    msel_t = np.concatenate([msel_t, msel_t], axis=1)
    key = np.arange(seq)
    kaug_w = np.zeros((seq, LANES), np.float32)
    kaug_w[:, POS_HI_LANE] = key // 16
    kaug_w[:, POS_LO_LANE] = key % 16
    kaug_s = kaug_w.copy()
    kaug_s[key, key // SEL_BLOCK] = PEN
    qslope = np.zeros((NSA_HEADS * blk, LANES), np.float32)
    for k, g in enumerate(NSA_HEAD_ORDER):
        qslope[k * blk:(k + 1) * blk, POS_HI_LANE] = 16.0 * NSA_SLOPES[g]
        qslope[k * blk:(k + 1) * blk, POS_LO_LANE] = NSA_SLOPES[g]
    return (jnp.asarray(msel_t, BF16), jnp.asarray(kaug_s, BF16), jnp.asarray(kaug_w, BF16),
            jnp.asarray(qslope, BF16))


def _nsa(pb3, pf3, pos_k, w1_k, w2_k, pos_v, w1_v, w2_v, blk=128):
    bsz, seq, _ = pb3.shape
    nchunk = seq // CMP_STRIDE
    cw = CMP_STRIDE * HEAD_DIM
    kvc = pf3[:, :, PF_KVC * LANES:(PF_KVC + 1) * LANES]
    xk = kvc[:, :, :HEAD_DIM].reshape(bsz, nchunk, cw)
    xv = kvc[:, :, HEAD_DIM:].reshape(bsz, nchunk, cw)

    def prep(pos, w1, w2):
        w1d = jnp.concatenate([w1, w1], axis=1).astype(BF16).reshape(2, cw, LANES)
        w2p = jnp.zeros((LANES, LANES), F32).at[:HEAD_DIM, :].set(
            jnp.concatenate([w2, w2], axis=1)).astype(BF16)
        return pos.reshape(2, cw), w1d, w2p

    consts = _nsa_constants(seq, blk)

    def col(base, nblk=1):
        return pl.BlockSpec((None, seq, nblk * LANES), lambda b: (b, 0, base // nblk))

    def whole(shape):
        return pl.BlockSpec(shape, lambda b: (0,) * len(shape))

    return pl.pallas_call(
        functools.partial(_nsa_kernel, seq=seq, blk=blk),
        grid=(bsz,),
        in_specs=[
            col(PB_NQ, 2), col(PB_NKS), col(PB_NVS), col(PB_NKW), col(PB_NVW),
            col(PF_NZ, 2), col(PF_MISC),
            pl.BlockSpec((None, nchunk, cw), lambda b: (b, 0, 0)),
            pl.BlockSpec((None, nchunk, cw), lambda b: (b, 0, 0)),
            whole((2, cw)), whole((2, cw, LANES)), whole((LANES, LANES)),
            whole((2, cw)), whole((2, cw, LANES)), whole((LANES, LANES)),
        ] + [whole(c.shape) for c in consts],
        out_specs=pl.BlockSpec((None, seq, NSA_W), lambda b: (b, 0, 0)),
        out_shape=jax.ShapeDtypeStruct((bsz, seq, NSA_W), BF16),
        scratch_shapes=[
            pltpu.VMEM((nchunk, LANES), BF16),
            pltpu.VMEM((LANES, nchunk), BF16),
            pltpu.VMEM((seq, 2 * LANES), BF16),
            pltpu.VMEM((seq, 2 * LANES), BF16),
            pltpu.VMEM((LANES, seq), BF16),
            pltpu.VMEM((LANES, seq), BF16),
        ],
        compiler_params=pltpu.CompilerParams(
            dimension_semantics=("arbitrary",), vmem_limit_bytes=VMEM_LIMIT),
        name="nsa_attn",
    )(pb3, pb3, pb3, pb3, pb3, pf3, pf3, xk, xv,
      *prep(pos_k, w1_k, w2_k), *prep(pos_v, w1_v, w2_v), *consts)


def kernel(x, norm_g, w_in, b_f, cmp_pos_k, cmp_w1_k, cmp_w2_k,
           cmp_pos_v, cmp_w1_v, cmp_w2_v, w_out, final_g):
    bsz, seq, d = x.shape
    xf = x.reshape(bsz * seq, d)
    for l in range(DEPTH):
        pb, pf = _inproj(xf, norm_g[l], _pack_w_in(w_in[l]))
        pb3 = pb.reshape(bsz, seq, NB_COLS)
        pf3 = pf.reshape(bsz, seq, NF_COLS)
        o_fox = _fox(pb3, pf3, b_f[l])
        o_sb = _sb(pb3, pf3)
        o_nsa = _nsa(pb3, pf3, cmp_pos_k[l], cmp_w1_k[l], cmp_w2_k[l],
                     cmp_pos_v[l], cmp_w1_v[l], cmp_w2_v[l])
        xf = _outproj(o_fox.reshape(bsz * seq, FOX_W), o_sb.reshape(bsz * seq, SB_W),
                      o_nsa.reshape(bsz * seq, NSA_W), xf, w_out[l].astype(BF16),
                      final_g, final=(l == DEPTH - 1))
    return xf.reshape(bsz, seq, d)
```

```python
import functools

import numpy as np
import jax
import jax.numpy as jnp
from jax import lax
from jax.experimental import pallas as pl
from jax.experimental.pallas import tpu as pltpu

F32 = jnp.float32
BF16 = jnp.bfloat16

D_MODEL = 1024
DEPTH = 2
HEAD_DIM = 64
LANES = 128
FOX_HEADS = 6
SB_HEADS = 6
NSA_HEADS = 4
FOX_W = FOX_HEADS * HEAD_DIM
SB_W = SB_HEADS * HEAD_DIM
NSA_W = NSA_HEADS * HEAD_DIM
CMP_BLOCK = 32
CMP_STRIDE = 16
SEL_BLOCK = 64
SEL_TOPK = 8
SEL_N_LOCAL = 2
WINDOW = 512
NORM_EPS = 1e-6
QK_SCALE = HEAD_DIM ** -0.5
NEG_BIG = -1e30
PEN = -(2.0 ** 100)

PB_FQ, PB_FK, PB_FV, PB_SQ, PB_SK, PB_SV, PB_NQ, PB_NKS, PB_NVS, PB_NKW, PB_NVW = (
    0, 3, 6, 9, 12, 15, 18, 20, 21, 22, 23)
PB_BLOCKS = 24
PF_FZ, PF_SZ, PF_NZ, PF_KVC, PF_MISC = 0, 3, 6, 8, 9
PF_BLOCKS = 10
NB_COLS = PB_BLOCKS * LANES
NF_COLS = PF_BLOCKS * LANES
MISC_PAIR_LANES = 16
MISC_FF_COPIES = 6
MISC_NG = MISC_PAIR_LANES * (FOX_HEADS // 2)

VMEM_PHYSICAL = 64 * 1024 * 1024
VMEM_LIMIT = VMEM_PHYSICAL - 4 * 1024 * 1024

_NT = (((1,), (1,)), ((), ()))


def _dot(a, b):
    return jnp.dot(a, b, preferred_element_type=F32)


def _dot_nt(a, b):
    return lax.dot_general(a, b, _NT, preferred_element_type=F32)


def _sigmoid(x):
    return 1.0 / (1.0 + jnp.exp(-x))


def _log_sigmoid(x):
    return -(jnp.maximum(-x, 0.0) + jnp.log1p(jnp.exp(-jnp.abs(x))))


def _split3(x):
    hi = x.astype(BF16)
    r = x - hi.astype(F32)
    mid = r.astype(BF16)
    lo = (r - mid.astype(F32)).astype(BF16)
    return hi, mid, lo


def _split2(x):
    hi = x.astype(BF16)
    lo = (x - hi.astype(F32)).astype(BF16)
    return hi, lo


def _rep(x, n):
    return x if n == 1 else jnp.concatenate([x] * n, axis=1)


PIPELINE_DEPTH = 4


def _emit_pipelined(items, first, second, depth=PIPELINE_DEPTH):
    pending = {}
    for n, item in enumerate(items):
        pending[item] = first(*item)
        if n >= depth:
            prev = items[n - depth]
            second(*prev, pending.pop(prev))
    for prev in items[max(len(items) - depth, 0):]:
        second(*prev, pending.pop(prev))


def _inproj_kernel(x_ref, g_ref, w_ref, pb_ref, pf_ref):
    x = x_ref[...]
    ms = jnp.mean(x * x, axis=-1, keepdims=True)
    h = (x * lax.rsqrt(ms + NORM_EPS) * g_ref[...]).astype(BF16)
    cb = 512
    for c in range(0, NB_COLS, cb):
        pb_ref[:, c:c + cb] = _dot(h, w_ref[:, c:c + cb]).astype(BF16)
    cf = 640
    for c in range(0, NF_COLS, cf):
        pf_ref[:, c:c + cf] = _dot(h, w_ref[:, NB_COLS + c:NB_COLS + c + cf])


def _inproj(xf, g, w_all, tm=512):
    m, d = xf.shape
    return pl.pallas_call(
        _inproj_kernel,
        grid=(m // tm,),
        in_specs=[
            pl.BlockSpec((tm, d), lambda i: (i, 0)),
            pl.BlockSpec((1, d), lambda i: (0, 0)),
            pl.BlockSpec((d, NB_COLS + NF_COLS), lambda i: (0, 0)),
        ],
        out_specs=[
            pl.BlockSpec((tm, NB_COLS), lambda i: (i, 0)),
            pl.BlockSpec((tm, NF_COLS), lambda i: (i, 0)),
        ],
        out_shape=[
            jax.ShapeDtypeStruct((m, NB_COLS), BF16),
            jax.ShapeDtypeStruct((m, NF_COLS), F32),
        ],
        compiler_params=pltpu.CompilerParams(
            dimension_semantics=("arbitrary",), vmem_limit_bytes=VMEM_LIMIT),
        name="inproj",
    )(xf, g.reshape(1, d), w_all)


def _pack_w_in(w):
    sizes = (FOX_W, FOX_W, FOX_W, FOX_HEADS, FOX_W, SB_W, SB_W, SB_W, SB_W,
             NSA_W, HEAD_DIM, HEAD_DIM, HEAD_DIM, HEAD_DIM, HEAD_DIM, HEAD_DIM,
             3 * NSA_HEADS, NSA_W)
    offs = np.concatenate([[0], np.cumsum(sizes)])
    (fq, fk, fv, ff, fz, sq, sk, sv, sz, nq, nkc, nvc, nks, nvs, nkw, nvw, ng, nz) = [
        w[:, offs[i]:offs[i + 1]] for i in range(len(sizes))]
    zeros = jnp.zeros((w.shape[0], LANES), w.dtype)
    cols = [fq * QK_SCALE, fk, fv, sq * QK_SCALE, sk, sv, nq * QK_SCALE,
            nks, nks, nvs, nvs, nkw, nkw, nvw, nvw,
            fz, sz, nz, nkc, nvc, _misc_ff_layout(ff, zeros), ng,
            zeros[:, :LANES - MISC_NG - 3 * NSA_HEADS]]
    return jnp.concatenate(cols, axis=1).astype(BF16)


def _misc_ff_layout(ff, zeros):
    cols = []
    for p in range(FOX_HEADS // 2):
        cols += [ff[:, 2 * p:2 * p + 1]] * MISC_FF_COPIES + [ff[:, 2 * p + 1:2 * p + 2]] * MISC_FF_COPIES
        cols.append(zeros[:, :MISC_PAIR_LANES - 2 * MISC_FF_COPIES])
    return jnp.concatenate(cols, axis=1)


def _outproj_kernel(of_ref, os_ref, on_ref, x_ref, w_ref, g_ref, o_ref, *, final):
    y = (x_ref[...]
         + _dot(of_ref[...], w_ref[0:FOX_W, :])
         + _dot(os_ref[...], w_ref[FOX_W:FOX_W + SB_W, :])
         + _dot(on_ref[...], w_ref[FOX_W + SB_W:, :]))
    if final:
        ms = jnp.mean(y * y, axis=-1, keepdims=True)
        y = y * lax.rsqrt(ms + NORM_EPS) * g_ref[...]
    o_ref[...] = y


def _outproj(o_fox, o_sb, o_nsa, xf, w, g, final, tm=512):
    m, d = xf.shape
    return pl.pallas_call(
        functools.partial(_outproj_kernel, final=final),
        grid=(m // tm,),
        in_specs=[
            pl.BlockSpec((tm, FOX_W), lambda i: (i, 0)),
            pl.BlockSpec((tm, SB_W), lambda i: (i, 0)),
            pl.BlockSpec((tm, NSA_W), lambda i: (i, 0)),
            pl.BlockSpec((tm, d), lambda i: (i, 0)),
            pl.BlockSpec((d, d), lambda i: (0, 0)),
            pl.BlockSpec((1, d), lambda i: (0, 0)),
        ],
        out_specs=pl.BlockSpec((tm, d), lambda i: (i, 0)),
        out_shape=jax.ShapeDtypeStruct((m, d), F32),
        compiler_params=pltpu.CompilerParams(
            dimension_semantics=("arbitrary",), vmem_limit_bytes=VMEM_LIMIT),
        name="outproj_final" if final else "outproj",
    )(o_fox, o_sb, o_nsa, xf, w, g.reshape(1, d))


def _fox_kernel(q_ref, k_ref, v_ref, z_ref, misc_ref, bf_ref, tri_ref,
                o_ref, qq_ref, kk_ref, vpt_ref, cps_ref, *, seq, blk):
    pair = pl.program_id(1)
    low = lax.broadcasted_iota(jnp.int32, (blk, LANES), 1) < HEAD_DIM
    lane_b = lax.broadcasted_iota(jnp.int32, (blk, LANES), 1)
    nblk = seq // blk

    @pl.when(pair == 0)
    def _():
        tri = tri_ref[...]
        part = (lane_b % MISC_PAIR_LANES) % 3
        carry = jnp.zeros((1, LANES), F32)
        for b in range(nblk):
            rows = slice(b * blk, (b + 1) * blk)
            ls = _log_sigmoid(misc_ref[rows, :] + bf_ref[...])
            c3 = _dot(tri, jnp.concatenate(_split3(ls), axis=1))
            cb = c3[:, 0:LANES] + c3[:, LANES:2 * LANES] + c3[:, 2 * LANES:3 * LANES] + carry
            carry = cb[blk - 1:blk, :]
            hi, mid, lo = _split3(cb)
            cps_ref[rows, :] = jnp.where(part == 0, hi.astype(F32), jnp.where(
                part == 1, mid.astype(F32), lo.astype(F32))).astype(BF16)

    rel = lane_b - MISC_PAIR_LANES * pair
    third = jnp.where(rel < 0, -1, jnp.where(rel < 3, 0, jnp.where(rel < 6, 1, jnp.where(
        rel < 9, 2, jnp.where(rel < 12, 3, -1)))))

    def lanes_of(*thirds):
        sel = jnp.zeros((blk, LANES), F32)
        for t in thirds:
            sel = jnp.where(third == t, 1.0, sel)
        return sel.astype(BF16)

    k_ones, k_parts = lanes_of(0, 2), lanes_of(1, 3)
    qa_parts, qa_ones = lanes_of(0), lanes_of(1)
    qb_parts, qb_ones = lanes_of(2), lanes_of(3)
    eye = jnp.where(lax.broadcasted_iota(jnp.int32, (LANES, LANES), 0)
                    == lax.broadcasted_iota(jnp.int32, (LANES, LANES), 1), 1.0, 0.0).astype(BF16)
    for b in range(nblk):
        rows = slice(b * blk, (b + 1) * blk)
        cps = cps_ref[rows, :]
        q2 = q_ref[rows, :]
        v2 = v_ref[rows, :]
        zero = jnp.zeros_like(q2)
        one = jnp.ones_like(q2)
        kk_ref[rows, 0:LANES] = k_ref[rows, :]
        kk_ref[rows, LANES:2 * LANES] = k_ones - cps * k_parts
        qq_ref[0, rows, 0:LANES] = jnp.where(low, q2, zero)
        qq_ref[0, rows, LANES:2 * LANES] = cps * qa_parts + qa_ones
        qq_ref[1, rows, 0:LANES] = jnp.where(low, zero, q2)
        qq_ref[1, rows, LANES:2 * LANES] = cps * qb_parts + qb_ones
        vpt_ref[0, :, rows] = _dot_nt(eye, jnp.where(low, v2, one)).astype(BF16)
        vpt_ref[1, :, rows] = _dot_nt(eye, jnp.where(low, one, v2)).astype(BF16)

    causal_t = (lax.broadcasted_iota(jnp.int32, (blk, blk), 0)
                <= lax.broadcasted_iota(jnp.int32, (blk, blk), 1))

    def scores(i, h):
        r0 = i * blk
        rows = slice(r0, r0 + blk)
        qa = qq_ref[h, rows, :]
        sd = jnp.where(causal_t, _dot_nt(kk_ref[rows, :], qa), -jnp.inf)
        m = jnp.max(sd, axis=0, keepdims=True)
        sm = None
        if i > 0:
            sm = _dot_nt(kk_ref[0:r0, :], qa)
            m = jnp.maximum(m, jnp.max(sm, axis=0, keepdims=True))
        return sd, sm, m

    outs = {}

    def weighted_sum(i, h, st):
        sd, sm, m = st
        rows = slice(i * blk, (i + 1) * blk)
        p = jnp.exp(sd - m).astype(BF16)
        if i > 0:
            p = jnp.concatenate([jnp.exp(sm - m).astype(BF16), p], axis=0)
        a = _dot(vpt_ref[h, :, 0:(i + 1) * blk], p)
        if h == 0:
            outs[i, h] = a[0:HEAD_DIM] / a[HEAD_DIM:LANES]
        else:
            outs[i, h] = a[HEAD_DIM:LANES] / a[0:HEAD_DIM]
            o = jnp.concatenate([outs.pop((i, 0)), outs.pop((i, 1))], axis=0).T
            z = z_ref[rows, :]
            o_ref[rows, :] = (o * (z * _sigmoid(z))).astype(BF16)

    _emit_pipelined([(i, h) for i in range(nblk) for h in range(2)], scores, weighted_sum)


def _fox(pb3, pf3, b_f, blk=256):
    bsz, seq, _ = pb3.shape
    npair = FOX_HEADS // 2
    bias = jnp.zeros((1, LANES), F32).at[:, :MISC_NG].set(
        _misc_ff_layout(b_f.reshape(1, FOX_HEADS), jnp.zeros((1, LANES), F32)))
    tri = jnp.asarray(np.tril(np.ones((blk, blk), np.float32)), BF16)

    def col(base):
        return pl.BlockSpec((None, seq, LANES), lambda b, p: (b, 0, base + p))

    def whole(shape):
        return pl.BlockSpec(shape, lambda b, p: (0,) * len(shape))

    return pl.pallas_call(
        functools.partial(_fox_kernel, seq=seq, blk=blk),
        grid=(bsz, npair),
        in_specs=[
            col(PB_FQ), col(PB_FK), col(PB_FV), col(PF_FZ),
            pl.BlockSpec((None, seq, LANES), lambda b, p: (b, 0, PF_MISC)),
            whole((1, LANES)), whole((blk, blk)),
        ],
        out_specs=pl.BlockSpec((None, seq, LANES), lambda b, p: (b, 0, p)),
        out_shape=jax.ShapeDtypeStruct((bsz, seq, FOX_W), BF16),
        scratch_shapes=[
            pltpu.VMEM((2, seq, 2 * LANES), BF16),
            pltpu.VMEM((seq, 2 * LANES), BF16),
            pltpu.VMEM((2, LANES, seq), BF16),
            pltpu.VMEM((seq, LANES), BF16),
        ],
        compiler_params=pltpu.CompilerParams(
            dimension_semantics=("arbitrary", "arbitrary"), vmem_limit_bytes=VMEM_LIMIT),
        name="fox_attn",
    )(pb3, pb3, pb3, pf3, pf3, bias, tri)


SB_NEAR_TILES = 2
SB_DEAD = -104.0


def _sb_kernel(q_ref, k_ref, v_ref, z_ref, suf_ref, o_ref, acc_ref, carry_ref, *, seq, blk):
    low = lax.broadcasted_iota(jnp.int32, (blk, LANES), 1) < HEAD_DIM
    strict = (lax.broadcasted_iota(jnp.int32, (blk, blk), 1)
              < lax.broadcasted_iota(jnp.int32, (blk, blk), 0))
    suf = suf_ref[...]

    def log_terms(qh, j_hi, j_lo):
        z = _dot_nt(qh, k_ref[j_lo * blk:(j_hi + 1) * blk, :])
        lsz = jnp.minimum(z, 0.0) - jnp.log(1.0 + jnp.exp(-jnp.abs(z)))
        return lsz, lsz - z

    def tiles(i, j_hi, j_lo, carry, terms):
        lsz, l1m = terms
        k0 = j_lo * blk
        out = None
        for j in range(j_hi, j_lo - 1, -1):
            cs = slice(j * blk - k0, (j + 1) * blk - k0)
            l1 = l1m[:, cs]
            if j == i:
                l1 = jnp.where(strict, l1, 0.0)
            r = _dot(l1.astype(BF16), suf)
            a = jnp.exp(lsz[:, cs] + (r + carry))
            if j == i:
                a = jnp.where(strict, a, 0.0)
            pv = _dot(a.astype(BF16), v_ref[j * blk:(j + 1) * blk, :])
            out = pv if out is None else out + pv
            carry = carry + (r[:, 0:1] + l1[:, 0:1])
        return out, carry

    nblk = seq // blk

    def q_head(i, h):
        q2 = q_ref[i * blk:(i + 1) * blk, :]
        zero = jnp.zeros_like(q2)
        return jnp.where(low, q2, zero) if h == 0 else jnp.where(low, zero, q2)

    def j_near(i):
        return max(i - SB_NEAR_TILES + 1, 0)

    alive = {}

    def near_terms(i, h):
        return log_terms(q_head(i, h), i, j_near(i))

    def near_tiles(i, h, terms):
        out, carry = tiles(i, i, j_near(i), jnp.zeros((blk, 1), F32), terms)
        acc_ref[i, h] = out
        if j_near(i) > 0:
            carry_ref[i, h] = jnp.broadcast_to(carry, (blk, LANES))
            alive[i, h] = jnp.max(carry) >= SB_DEAD

    _emit_pipelined([(i, h) for i in range(nblk) for h in range(2)], near_terms, near_tiles)

    for (i, h), flag in alive.items():
        @pl.when(flag)
        def _(i=i, h=h):
            far, _ = tiles(i, j_near(i) - 1, 0, carry_ref[i, h][:, 0:1],
                           log_terms(q_head(i, h), j_near(i) - 1, 0))
            acc_ref[i, h] = acc_ref[i, h] + far

    for i in range(nblk):
        rows = slice(i * blk, (i + 1) * blk)
        o = jnp.where(low, acc_ref[i, 0], acc_ref[i, 1])
        zg = z_ref[rows, :]
        o_ref[rows, :] = (o * (zg * _sigmoid(zg))).astype(BF16)


def _sb(pb3, pf3, blk=256):
    bsz, seq, _ = pb3.shape
    npair = SB_HEADS // 2
    suf = jnp.asarray(np.tril(np.ones((blk, blk), np.float32), -1), BF16)

    def col(base):
        return pl.BlockSpec((None, seq, LANES), lambda b, p: (b, 0, base + p))

    return pl.pallas_call(
        functools.partial(_sb_kernel, seq=seq, blk=blk),
        grid=(bsz, npair),
        in_specs=[
            col(PB_SQ), col(PB_SK), col(PB_SV), col(PF_SZ),
            pl.BlockSpec(suf.shape, lambda b, p: (0, 0)),
        ],
        out_specs=pl.BlockSpec((None, seq, LANES), lambda b, p: (b, 0, p)),
        out_shape=jax.ShapeDtypeStruct((bsz, seq, SB_W), BF16),
        scratch_shapes=[pltpu.VMEM((seq // blk, 2, blk, LANES), F32),
                        pltpu.VMEM((seq // blk, 2, blk, LANES), F32)],
        compiler_params=pltpu.CompilerParams(
            dimension_semantics=("arbitrary", "arbitrary"), vmem_limit_bytes=VMEM_LIMIT),
        name="sb_attn",
    )(pb3, pb3, pb3, pf3, suf)


NSA_HEAD_ORDER = (0, 1, 2, 3)
SEL_LANES = 32
POS_HI_LANE = 32
POS_LO_LANE = 33
NSA_SLOPES = tuple(2.0 ** (-8.0 * (g + 1) / NSA_HEADS) for g in range(NSA_HEADS))


def _nsa_kernel(q_ref, ks_ref, vs_ref, kw_ref, vw_ref, z_ref, misc_ref, xk_ref, xv_ref,
                posk_ref, w1k_ref, w2k_ref, posv_ref, w1v_ref, w2v_ref,
                mselt_ref, kaugs_ref, kaugw_ref, qslope_ref,
                o_ref, kc_ref, vct_ref, kks_ref, kkw_ref, vst_ref, vwt_ref, *, seq, blk):
    cols4 = NSA_HEADS * blk
    n_cmp = (seq - CMP_BLOCK) // CMP_STRIDE + 1
    n_sel = seq // SEL_BLOCK
    nwin = WINDOW // blk
    eye = jnp.where(lax.broadcasted_iota(jnp.int32, (LANES, LANES), 0)
                    == lax.broadcasted_iota(jnp.int32, (LANES, LANES), 1), 1.0, 0.0).astype(BF16)

    def compress(x_ref, pos_ref, w1_ref, w2_ref):
        x = x_ref[...]
        a = _dot((x + pos_ref[0:1, :]).astype(BF16), w1_ref[0])
        b = _dot((x + pos_ref[1:2, :]).astype(BF16), w1_ref[1])
        hid = a + pltpu.roll(b, b.shape[0] - 1, axis=0)
        hid = hid * _sigmoid(hid)
        return _dot(hid.astype(BF16), w2_ref[...]).astype(BF16)

    kc_ref[...] = compress(xk_ref, posk_ref, w1k_ref, w2k_ref)
    vct_ref[...] = _dot_nt(eye, compress(xv_ref, posv_ref, w1v_ref, w2v_ref)).astype(BF16)

    kks_ref[:, 0:LANES] = ks_ref[...]
    kks_ref[:, LANES:2 * LANES] = kaugs_ref[...]
    kkw_ref[:, 0:LANES] = kw_ref[...]
    kkw_ref[:, LANES:2 * LANES] = kaugw_ref[...]
    tb = 2 * LANES
    low_t = lax.broadcasted_iota(jnp.int32, (tb, LANES), 1) < HEAD_DIM
    for b in range(seq // tb):
        rows = slice(b * tb, (b + 1) * tb)
        vs = vs_ref[rows, :]
        vw = vw_ref[rows, :]
        one = jnp.ones_like(vs)
        vst_ref[:, rows] = _dot_nt(eye, jnp.where(low_t, vs, one)).astype(BF16)
        vwt_ref[:, rows] = _dot_nt(eye, jnp.where(low_t, vw, one)).astype(BF16)

    low = lax.broadcasted_iota(jnp.int32, (blk, LANES), 1) < HEAD_DIM
    colt = lax.broadcasted_iota(jnp.int32, (1, cols4), 1)
    grp = colt // blk
    tok = colt - grp * blk
    gslope = [NSA_SLOPES[g] for g in NSA_HEAD_ORDER]
    slope = jnp.where(grp == 0, gslope[0], jnp.where(grp == 1, gslope[1],
                      jnp.where(grp == 2, gslope[2], gslope[3]))).astype(F32)
    keyr = lax.broadcasted_iota(jnp.int32, (blk, 1), 0)
    causal_t = keyr <= tok
    after_t = keyr > tok
    cmpr = lax.broadcasted_iota(jnp.int32, (LANES, 1), 0)
    cmp_end = (cmpr * CMP_STRIDE + (CMP_BLOCK - 1)).astype(F32)
    jrow = lax.broadcasted_iota(jnp.int32, (n_sel, blk), 0)
    tcol = lax.broadcasted_iota(jnp.int32, (n_sel, blk), 1)
    sel_lane = lax.broadcasted_iota(jnp.int32, (cols4, LANES), 1) < SEL_LANES
    qslope = qslope_ref[...]

    def col_max(parts):
        m = None
        for s in parts:
            mp = jnp.max(s, axis=0, keepdims=True)
            m = mp if m is None else jnp.maximum(m, mp)
        return m

    def softmax_pv(parts, m, keys, vt_ref):
        p = [jnp.exp(s - m).astype(BF16) for s in parts]
        p = p[0] if len(p) == 1 else jnp.concatenate(p, axis=0)
        acc = _dot(vt_ref[:, keys], p)
        return acc[0:HEAD_DIM] / acc[HEAD_DIM:LANES]

    def scores(i):
        qs = i * blk
        rows = slice(qs, qs + blk)
        q01 = q_ref[rows, 0:LANES]
        q23 = q_ref[rows, LANES:2 * LANES]
        zero = jnp.zeros_like(q01)
        qst = jnp.concatenate([jnp.where(low, q01, zero), jnp.where(low, zero, q01),
                               jnp.where(low, q23, zero), jnp.where(low, zero, q23)], axis=0)

        dist_c = (qs + tok).astype(F32) - cmp_end
        valid_c = (dist_c >= 0.0) & (cmpr < n_cmp)
        sc = _dot_nt(kc_ref[...], qst) - slope * dist_c
        sc = jnp.where(valid_c, sc, -jnp.inf)
        mc = jnp.max(sc, axis=0, keepdims=True)
        mc = jnp.where(mc == -jnp.inf, 0.0, mc)
        pc = jnp.exp(sc - mc)
        ssum = jnp.sum(pc, axis=0, keepdims=True)
        pc = pc / jnp.where(ssum > 0.0, ssum, 1.0)
        oc = _dot(vct_ref[...], pc.astype(BF16))[0:HEAD_DIM]

        pcs = (pc[:, 0:blk] + pc[:, blk:2 * blk] + pc[:, 2 * blk:3 * blk]
               + pc[:, 3 * blk:4 * blk])
        hi, lo = _split2(pcs)
        imp = _dot(mselt_ref[...], jnp.concatenate([hi, lo], axis=0))
        back = (qs + tcol) // SEL_BLOCK - jrow
        imp = jnp.where(back < 0, -jnp.inf,
                        jnp.where(back < SEL_N_LOCAL, jnp.inf,
                                  jnp.where(jrow == 0, jnp.inf, imp)))
        rank = jnp.zeros((n_sel, blk), F32)
        for c in range(n_sel):
            rowc = imp[c:c + 1, :]
            tie = jnp.where(jrow > c, 1.0, 0.0)
            rank = rank + jnp.where(rowc > imp, 1.0, jnp.where(rowc == imp, tie, 0.0))
        unsel_t = jnp.where(rank >= float(SEL_TOPK), 1.0, 0.0).astype(BF16)
        unsel_t = jnp.concatenate([unsel_t, jnp.zeros((LANES - n_sel, blk), BF16)], axis=0)
        unsel = _dot_nt(eye, unsel_t).astype(BF16)

        qop_w = jnp.concatenate([qst, qslope], axis=1)
        qaug = jnp.where(sel_lane, jnp.concatenate([unsel] * NSA_HEADS, axis=0), qslope)
        qop_s = jnp.concatenate([qst, qaug], axis=1)

        win = []
        w0 = max(qs - WINDOW, 0)
        if i >= nwin:
            win.append(jnp.where(after_t, _dot_nt(kkw_ref[w0:w0 + blk, :], qop_w), -jnp.inf))
        wm = max(qs - WINDOW + blk, 0)
        if qs > wm:
            win.append(_dot_nt(kkw_ref[wm:qs, :], qop_w))
        win.append(jnp.where(causal_t, _dot_nt(kkw_ref[rows, :], qop_w), -jnp.inf))

        sel = []
        if i > 0:
            sel.append(_dot_nt(kks_ref[0:qs, :], qop_s))
        sel.append(jnp.where(causal_t, _dot_nt(kks_ref[rows, :], qop_s), -jnp.inf))
        return oc, win, col_max(win), sel, col_max(sel)

    def outputs(i, st):
        oc, win, m_win, sel, m_sel = st
        qs = i * blk
        rows = slice(qs, qs + blk)
        o_win = softmax_pv(win, m_win, slice(max(qs - WINDOW, 0), qs + blk), vwt_ref)
        o_sel = softmax_pv(sel, m_sel, slice(0, qs + blk), vst_ref)

        sg_t = _sigmoid(misc_ref[rows, :]).T

        def gate(branch):
            return jnp.concatenate(
                [sg_t[MISC_NG + 3 * g + branch:MISC_NG + 3 * g + branch + 1, :]
                 for g in NSA_HEAD_ORDER], axis=1)

        tot = oc * gate(0) + o_sel * gate(1) + o_win * gate(2)
        by_head = {g: tot[:, k * blk:(k + 1) * blk] for k, g in enumerate(NSA_HEAD_ORDER)}
        out = jnp.concatenate([by_head[g] for g in range(NSA_HEADS)], axis=0).T
        z = z_ref[rows, :]
        o_ref[rows, :] = (out * (z * _sigmoid(z))).astype(BF16)

    _emit_pipelined([(i,) for i in range(seq // blk)], scores, outputs, depth=1)


def _nsa_constants(seq, blk):
    n_cmp = (seq - CMP_BLOCK) // CMP_STRIDE + 1
    n_sel = seq // SEL_BLOCK
    cs = np.arange(n_cmp) * CMP_STRIDE
    ce = cs + CMP_BLOCK - 1
    ss = np.arange(n_sel) * SEL_BLOCK
    se = ss + SEL_BLOCK - 1
    msel_t = np.zeros((n_sel, LANES), np.float32)
    msel_t[:, :n_cmp] = ((cs[:, None] <= se[None, :]) & (ce[:, None] >= ss[None, :])).T
    msel_t = np.concatenate([msel_t, msel_t], axis=1)
    key = np.arange(seq)
    kaug_w = np.zeros((seq, LANES), np.float32)
    kaug_w[:, POS_HI_LANE] = key // 16
    kaug_w[:, POS_LO_LANE] = key % 16
    kaug_s = kaug_w.copy()
    kaug_s[key, key // SEL_BLOCK] = PEN
    qslope = np.zeros((NSA_HEADS * blk, LANES), np.float32)
    for k, g in enumerate(NSA_HEAD_ORDER):
        qslope[k * blk:(k + 1) * blk, POS_HI_LANE] = 16.0 * NSA_SLOPES[g]
        qslope[k * blk:(k + 1) * blk, POS_LO_LANE] = NSA_SLOPES[g]
    return (jnp.asarray(msel_t, BF16), jnp.asarray(kaug_s, BF16), jnp.asarray(kaug_w, BF16),
            jnp.asarray(qslope, BF16))


def _nsa(pb3, pf3, pos_k, w1_k, w2_k, pos_v, w1_v, w2_v, blk=128):
    bsz, seq, _ = pb3.shape
    nchunk = seq // CMP_STRIDE
    cw = CMP_STRIDE * HEAD_DIM
    kvc = pf3[:, :, PF_KVC * LANES:(PF_KVC + 1) * LANES]
    xk = kvc[:, :, :HEAD_DIM].reshape(bsz, nchunk, cw)
    xv = kvc[:, :, HEAD_DIM:].reshape(bsz, nchunk, cw)

    def prep(pos, w1, w2):
        w1d = jnp.concatenate([w1, w1], axis=1).astype(BF16).reshape(2, cw, LANES)
        w2p = jnp.zeros((LANES, LANES), F32).at[:HEAD_DIM, :].set(
            jnp.concatenate([w2, w2], axis=1)).astype(BF16)
        return pos.reshape(2, cw), w1d, w2p

    consts = _nsa_constants(seq, blk)

    def col(base, nblk=1):
        return pl.BlockSpec((None, seq, nblk * LANES), lambda b: (b, 0, base // nblk))

    def whole(shape):
        return pl.BlockSpec(shape, lambda b: (0,) * len(shape))

    return pl.pallas_call(
        functools.partial(_nsa_kernel, seq=seq, blk=blk),
        grid=(bsz,),
        in_specs=[
            col(PB_NQ, 2), col(PB_NKS), col(PB_NVS), col(PB_NKW), col(PB_NVW),
            col(PF_NZ, 2), col(PF_MISC),
            pl.BlockSpec((None, nchunk, cw), lambda b: (b, 0, 0)),
            pl.BlockSpec((None, nchunk, cw), lambda b: (b, 0, 0)),
            whole((2, cw)), whole((2, cw, LANES)), whole((LANES, LANES)),
            whole((2, cw)), whole((2, cw, LANES)), whole((LANES, LANES)),
        ] + [whole(c.shape) for c in consts],
        out_specs=pl.BlockSpec((None, seq, NSA_W), lambda b: (b, 0, 0)),
        out_shape=jax.ShapeDtypeStruct((bsz, seq, NSA_W), BF16),
        scratch_shapes=[
            pltpu.VMEM((nchunk, LANES), BF16),
            pltpu.VMEM((LANES, nchunk), BF16),
            pltpu.VMEM((seq, 2 * LANES), BF16),
            pltpu.VMEM((seq, 2 * LANES), BF16),
            pltpu.VMEM((LANES, seq), BF16),
            pltpu.VMEM((LANES, seq), BF16),
        ],
        compiler_params=pltpu.CompilerParams(
            dimension_semantics=("arbitrary",), vmem_limit_bytes=VMEM_LIMIT),
        name="nsa_attn",
    )(pb3, pb3, pb3, pb3, pb3, pf3, pf3, xk, xv,
      *prep(pos_k, w1_k, w2_k), *prep(pos_v, w1_v, w2_v), *consts)


def kernel(x, norm_g, w_in, b_f, cmp_pos_k, cmp_w1_k, cmp_w2_k,
           cmp_pos_v, cmp_w1_v, cmp_w2_v, w_out, final_g):
    bsz, seq, d = x.shape
    xf = x.reshape(bsz * seq, d)
    for l in range(DEPTH):
        pb, pf = _inproj(xf, norm_g[l], _pack_w_in(w_in[l]))
        pb3 = pb.reshape(bsz, seq, NB_COLS)
        pf3 = pf.reshape(bsz, seq, NF_COLS)
        o_fox = _fox(pb3, pf3, b_f[l])
        o_sb = _sb(pb3, pf3)
        o_nsa = _nsa(pb3, pf3, cmp_pos_k[l], cmp_w1_k[l], cmp_w2_k[l],
                     cmp_pos_v[l], cmp_w1_v[l], cmp_w2_v[l])
        xf = _outproj(o_fox.reshape(bsz * seq, FOX_W), o_sb.reshape(bsz * seq, SB_W),
                      o_nsa.reshape(bsz * seq, NSA_W), xf, w_out[l].astype(BF16),
                      final_g, final=(l == DEPTH - 1))
    return xf.reshape(bsz, seq, d)
```

```python
import functools

import numpy as np
import jax
import jax.numpy as jnp
from jax import lax
from jax.experimental import pallas as pl
from jax.experimental.pallas import tpu as pltpu

F32 = jnp.float32
BF16 = jnp.bfloat16

D_MODEL = 1024
DEPTH = 2
HEAD_DIM = 64
LANES = 128
FOX_HEADS = 6
SB_HEADS = 6
NSA_HEADS = 4
FOX_W = FOX_HEADS * HEAD_DIM
SB_W = SB_HEADS * HEAD_DIM
NSA_W = NSA_HEADS * HEAD_DIM
CMP_BLOCK = 32
CMP_STRIDE = 16
SEL_BLOCK = 64
SEL_TOPK = 8
SEL_N_LOCAL = 2
WINDOW = 512
NORM_EPS = 1e-6
QK_SCALE = HEAD_DIM ** -0.5
NEG_BIG = -1e30
PEN = -(2.0 ** 100)

PB_FQ, PB_FK, PB_FV, PB_SQ, PB_SK, PB_SV, PB_NQ, PB_NSEL, PB_NWIN = (
    0, 3, 6, 9, 12, 15, 18, 20, 21)
PB_BLOCKS = 22
PF_FZ, PF_SZ, PF_NZ, PF_KVC, PF_MISC = 0, 3, 6, 8, 9
PF_BLOCKS = 10
NB_COLS = PB_BLOCKS * LANES
NF_COLS = PF_BLOCKS * LANES
MISC_PAIR_LANES = 16
MISC_FF_COPIES = 6
MISC_NG = MISC_PAIR_LANES * (FOX_HEADS // 2)

VMEM_PHYSICAL = 64 * 1024 * 1024
VMEM_LIMIT = VMEM_PHYSICAL - 4 * 1024 * 1024

_NT = (((1,), (1,)), ((), ()))


def _dot(a, b):
    return jnp.dot(a, b, preferred_element_type=F32)


def _dot_nt(a, b):
    return lax.dot_general(a, b, _NT, preferred_element_type=F32)


def _sigmoid(x):
    return 1.0 / (1.0 + jnp.exp(-x))


def _log_sigmoid(x):
    return -(jnp.maximum(-x, 0.0) + jnp.log1p(jnp.exp(-jnp.abs(x))))


def _split3(x):
    hi = x.astype(BF16)
    r = x - hi.astype(F32)
    mid = r.astype(BF16)
    lo = (r - mid.astype(F32)).astype(BF16)
    return hi, mid, lo


def _split2(x):
    hi = x.astype(BF16)
    lo = (x - hi.astype(F32)).astype(BF16)
    return hi, lo


def _rep(x, n):
    return x if n == 1 else jnp.concatenate([x] * n, axis=1)


PIPELINE_DEPTH = 4


def _emit_pipelined(items, first, second, depth=PIPELINE_DEPTH):
    pending = {}
    for n, item in enumerate(items):
        pending[item] = first(*item)
        if n >= depth:
            prev = items[n - depth]
            second(*prev, pending.pop(prev))
    for prev in items[max(len(items) - depth, 0):]:
        second(*prev, pending.pop(prev))


def _inproj_kernel(x_ref, g_ref, w_ref, pb_ref, pf_ref):
    x = x_ref[...]
    ms = jnp.mean(x * x, axis=-1, keepdims=True)
    h = (x * lax.rsqrt(ms + NORM_EPS) * g_ref[...]).astype(BF16)
    chunk = 4 * LANES
    for c in range(0, NB_COLS, chunk):
        e = min(c + chunk, NB_COLS)
        pb_ref[:, c:e] = _dot(h, w_ref[:, c:e]).astype(BF16)
    for c in range(0, NF_COLS, chunk):
        e = min(c + chunk, NF_COLS)
        pf_ref[:, c:e] = _dot(h, w_ref[:, NB_COLS + c:NB_COLS + e])


def _inproj(xf, g, w_all, tm=512):
    m, d = xf.shape
    return pl.pallas_call(
        _inproj_kernel,
        grid=(m // tm,),
        in_specs=[
            pl.BlockSpec((tm, d), lambda i: (i, 0)),
            pl.BlockSpec((1, d), lambda i: (0, 0)),
            pl.BlockSpec((d, NB_COLS + NF_COLS), lambda i: (0, 0)),
        ],
        out_specs=[
            pl.BlockSpec((tm, NB_COLS), lambda i: (i, 0)),
            pl.BlockSpec((tm, NF_COLS), lambda i: (i, 0)),
        ],
        out_shape=[
            jax.ShapeDtypeStruct((m, NB_COLS), BF16),
            jax.ShapeDtypeStruct((m, NF_COLS), F32),
        ],
        compiler_params=pltpu.CompilerParams(
            dimension_semantics=("arbitrary",), vmem_limit_bytes=VMEM_LIMIT),
        name="inproj",
    )(xf, g.reshape(1, d), w_all)


def _pack_w_in(w):
    sizes = (FOX_W, FOX_W, FOX_W, FOX_HEADS, FOX_W, SB_W, SB_W, SB_W, SB_W,
             NSA_W, HEAD_DIM, HEAD_DIM, HEAD_DIM, HEAD_DIM, HEAD_DIM, HEAD_DIM,
             3 * NSA_HEADS, NSA_W)
    offs = np.concatenate([[0], np.cumsum(sizes)])
    (fq, fk, fv, ff, fz, sq, sk, sv, sz, nq, nkc, nvc, nks, nvs, nkw, nvw, ng, nz) = [
        w[:, offs[i]:offs[i + 1]] for i in range(len(sizes))]
    zeros = jnp.zeros((w.shape[0], LANES), w.dtype)
    cols = [fq * QK_SCALE, fk, fv, sq * QK_SCALE, sk, sv, nq * QK_SCALE,
            nks, nvs, nkw, nvw,
            fz, sz, nz, nkc, nvc, _misc_ff_layout(ff, zeros), ng,
            zeros[:, :LANES - MISC_NG - 3 * NSA_HEADS]]
    return jnp.concatenate(cols, axis=1).astype(BF16)


def _misc_ff_layout(ff, zeros):
    cols = []
    for p in range(FOX_HEADS // 2):
        cols += [ff[:, 2 * p:2 * p + 1]] * MISC_FF_COPIES + [ff[:, 2 * p + 1:2 * p + 2]] * MISC_FF_COPIES
        cols.append(zeros[:, :MISC_PAIR_LANES - 2 * MISC_FF_COPIES])
    return jnp.concatenate(cols, axis=1)


def _outproj_kernel(of_ref, os_ref, on_ref, x_ref, w_ref, g_ref, o_ref, *, final):
    y = (x_ref[...]
         + _dot(of_ref[...], w_ref[0:FOX_W, :])
         + _dot(os_ref[...], w_ref[FOX_W:FOX_W + SB_W, :])
         + _dot(on_ref[...], w_ref[FOX_W + SB_W:, :]))
    if final:
        ms = jnp.mean(y * y, axis=-1, keepdims=True)
        y = y * lax.rsqrt(ms + NORM_EPS) * g_ref[...]
    o_ref[...] = y


def _outproj(o_fox, o_sb, o_nsa, xf, w, g, final, tm=512):
    m, d = xf.shape
    return pl.pallas_call(
        functools.partial(_outproj_kernel, final=final),
        grid=(m // tm,),
        in_specs=[
            pl.BlockSpec((tm, FOX_W), lambda i: (i, 0)),
            pl.BlockSpec((tm, SB_W), lambda i: (i, 0)),
            pl.BlockSpec((tm, NSA_W), lambda i: (i, 0)),
            pl.BlockSpec((tm, d), lambda i: (i, 0)),
            pl.BlockSpec((d, d), lambda i: (0, 0)),
            pl.BlockSpec((1, d), lambda i: (0, 0)),
        ],
        out_specs=pl.BlockSpec((tm, d), lambda i: (i, 0)),
        out_shape=jax.ShapeDtypeStruct((m, d), F32),
        compiler_params=pltpu.CompilerParams(
            dimension_semantics=("arbitrary",), vmem_limit_bytes=VMEM_LIMIT),
        name="outproj_final" if final else "outproj",
    )(o_fox, o_sb, o_nsa, xf, w, g.reshape(1, d))


def _fox_kernel(q_ref, k_ref, v_ref, z_ref, misc_ref, bf_ref, tri_ref,
                o_ref, qq_ref, kk_ref, vpt_ref, cps_ref, *, seq, blk):
    pair = pl.program_id(1)
    low = lax.broadcasted_iota(jnp.int32, (blk, LANES), 1) < HEAD_DIM
    lane_b = lax.broadcasted_iota(jnp.int32, (blk, LANES), 1)
    nblk = seq // blk

    @pl.when(pair == 0)
    def _():
        tri = tri_ref[...]
        part = (lane_b % MISC_PAIR_LANES) % 3
        carry = jnp.zeros((1, LANES), F32)
        for b in range(nblk):
            rows = slice(b * blk, (b + 1) * blk)
            ls = _log_sigmoid(misc_ref[rows, :] + bf_ref[...])
            c3 = _dot(tri, jnp.concatenate(_split3(ls), axis=1))
            cb = c3[:, 0:LANES] + c3[:, LANES:2 * LANES] + c3[:, 2 * LANES:3 * LANES] + carry
            carry = cb[blk - 1:blk, :]
            hi, mid, lo = _split3(cb)
            cps_ref[rows, :] = jnp.where(part == 0, hi.astype(F32), jnp.where(
                part == 1, mid.astype(F32), lo.astype(F32))).astype(BF16)

    rel = lane_b - MISC_PAIR_LANES * pair
    third = jnp.where(rel < 0, -1, jnp.where(rel < 3, 0, jnp.where(rel < 6, 1, jnp.where(
        rel < 9, 2, jnp.where(rel < 12, 3, -1)))))

    def lanes_of(*thirds):
        sel = jnp.zeros((blk, LANES), F32)
        for t in thirds:
            sel = jnp.where(third == t, 1.0, sel)
        return sel.astype(BF16)

    k_ones, k_parts = lanes_of(0, 2), lanes_of(1, 3)
    qa_parts, qa_ones = lanes_of(0), lanes_of(1)
    qb_parts, qb_ones = lanes_of(2), lanes_of(3)
    eye = jnp.where(lax.broadcasted_iota(jnp.int32, (LANES, LANES), 0)
                    == lax.broadcasted_iota(jnp.int32, (LANES, LANES), 1), 1.0, 0.0).astype(BF16)
    for b in range(nblk):
        rows = slice(b * blk, (b + 1) * blk)
        cps = cps_ref[rows, :]
        q2 = q_ref[rows, :]
        v2 = v_ref[rows, :]
        zero = jnp.zeros_like(q2)
        one = jnp.ones_like(q2)
        kk_ref[rows, 0:LANES] = k_ref[rows, :]
        kk_ref[rows, LANES:2 * LANES] = k_ones - cps * k_parts
        qq_ref[0, rows, 0:LANES] = jnp.where(low, q2, zero)
        qq_ref[0, rows, LANES:2 * LANES] = cps * qa_parts + qa_ones
        qq_ref[1, rows, 0:LANES] = jnp.where(low, zero, q2)
        qq_ref[1, rows, LANES:2 * LANES] = cps * qb_parts + qb_ones
        vpt_ref[0, :, rows] = _dot_nt(eye, jnp.where(low, v2, one)).astype(BF16)
        vpt_ref[1, :, rows] = _dot_nt(eye, jnp.where(low, one, v2)).astype(BF16)

    causal_t = (lax.broadcasted_iota(jnp.int32, (blk, blk), 0)
                <= lax.broadcasted_iota(jnp.int32, (blk, blk), 1))

    def scores(i, h):
        r0 = i * blk
        rows = slice(r0, r0 + blk)
        qa = qq_ref[h, rows, :]
        sd = jnp.where(causal_t, _dot_nt(kk_ref[rows, :], qa), -jnp.inf)
        m = jnp.max(sd, axis=0, keepdims=True)
        sm = None
        if i > 0:
            sm = _dot_nt(kk_ref[0:r0, :], qa)
            m = jnp.maximum(m, jnp.max(sm, axis=0, keepdims=True))
        return sd, sm, m

    outs = {}

    def weighted_sum(i, h, st):
        sd, sm, m = st
        rows = slice(i * blk, (i + 1) * blk)
        p = jnp.exp(sd - m).astype(BF16)
        if i > 0:
            p = jnp.concatenate([jnp.exp(sm - m).astype(BF16), p], axis=0)
        a = _dot(vpt_ref[h, :, 0:(i + 1) * blk], p)
        if h == 0:
            outs[i, h] = a[0:HEAD_DIM] / a[HEAD_DIM:LANES]
        else:
            outs[i, h] = a[HEAD_DIM:LANES] / a[0:HEAD_DIM]
            o = jnp.concatenate([outs.pop((i, 0)), outs.pop((i, 1))], axis=0).T
            z = z_ref[rows, :]
            o_ref[rows, :] = (o * (z * _sigmoid(z))).astype(BF16)

    _emit_pipelined([(i, h) for i in range(nblk) for h in range(2)], scores, weighted_sum)


def _fox(pb3, pf3, b_f, blk=256):
    bsz, seq, _ = pb3.shape
    npair = FOX_HEADS // 2
    bias = jnp.zeros((1, LANES), F32).at[:, :MISC_NG].set(
        _misc_ff_layout(b_f.reshape(1, FOX_HEADS), jnp.zeros((1, LANES), F32)))
    tri = jnp.asarray(np.tril(np.ones((blk, blk), np.float32)), BF16)

    def col(base):
        return pl.BlockSpec((None, seq, LANES), lambda b, p: (b, 0, base + p))

    def whole(shape):
        return pl.BlockSpec(shape, lambda b, p: (0,) * len(shape))

    return pl.pallas_call(
        functools.partial(_fox_kernel, seq=seq, blk=blk),
        grid=(bsz, npair),
        in_specs=[
            col(PB_FQ), col(PB_FK), col(PB_FV), col(PF_FZ),
            pl.BlockSpec((None, seq, LANES), lambda b, p: (b, 0, PF_MISC)),
            whole((1, LANES)), whole((blk, blk)),
        ],
        out_specs=pl.BlockSpec((None, seq, LANES), lambda b, p: (b, 0, p)),
        out_shape=jax.ShapeDtypeStruct((bsz, seq, FOX_W), BF16),
        scratch_shapes=[
            pltpu.VMEM((2, seq, 2 * LANES), BF16),
            pltpu.VMEM((seq, 2 * LANES), BF16),
            pltpu.VMEM((2, LANES, seq), BF16),
            pltpu.VMEM((seq, LANES), BF16),
        ],
        compiler_params=pltpu.CompilerParams(
            dimension_semantics=("arbitrary", "arbitrary"), vmem_limit_bytes=VMEM_LIMIT),
        name="fox_attn",
    )(pb3, pb3, pb3, pf3, pf3, bias, tri)


SB_NEAR_TILES = 2
SB_DEAD = -104.0


def _sb_kernel(q_ref, k_ref, v_ref, z_ref, suf_ref, o_ref, acc_ref, carry_ref, *, seq, blk):
    low = lax.broadcasted_iota(jnp.int32, (blk, LANES), 1) < HEAD_DIM
    strict = (lax.broadcasted_iota(jnp.int32, (blk, blk), 1)
              < lax.broadcasted_iota(jnp.int32, (blk, blk), 0))
    suf = suf_ref[...]

    def log_terms(qh, j_hi, j_lo):
        z = _dot_nt(qh, k_ref[j_lo * blk:(j_hi + 1) * blk, :])
        lsz = jnp.minimum(z, 0.0) - jnp.log(1.0 + jnp.exp(-jnp.abs(z)))
        return lsz, lsz - z

    def tiles(i, j_hi, j_lo, carry, terms):
        lsz, l1m = terms
        k0 = j_lo * blk
        out = None
        for j in range(j_hi, j_lo - 1, -1):
            cs = slice(j * blk - k0, (j + 1) * blk - k0)
            l1 = l1m[:, cs]
            if j == i:
                l1 = jnp.where(strict, l1, 0.0)
            r = _dot(l1.astype(BF16), suf)
            a = jnp.exp(lsz[:, cs] + (r + carry))
            if j == i:
                a = jnp.where(strict, a, 0.0)
            pv = _dot(a.astype(BF16), v_ref[j * blk:(j + 1) * blk, :])
            out = pv if out is None else out + pv
            carry = carry + (r[:, 0:1] + l1[:, 0:1])
        return out, carry

    nblk = seq // blk

    def q_head(i, h):
        q2 = q_ref[i * blk:(i + 1) * blk, :]
        zero = jnp.zeros_like(q2)
        return jnp.where(low, q2, zero) if h == 0 else jnp.where(low, zero, q2)

    def j_near(i):
        return max(i - SB_NEAR_TILES + 1, 0)

    alive = {}

    def near_terms(i, h):
        return log_terms(q_head(i, h), i, j_near(i))

    def near_tiles(i, h, terms):
        out, carry = tiles(i, i, j_near(i), jnp.zeros((blk, 1), F32), terms)
        acc_ref[i, h] = out
        if j_near(i) > 0:
            carry_ref[i, h] = jnp.broadcast_to(carry, (blk, LANES))
            alive[i, h] = jnp.max(carry) >= SB_DEAD

    _emit_pipelined([(i, h) for i in range(nblk) for h in range(2)], near_terms, near_tiles)

    for (i, h), flag in alive.items():
        @pl.when(flag)
        def _(i=i, h=h):
            far, _ = tiles(i, j_near(i) - 1, 0, carry_ref[i, h][:, 0:1],
                           log_terms(q_head(i, h), j_near(i) - 1, 0))
            acc_ref[i, h] = acc_ref[i, h] + far

    for i in range(nblk):
        rows = slice(i * blk, (i + 1) * blk)
        o = jnp.where(low, acc_ref[i, 0], acc_ref[i, 1])
        zg = z_ref[rows, :]
        o_ref[rows, :] = (o * (zg * _sigmoid(zg))).astype(BF16)


def _sb(pb3, pf3, blk=256):
    bsz, seq, _ = pb3.shape
    npair = SB_HEADS // 2
    suf = jnp.asarray(np.tril(np.ones((blk, blk), np.float32), -1), BF16)

    def col(base):
        return pl.BlockSpec((None, seq, LANES), lambda b, p: (b, 0, base + p))

    return pl.pallas_call(
        functools.partial(_sb_kernel, seq=seq, blk=blk),
        grid=(bsz, npair),
        in_specs=[
            col(PB_SQ), col(PB_SK), col(PB_SV), col(PF_SZ),
            pl.BlockSpec(suf.shape, lambda b, p: (0, 0)),
        ],
        out_specs=pl.BlockSpec((None, seq, LANES), lambda b, p: (b, 0, p)),
        out_shape=jax.ShapeDtypeStruct((bsz, seq, SB_W), BF16),
        scratch_shapes=[pltpu.VMEM((seq // blk, 2, blk, LANES), F32),
                        pltpu.VMEM((seq // blk, 2, blk, LANES), F32)],
        compiler_params=pltpu.CompilerParams(
            dimension_semantics=("arbitrary", "arbitrary"), vmem_limit_bytes=VMEM_LIMIT),
        name="sb_attn",
    )(pb3, pb3, pb3, pf3, suf)


NSA_HEAD_ORDER = (0, 1, 2, 3)
SEL_LANES = 32
POS_HI_LANE = 32
POS_LO_LANE = 33
NSA_SLOPES = tuple(2.0 ** (-8.0 * (g + 1) / NSA_HEADS) for g in range(NSA_HEADS))


def _nsa_kernel(q_ref, ksv_ref, kwv_ref, z_ref, misc_ref, kvc_ref,
                pos_ref, w1_ref, w2k_ref, w2v_ref,
                mselt_ref, kaugs_ref, kaugw_ref, qslope_ref,
                o_ref, kc_ref, vct_ref, kks_ref, kkw_ref, vst_ref, vwt_ref, *, seq, blk):
    cols4 = NSA_HEADS * blk
    n_cmp = (seq - CMP_BLOCK) // CMP_STRIDE + 1
    n_sel = seq // SEL_BLOCK
    nwin = WINDOW // blk
    eye = jnp.where(lax.broadcasted_iota(jnp.int32, (LANES, LANES), 0)
                    == lax.broadcasted_iota(jnp.int32, (LANES, LANES), 1), 1.0, 0.0).astype(BF16)

    nchunk = seq // CMP_STRIDE
    ha = jnp.zeros((nchunk, LANES), F32)
    hb = jnp.zeros((nchunk, LANES), F32)
    for p in range(CMP_STRIDE):
        xp = kvc_ref[pl.ds(p, nchunk, stride=CMP_STRIDE), :]
        q = p + CMP_STRIDE
        ha = ha + _dot((xp + pos_ref[p:p + 1, :]).astype(BF16), w1_ref[p])
        hb = hb + _dot((xp + pos_ref[q:q + 1, :]).astype(BF16), w1_ref[q])
    hid = ha + pltpu.roll(hb, nchunk - 1, axis=0)
    hid = (hid * _sigmoid(hid)).astype(BF16)
    kc_ref[...] = _dot(hid, w2k_ref[...]).astype(BF16)
    vct_ref[...] = _dot_nt(eye, _dot(hid, w2v_ref[...]).astype(BF16)).astype(BF16)

    tb = 2 * LANES
    lane_t = lax.broadcasted_iota(jnp.int32, (tb, LANES), 1)
    low_t = lane_t < HEAD_DIM
    er = lax.broadcasted_iota(jnp.int32, (LANES, LANES), 0)
    ec = lax.broadcasted_iota(jnp.int32, (LANES, LANES), 1)
    dup_lo = jnp.where(er == ec % HEAD_DIM, 1.0, 0.0).astype(BF16)
    swap = jnp.where(ec == (er + HEAD_DIM) % LANES, 1.0, 0.0).astype(BF16)
    kks_ref[:, LANES:2 * LANES] = kaugs_ref[...]
    kkw_ref[:, LANES:2 * LANES] = kaugw_ref[...]
    for b in range(seq // tb):
        rows = slice(b * tb, (b + 1) * tb)
        xs = ksv_ref[rows, :]
        xw = kwv_ref[rows, :]
        one = jnp.ones_like(xs)
        kks_ref[rows, 0:LANES] = _dot(xs, dup_lo).astype(BF16)
        kkw_ref[rows, 0:LANES] = _dot(xw, dup_lo).astype(BF16)
        vst_ref[:, rows] = _dot_nt(swap, jnp.where(low_t, one, xs)).astype(BF16)
        vwt_ref[:, rows] = _dot_nt(swap, jnp.where(low_t, one, xw)).astype(BF16)

    low = lax.broadcasted_iota(jnp.int32, (blk, LANES), 1) < HEAD_DIM
    colt = lax.broadcasted_iota(jnp.int32, (1, cols4), 1)
    grp = colt // blk
    tok = colt - grp * blk
    gslope = [NSA_SLOPES[g] for g in NSA_HEAD_ORDER]
    slope = jnp.where(grp == 0, gslope[0], jnp.where(grp == 1, gslope[1],
                      jnp.where(grp == 2, gslope[2], gslope[3]))).astype(F32)
    keyr = lax.broadcasted_iota(jnp.int32, (blk, 1), 0)
    causal_t = keyr <= tok
    after_t = keyr > tok
    cmpr = lax.broadcasted_iota(jnp.int32, (LANES, 1), 0)
    cmp_end = (cmpr * CMP_STRIDE + (CMP_BLOCK - 1)).astype(F32)
    jrow = lax.broadcasted_iota(jnp.int32, (n_sel, blk), 0)
    tcol = lax.broadcasted_iota(jnp.int32, (n_sel, blk), 1)
    sel_lane = lax.broadcasted_iota(jnp.int32, (cols4, LANES), 1) < SEL_LANES
    qslope = qslope_ref[...]

    def col_max(parts):
        m = None
        for s in parts:
            mp = jnp.max(s, axis=0, keepdims=True)
            m = mp if m is None else jnp.maximum(m, mp)
        return m

    def softmax_pv(parts, m, keys, vt_ref):
        p = [jnp.exp(s - m).astype(BF16) for s in parts]
        p = p[0] if len(p) == 1 else jnp.concatenate(p, axis=0)
        acc = _dot(vt_ref[:, keys], p)
        return acc[0:HEAD_DIM] / acc[HEAD_DIM:LANES]

    def scores(i):
        qs = i * blk
        rows = slice(qs, qs + blk)
        q01 = q_ref[rows, 0:LANES]
        q23 = q_ref[rows, LANES:2 * LANES]
        zero = jnp.zeros_like(q01)
        qst = jnp.concatenate([jnp.where(low, q01, zero), jnp.where(low, zero, q01),
                               jnp.where(low, q23, zero), jnp.where(low, zero, q23)], axis=0)

        dist_c = (qs + tok).astype(F32) - cmp_end
        valid_c = (dist_c >= 0.0) & (cmpr < n_cmp)
        sc = _dot_nt(kc_ref[...], qst) - slope * dist_c
        sc = jnp.where(valid_c, sc, -jnp.inf)
        mc = jnp.max(sc, axis=0, keepdims=True)
        mc = jnp.where(mc == -jnp.inf, 0.0, mc)
        pc = jnp.exp(sc - mc)
        ssum = jnp.sum(pc, axis=0, keepdims=True)
        pc = pc / jnp.where(ssum > 0.0, ssum, 1.0)
        oc = _dot(vct_ref[...], pc.astype(BF16))[0:HEAD_DIM]

        pcs = (pc[:, 0:blk] + pc[:, blk:2 * blk] + pc[:, 2 * blk:3 * blk]
               + pc[:, 3 * blk:4 * blk])
        hi, lo = _split2(pcs)
        imp = _dot(mselt_ref[...], jnp.concatenate([hi, lo], axis=0))
        back = (qs + tcol) // SEL_BLOCK - jrow
        imp = jnp.where(back < 0, -jnp.inf,
                        jnp.where(back < SEL_N_LOCAL, jnp.inf,
                                  jnp.where(jrow == 0, jnp.inf, imp)))
        rank = jnp.zeros((n_sel, blk), F32)
        for c in range(n_sel):
            rowc = imp[c:c + 1, :]
            tie = jnp.where(jrow > c, 1.0, 0.0)
            rank = rank + jnp.where(rowc > imp, 1.0, jnp.where(rowc == imp, tie, 0.0))
        unsel_t = jnp.where(rank >= float(SEL_TOPK), 1.0, 0.0).astype(BF16)
        unsel_t = jnp.concatenate([unsel_t, jnp.zeros((LANES - n_sel, blk), BF16)], axis=0)
        unsel = _dot_nt(eye, unsel_t).astype(BF16)

        qop_w = jnp.concatenate([qst, qslope], axis=1)
        qaug = jnp.where(sel_lane, jnp.concatenate([unsel] * NSA_HEADS, axis=0), qslope)
        qop_s = jnp.concatenate([qst, qaug], axis=1)

        win = []
        w0 = max(qs - WINDOW, 0)
        if i >= nwin:
            win.append(jnp.where(after_t, _dot_nt(kkw_ref[w0:w0 + blk, :], qop_w), -jnp.inf))
        wm = max(qs - WINDOW + blk, 0)
        if qs > wm:
            win.append(_dot_nt(kkw_ref[wm:qs, :], qop_w))
        win.append(jnp.where(causal_t, _dot_nt(kkw_ref[rows, :], qop_w), -jnp.inf))

        sel = []
        if i > 0:
            sel.append(_dot_nt(kks_ref[0:qs, :], qop_s))
        sel.append(jnp.where(causal_t, _dot_nt(kks_ref[rows, :], qop_s), -jnp.inf))
        return oc, win, col_max(win), sel, col_max(sel)

    def outputs(i, st):
        oc, win, m_win, sel, m_sel = st
        qs = i * blk
        rows = slice(qs, qs + blk)
        o_win = softmax_pv(win, m_win, slice(max(qs - WINDOW, 0), qs + blk), vwt_ref)
        o_sel = softmax_pv(sel, m_sel, slice(0, qs + blk), vst_ref)

        sg_t = _sigmoid(misc_ref[rows, :]).T

        def gate(branch):
            return jnp.concatenate(
                [sg_t[MISC_NG + 3 * g + branch:MISC_NG + 3 * g + branch + 1, :]
                 for g in NSA_HEAD_ORDER], axis=1)

        tot = oc * gate(0) + o_sel * gate(1) + o_win * gate(2)
        by_head = {g: tot[:, k * blk:(k + 1) * blk] for k, g in enumerate(NSA_HEAD_ORDER)}
        out = jnp.concatenate([by_head[g] for g in range(NSA_HEADS)], axis=0).T
        z = z_ref[rows, :]
        o_ref[rows, :] = (out * (z * _sigmoid(z))).astype(BF16)

    _emit_pipelined([(i,) for i in range(seq // blk)], scores, outputs, depth=1)


def _nsa_constants(seq, blk):
    n_cmp = (seq - CMP_BLOCK) // CMP_STRIDE + 1
    n_sel = seq // SEL_BLOCK
    cs = np.arange(n_cmp) * CMP_STRIDE
    ce = cs + CMP_BLOCK - 1
    ss = np.arange(n_sel) * SEL_BLOCK
    se = ss + SEL_BLOCK - 1
    msel_t = np.zeros((n_sel, LANES), np.float32)
    msel_t[:, :n_cmp] = ((cs[:, None] <= se[None, :]) & (ce[:, None] >= ss[None, :])).T
    msel_t = np.concatenate([msel_t, msel_t], axis=1)
    key = np.arange(seq)
    kaug_w = np.zeros((seq, LANES), np.float32)
    kaug_w[:, POS_HI_LANE] = key // 16
    kaug_w[:, POS_LO_LANE] = key % 16
    kaug_s = kaug_w.copy()
    kaug_s[key, key // SEL_BLOCK] = PEN
    qslope = np.zeros((NSA_HEADS * blk, LANES), np.float32)
    for k, g in enumerate(NSA_HEAD_ORDER):
        qslope[k * blk:(k + 1) * blk, POS_HI_LANE] = 16.0 * NSA_SLOPES[g]
        qslope[k * blk:(k + 1) * blk, POS_LO_LANE] = NSA_SLOPES[g]
    return (jnp.asarray(msel_t, BF16), jnp.asarray(kaug_s, BF16), jnp.asarray(kaug_w, BF16),
            jnp.asarray(qslope, BF16))


def _nsa(pb3, pf3, pos_k, w1_k, w2_k, pos_v, w1_v, w2_v, blk=128):
    bsz, seq, _ = pb3.shape
    nchunk = seq // CMP_STRIDE
    w1 = jnp.zeros((CMP_BLOCK, LANES, LANES), F32)
    w1 = w1.at[:, :HEAD_DIM, :HEAD_DIM].set(w1_k.reshape(CMP_BLOCK, HEAD_DIM, HEAD_DIM))
    w1 = w1.at[:, HEAD_DIM:, HEAD_DIM:].set(w1_v.reshape(CMP_BLOCK, HEAD_DIM, HEAD_DIM))
    pos = jnp.concatenate([pos_k, pos_v], axis=1)
    zero = jnp.zeros((LANES, LANES), F32)
    w2k = zero.at[:HEAD_DIM, :].set(jnp.concatenate([w2_k, w2_k], axis=1))
    w2v = zero.at[HEAD_DIM:, :].set(jnp.concatenate([w2_v, w2_v], axis=1))
    consts = _nsa_constants(seq, blk)

    def col(base, nblk=1):
        return pl.BlockSpec((None, seq, nblk * LANES), lambda b: (b, 0, base // nblk))

    def whole(shape):
        return pl.BlockSpec(shape, lambda b: (0,) * len(shape))

    return pl.pallas_call(
        functools.partial(_nsa_kernel, seq=seq, blk=blk),
        grid=(bsz,),
        in_specs=[
            col(PB_NQ, 2), col(PB_NSEL), col(PB_NWIN),
            col(PF_NZ, 2), col(PF_MISC), col(PF_KVC),
            whole(pos.shape), whole(w1.shape), whole(w2k.shape), whole(w2v.shape),
        ] + [whole(c.shape) for c in consts],
        out_specs=pl.BlockSpec((None, seq, NSA_W), lambda b: (b, 0, 0)),
        out_shape=jax.ShapeDtypeStruct((bsz, seq, NSA_W), BF16),
        scratch_shapes=[
            pltpu.VMEM((nchunk, LANES), BF16),
            pltpu.VMEM((LANES, nchunk), BF16),
            pltpu.VMEM((seq, 2 * LANES), BF16),
            pltpu.VMEM((seq, 2 * LANES), BF16),
            pltpu.VMEM((LANES, seq), BF16),
            pltpu.VMEM((LANES, seq), BF16),
        ],
        compiler_params=pltpu.CompilerParams(
            dimension_semantics=("arbitrary",), vmem_limit_bytes=VMEM_LIMIT),
        name="nsa_attn",
    )(pb3, pb3, pb3, pf3, pf3, pf3, pos, w1.astype(BF16), w2k.astype(BF16), w2v.astype(BF16),
      *consts)


def kernel(x, norm_g, w_in, b_f, cmp_pos_k, cmp_w1_k, cmp_w2_k,
           cmp_pos_v, cmp_w1_v, cmp_w2_v, w_out, final_g):
    bsz, seq, d = x.shape
    xf = x.reshape(bsz * seq, d)
    for l in range(DEPTH):
        pb, pf = _inproj(xf, norm_g[l], _pack_w_in(w_in[l]))
        pb3 = pb.reshape(bsz, seq, NB_COLS)
        pf3 = pf.reshape(bsz, seq, NF_COLS)
        o_fox = _fox(pb3, pf3, b_f[l])
        o_sb = _sb(pb3, pf3)
        o_nsa = _nsa(pb3, pf3, cmp_pos_k[l], cmp_w1_k[l], cmp_w2_k[l],
                     cmp_pos_v[l], cmp_w1_v[l], cmp_w2_v[l])
        xf = _outproj(o_fox.reshape(bsz * seq, FOX_W), o_sb.reshape(bsz * seq, SB_W),
                      o_nsa.reshape(bsz * seq, NSA_W), xf, w_out[l].astype(BF16),
                      final_g, final=(l == DEPTH - 1))
    return xf.reshape(bsz, seq, d)
```

```python
import functools

import numpy as np
import jax
import jax.numpy as jnp
from jax import lax
from jax.experimental import pallas as pl
from jax.experimental.pallas import tpu as pltpu

F32 = jnp.float32
BF16 = jnp.bfloat16

D_MODEL = 1024
DEPTH = 2
HEAD_DIM = 64
LANES = 128
FOX_HEADS = 6
SB_HEADS = 6
NSA_HEADS = 4
FOX_W = FOX_HEADS * HEAD_DIM
SB_W = SB_HEADS * HEAD_DIM
NSA_W = NSA_HEADS * HEAD_DIM
CMP_BLOCK = 32
CMP_STRIDE = 16
SEL_BLOCK = 64
SEL_TOPK = 8
SEL_N_LOCAL = 2
WINDOW = 512
NORM_EPS = 1e-6
QK_SCALE = HEAD_DIM ** -0.5
PEN = -(2.0 ** 100)

PB_FQ, PB_FK, PB_FV, PB_SQ, PB_SK, PB_SV, PB_NQ, PB_NSEL, PB_NWIN = (
    0, 3, 6, 9, 12, 15, 18, 20, 21)
PB_BLOCKS = 22
PF_FZ, PF_SZ, PF_NZ, PF_KVC, PF_MISC = 0, 3, 6, 8, 9
PF_BLOCKS = 10
NB_COLS = PB_BLOCKS * LANES
NF_COLS = PF_BLOCKS * LANES
MISC_PAIR_LANES = 16
MISC_FF_COPIES = 6
MISC_NG = MISC_PAIR_LANES * (FOX_HEADS // 2)

VMEM_PHYSICAL = 64 * 1024 * 1024
VMEM_LIMIT = VMEM_PHYSICAL - 4 * 1024 * 1024

_NT = (((1,), (1,)), ((), ()))


def _dot(a, b):
    return jnp.dot(a, b, preferred_element_type=F32)


def _dot_nt(a, b):
    return lax.dot_general(a, b, _NT, preferred_element_type=F32)


def _sigmoid(x):
    return 1.0 / (1.0 + jnp.exp(-x))


def _log_sigmoid(x):
    return -(jnp.maximum(-x, 0.0) + jnp.log1p(jnp.exp(-jnp.abs(x))))


def _split3(x):
    hi = x.astype(BF16)
    r = x - hi.astype(F32)
    mid = r.astype(BF16)
    lo = (r - mid.astype(F32)).astype(BF16)
    return hi, mid, lo


def _split2(x):
    hi = x.astype(BF16)
    lo = (x - hi.astype(F32)).astype(BF16)
    return hi, lo


PIPELINE_DEPTH = 4


def _emit_pipelined(items, first, second, depth=PIPELINE_DEPTH):
    pending = {}
    for n, item in enumerate(items):
        pending[item] = first(*item)
        if n >= depth:
            prev = items[n - depth]
            second(*prev, pending.pop(prev))
    for prev in items[max(len(items) - depth, 0):]:
        second(*prev, pending.pop(prev))


def _inproj_kernel(x_ref, g_ref, w_ref, pb_ref, pf_ref):
    x = x_ref[...]
    ms = jnp.mean(x * x, axis=-1, keepdims=True)
    h = (x * lax.rsqrt(ms + NORM_EPS) * g_ref[...]).astype(BF16)
    chunk = 4 * LANES
    for c in range(0, NB_COLS, chunk):
        e = min(c + chunk, NB_COLS)
        pb_ref[:, c:e] = _dot(h, w_ref[:, c:e]).astype(BF16)
    for c in range(0, NF_COLS, chunk):
        e = min(c + chunk, NF_COLS)
        pf_ref[:, c:e] = _dot(h, w_ref[:, NB_COLS + c:NB_COLS + e])


def _inproj(xf, g, w_all, tm=512):
    m, d = xf.shape
    return pl.pallas_call(
        _inproj_kernel,
        grid=(m // tm,),
        in_specs=[
            pl.BlockSpec((tm, d), lambda i: (i, 0)),
            pl.BlockSpec((1, d), lambda i: (0, 0)),
            pl.BlockSpec((d, NB_COLS + NF_COLS), lambda i: (0, 0)),
        ],
        out_specs=[
            pl.BlockSpec((tm, NB_COLS), lambda i: (i, 0)),
            pl.BlockSpec((tm, NF_COLS), lambda i: (i, 0)),
        ],
        out_shape=[
            jax.ShapeDtypeStruct((m, NB_COLS), BF16),
            jax.ShapeDtypeStruct((m, NF_COLS), F32),
        ],
        compiler_params=pltpu.CompilerParams(
            dimension_semantics=("arbitrary",), vmem_limit_bytes=VMEM_LIMIT),
        name="inproj",
    )(xf, g.reshape(1, d), w_all)


def _pack_w_in(w):
    sizes = (FOX_W, FOX_W, FOX_W, FOX_HEADS, FOX_W, SB_W, SB_W, SB_W, SB_W,
             NSA_W, HEAD_DIM, HEAD_DIM, HEAD_DIM, HEAD_DIM, HEAD_DIM, HEAD_DIM,
             3 * NSA_HEADS, NSA_W)
    offs = np.concatenate([[0], np.cumsum(sizes)])
    w = w.astype(BF16)
    (fq, fk, fv, ff, fz, sq, sk, sv, sz, nq, nkc, nvc, nks, nvs, nkw, nvw, ng, nz) = [
        w[:, offs[i]:offs[i + 1]] for i in range(len(sizes))]
    zeros = jnp.zeros((w.shape[0], LANES), w.dtype)
    cols = [fq * QK_SCALE, fk, fv, sq * QK_SCALE, sk, sv, nq * QK_SCALE,
            nks, nvs, nkw, nvw,
            fz, sz, nz, nkc, nvc, _misc_ff_layout(ff, zeros), ng,
            zeros[:, :LANES - MISC_NG - 3 * NSA_HEADS]]
    return jnp.concatenate(cols, axis=1)


def _misc_ff_layout(ff, zeros):
    cols = []
    for p in range(FOX_HEADS // 2):
        cols += [ff[:, 2 * p:2 * p + 1]] * MISC_FF_COPIES + [ff[:, 2 * p + 1:2 * p + 2]] * MISC_FF_COPIES
        cols.append(zeros[:, :MISC_PAIR_LANES - 2 * MISC_FF_COPIES])
    return jnp.concatenate(cols, axis=1)


def _outproj_kernel(of_ref, os_ref, on_ref, x_ref, w_ref, g_ref, o_ref, *, final):
    y = (x_ref[...]
         + _dot(of_ref[...], w_ref[0:FOX_W, :])
         + _dot(os_ref[...], w_ref[FOX_W:FOX_W + SB_W, :])
         + _dot(on_ref[...], w_ref[FOX_W + SB_W:, :]))
    if final:
        ms = jnp.mean(y * y, axis=-1, keepdims=True)
        y = y * lax.rsqrt(ms + NORM_EPS) * g_ref[...]
    o_ref[...] = y


def _outproj(o_fox, o_sb, o_nsa, xf, w, g, final, tm=1024):
    m, d = xf.shape
    return pl.pallas_call(
        functools.partial(_outproj_kernel, final=final),
        grid=(m // tm,),
        in_specs=[
            pl.BlockSpec((tm, FOX_W), lambda i: (i, 0)),
            pl.BlockSpec((tm, SB_W), lambda i: (i, 0)),
            pl.BlockSpec((tm, NSA_W), lambda i: (i, 0)),
            pl.BlockSpec((tm, d), lambda i: (i, 0)),
            pl.BlockSpec((d, d), lambda i: (0, 0)),
            pl.BlockSpec((1, d), lambda i: (0, 0)),
        ],
        out_specs=pl.BlockSpec((tm, d), lambda i: (i, 0)),
        out_shape=jax.ShapeDtypeStruct((m, d), F32),
        compiler_params=pltpu.CompilerParams(
            dimension_semantics=("arbitrary",), vmem_limit_bytes=VMEM_LIMIT),
        name="outproj_final" if final else "outproj",
    )(o_fox, o_sb, o_nsa, xf, w, g.reshape(1, d))


def _fox_kernel(q_ref, k_ref, v_ref, z_ref, misc_ref, bf_ref, tri_ref,
                o_ref, qq_ref, kk_ref, vpt_ref, cps_ref, *, seq, blk):
    pair = pl.program_id(1)
    low = lax.broadcasted_iota(jnp.int32, (blk, LANES), 1) < HEAD_DIM
    lane_b = lax.broadcasted_iota(jnp.int32, (blk, LANES), 1)
    nblk = seq // blk

    @pl.when(pair == 0)
    def _():
        tri = tri_ref[...]
        part = (lane_b % MISC_PAIR_LANES) % 3
        carry = jnp.zeros((1, LANES), F32)
        for b in range(nblk):
            rows = slice(b * blk, (b + 1) * blk)
            ls = _log_sigmoid(misc_ref[rows, :] + bf_ref[...])
            c3 = _dot(tri, jnp.concatenate(_split3(ls), axis=1))
            cb = c3[:, 0:LANES] + c3[:, LANES:2 * LANES] + c3[:, 2 * LANES:3 * LANES] + carry
            carry = cb[blk - 1:blk, :]
            hi, mid, lo = _split3(cb)
            cps_ref[rows, :] = jnp.where(part == 0, hi.astype(F32), jnp.where(
                part == 1, mid.astype(F32), lo.astype(F32))).astype(BF16)

    rel = lane_b - MISC_PAIR_LANES * pair
    third = jnp.where(rel < 0, -1, jnp.where(rel < 3, 0, jnp.where(rel < 6, 1, jnp.where(
        rel < 9, 2, jnp.where(rel < 12, 3, -1)))))

    def lanes_of(*thirds):
        sel = jnp.zeros((blk, LANES), F32)
        for t in thirds:
            sel = jnp.where(third == t, 1.0, sel)
        return sel.astype(BF16)

    k_ones, k_parts = lanes_of(0, 2), lanes_of(1, 3)
    qa_parts, qa_ones = lanes_of(0), lanes_of(1)
    qb_parts, qb_ones = lanes_of(2), lanes_of(3)
    eye = jnp.where(lax.broadcasted_iota(jnp.int32, (LANES, LANES), 0)
                    == lax.broadcasted_iota(jnp.int32, (LANES, LANES), 1), 1.0, 0.0).astype(BF16)
    for b in range(nblk):
        rows = slice(b * blk, (b + 1) * blk)
        cps = cps_ref[rows, :]
        q2 = q_ref[rows, :]
        v2 = v_ref[rows, :]
        zero = jnp.zeros_like(q2)
        one = jnp.ones_like(q2)
        kk_ref[rows, 0:LANES] = k_ref[rows, :]
        kk_ref[rows, LANES:2 * LANES] = k_ones - cps * k_parts
        qq_ref[0, rows, 0:LANES] = jnp.where(low, q2, zero)
        qq_ref[0, rows, LANES:2 * LANES] = cps * qa_parts + qa_ones
        qq_ref[1, rows, 0:LANES] = jnp.where(low, zero, q2)
        qq_ref[1, rows, LANES:2 * LANES] = cps * qb_parts + qb_ones
        vpt_ref[0, :, rows] = _dot_nt(eye, jnp.where(low, v2, one)).astype(BF16)
        vpt_ref[1, :, rows] = _dot_nt(eye, jnp.where(low, one, v2)).astype(BF16)

    causal_t = (lax.broadcasted_iota(jnp.int32, (blk, blk), 0)
                <= lax.broadcasted_iota(jnp.int32, (blk, blk), 1))

    def scores(i, h):
        r0 = i * blk
        rows = slice(r0, r0 + blk)
        qa = qq_ref[h, rows, :]
        sd = jnp.where(causal_t, _dot_nt(kk_ref[rows, :], qa), -jnp.inf)
        m = jnp.max(sd, axis=0, keepdims=True)
        sm = None
        if i > 0:
            sm = _dot_nt(kk_ref[0:r0, :], qa)
            m = jnp.maximum(m, jnp.max(sm, axis=0, keepdims=True))
        return sd, sm, m

    outs = {}

    def weighted_sum(i, h, st):
        sd, sm, m = st
        rows = slice(i * blk, (i + 1) * blk)
        p = jnp.exp(sd - m).astype(BF16)
        if i > 0:
            p = jnp.concatenate([jnp.exp(sm - m).astype(BF16), p], axis=0)
        a = _dot(vpt_ref[h, :, 0:(i + 1) * blk], p)
        if h == 0:
            outs[i, h] = a[0:HEAD_DIM] / a[HEAD_DIM:LANES]
        else:
            outs[i, h] = a[HEAD_DIM:LANES] / a[0:HEAD_DIM]
            o = jnp.concatenate([outs.pop((i, 0)), outs.pop((i, 1))], axis=0).T
            z = z_ref[rows, :]
            o_ref[rows, :] = (o * (z * _sigmoid(z))).astype(BF16)

    _emit_pipelined([(i, h) for i in range(nblk) for h in range(2)], scores, weighted_sum)


def _fox(pb3, pf3, b_f, blk=256):
    bsz, seq, _ = pb3.shape
    npair = FOX_HEADS // 2
    bias = jnp.zeros((1, LANES), F32).at[:, :MISC_NG].set(
        _misc_ff_layout(b_f.reshape(1, FOX_HEADS), jnp.zeros((1, LANES), F32)))
    tri = jnp.asarray(np.tril(np.ones((blk, blk), np.float32)), BF16)

    def col(base):
        return pl.BlockSpec((None, seq, LANES), lambda b, p: (b, 0, base + p))

    def whole(shape):
        return pl.BlockSpec(shape, lambda b, p: (0,) * len(shape))

    return pl.pallas_call(
        functools.partial(_fox_kernel, seq=seq, blk=blk),
        grid=(bsz, npair),
        in_specs=[
            col(PB_FQ), col(PB_FK), col(PB_FV), col(PF_FZ),
            pl.BlockSpec((None, seq, LANES), lambda b, p: (b, 0, PF_MISC)),
            whole((1, LANES)), whole((blk, blk)),
        ],
        out_specs=pl.BlockSpec((None, seq, LANES), lambda b, p: (b, 0, p)),
        out_shape=jax.ShapeDtypeStruct((bsz, seq, FOX_W), BF16),
        scratch_shapes=[
            pltpu.VMEM((2, seq, 2 * LANES), BF16),
            pltpu.VMEM((seq, 2 * LANES), BF16),
            pltpu.VMEM((2, LANES, seq), BF16),
            pltpu.VMEM((seq, LANES), BF16),
        ],
        compiler_params=pltpu.CompilerParams(
            dimension_semantics=("arbitrary", "arbitrary"), vmem_limit_bytes=VMEM_LIMIT),
        name="fox_attn",
    )(pb3, pb3, pb3, pf3, pf3, bias, tri)


SB_NEAR_TILES = 2
SB_DEAD = -104.0


def _sb_kernel(q_ref, k_ref, v_ref, z_ref, suf_ref, o_ref, acc_ref, carry_ref, *, seq, blk):
    low = lax.broadcasted_iota(jnp.int32, (blk, LANES), 1) < HEAD_DIM
    strict = (lax.broadcasted_iota(jnp.int32, (blk, blk), 1)
              < lax.broadcasted_iota(jnp.int32, (blk, blk), 0))
    suf = suf_ref[...]

    def log_terms(qh, j_hi, j_lo):
        z = _dot_nt(qh, k_ref[j_lo * blk:(j_hi + 1) * blk, :])
        lsz = jnp.minimum(z, 0.0) - jnp.log(1.0 + jnp.exp(-jnp.abs(z)))
        return lsz, lsz - z

    def tiles(i, j_hi, j_lo, carry, terms):
        lsz, l1m = terms
        k0 = j_lo * blk
        out = None
        for j in range(j_hi, j_lo - 1, -1):
            cs = slice(j * blk - k0, (j + 1) * blk - k0)
            l1 = l1m[:, cs]
            if j == i:
                l1 = jnp.where(strict, l1, 0.0)
            r = _dot(l1.astype(BF16), suf)
            a = jnp.exp(lsz[:, cs] + (r + carry))
            if j == i:
                a = jnp.where(strict, a, 0.0)
            pv = _dot(a.astype(BF16), v_ref[j * blk:(j + 1) * blk, :])
            out = pv if out is None else out + pv
            carry = carry + (r[:, 0:1] + l1[:, 0:1])
        return out, carry

    nblk = seq // blk

    def q_head(i, h):
        q2 = q_ref[i * blk:(i + 1) * blk, :]
        zero = jnp.zeros_like(q2)
        return jnp.where(low, q2, zero) if h == 0 else jnp.where(low, zero, q2)

    def j_near(i):
        return max(i - SB_NEAR_TILES + 1, 0)

    alive = {}

    def near_terms(i, h):
        return log_terms(q_head(i, h), i, j_near(i))

    def near_tiles(i, h, terms):
        out, carry = tiles(i, i, j_near(i), jnp.zeros((blk, 1), F32), terms)
        acc_ref[i, h] = out
        if j_near(i) > 0:
            carry_ref[i, h] = jnp.broadcast_to(carry, (blk, LANES))
            alive[i, h] = jnp.max(carry) >= SB_DEAD

    _emit_pipelined([(i, h) for i in range(nblk) for h in range(2)], near_terms, near_tiles)

    for (i, h), flag in alive.items():
        @pl.when(flag)
        def _(i=i, h=h):
            far, _ = tiles(i, j_near(i) - 1, 0, carry_ref[i, h][:, 0:1],
                           log_terms(q_head(i, h), j_near(i) - 1, 0))
            acc_ref[i, h] = acc_ref[i, h] + far

    for i in range(nblk):
        rows = slice(i * blk, (i + 1) * blk)
        o = jnp.where(low, acc_ref[i, 0], acc_ref[i, 1])
        zg = z_ref[rows, :]
        o_ref[rows, :] = (o * (zg * _sigmoid(zg))).astype(BF16)


def _sb(pb3, pf3, blk=256):
    bsz, seq, _ = pb3.shape
    npair = SB_HEADS // 2
    suf = jnp.asarray(np.tril(np.ones((blk, blk), np.float32), -1), BF16)

    def col(base):
        return pl.BlockSpec((None, seq, LANES), lambda b, p: (b, 0, base + p))

    return pl.pallas_call(
        functools.partial(_sb_kernel, seq=seq, blk=blk),
        grid=(bsz, npair),
        in_specs=[
            col(PB_SQ), col(PB_SK), col(PB_SV), col(PF_SZ),
            pl.BlockSpec(suf.shape, lambda b, p: (0, 0)),
        ],
        out_specs=pl.BlockSpec((None, seq, LANES), lambda b, p: (b, 0, p)),
        out_shape=jax.ShapeDtypeStruct((bsz, seq, SB_W), BF16),
        scratch_shapes=[pltpu.VMEM((seq // blk, 2, blk, LANES), F32),
                        pltpu.VMEM((seq // blk, 2, blk, LANES), F32)],
        compiler_params=pltpu.CompilerParams(
            dimension_semantics=("arbitrary", "arbitrary"), vmem_limit_bytes=VMEM_LIMIT),
        name="sb_attn",
    )(pb3, pb3, pb3, pf3, suf)


NSA_HEAD_ORDER = (0, 1, 2, 3)
SEL_LANES = 32
POS_HI_LANE = 32
POS_LO_LANE = 33
NSA_SLOPES = tuple(2.0 ** (-8.0 * (g + 1) / NSA_HEADS) for g in range(NSA_HEADS))


def _nsa_kernel(q_ref, ksv_ref, kwv_ref, z_ref, misc_ref, kvc_ref,
                pos_ref, w1_ref, w2k_ref, w2v_ref,
                mselt_ref, kaugs_ref, kaugw_ref, qslope_ref,
                o_ref, kc_ref, vct_ref, kks_ref, kkw_ref, vst_ref, vwt_ref, *, seq, blk):
    cols4 = NSA_HEADS * blk
    n_cmp = (seq - CMP_BLOCK) // CMP_STRIDE + 1
    n_sel = seq // SEL_BLOCK
    nwin = WINDOW // blk
    eye = jnp.where(lax.broadcasted_iota(jnp.int32, (LANES, LANES), 0)
                    == lax.broadcasted_iota(jnp.int32, (LANES, LANES), 1), 1.0, 0.0).astype(BF16)

    nchunk = seq // CMP_STRIDE
    ha = jnp.zeros((nchunk, LANES), F32)
    hb = jnp.zeros((nchunk, LANES), F32)
    for p in range(CMP_STRIDE):
        xp = kvc_ref[pl.ds(p, nchunk, stride=CMP_STRIDE), :]
        q = p + CMP_STRIDE
        ha = ha + _dot((xp + pos_ref[p:p + 1, :]).astype(BF16), w1_ref[p])
        hb = hb + _dot((xp + pos_ref[q:q + 1, :]).astype(BF16), w1_ref[q])
    hid = ha + pltpu.roll(hb, nchunk - 1, axis=0)
    hid = (hid * _sigmoid(hid)).astype(BF16)
    kc_ref[...] = _dot(hid, w2k_ref[...]).astype(BF16)
    vct_ref[...] = _dot_nt(eye, _dot(hid, w2v_ref[...]).astype(BF16)).astype(BF16)

    tb = 2 * LANES
    lane_t = lax.broadcasted_iota(jnp.int32, (tb, LANES), 1)
    low_t = lane_t < HEAD_DIM
    er = lax.broadcasted_iota(jnp.int32, (LANES, LANES), 0)
    ec = lax.broadcasted_iota(jnp.int32, (LANES, LANES), 1)
    dup_lo = jnp.where(er == ec % HEAD_DIM, 1.0, 0.0).astype(BF16)
    swap = jnp.where(ec == (er + HEAD_DIM) % LANES, 1.0, 0.0).astype(BF16)
    kks_ref[:, LANES:2 * LANES] = kaugs_ref[...]
    kkw_ref[:, LANES:2 * LANES] = kaugw_ref[...]
    for b in range(seq // tb):
        rows = slice(b * tb, (b + 1) * tb)
        xs = ksv_ref[rows, :]
        xw = kwv_ref[rows, :]
        one = jnp.ones_like(xs)
        kks_ref[rows, 0:LANES] = _dot(xs, dup_lo).astype(BF16)
        kkw_ref[rows, 0:LANES] = _dot(xw, dup_lo).astype(BF16)
        vst_ref[:, rows] = _dot_nt(swap, jnp.where(low_t, one, xs)).astype(BF16)
        vwt_ref[:, rows] = _dot_nt(swap, jnp.where(low_t, one, xw)).astype(BF16)

    low = lax.broadcasted_iota(jnp.int32, (blk, LANES), 1) < HEAD_DIM
    colt = lax.broadcasted_iota(jnp.int32, (1, cols4), 1)
    grp = colt // blk
    tok = colt - grp * blk
    gslope = [NSA_SLOPES[g] for g in NSA_HEAD_ORDER]
    slope = jnp.where(grp == 0, gslope[0], jnp.where(grp == 1, gslope[1],
                      jnp.where(grp == 2, gslope[2], gslope[3]))).astype(F32)
    keyr = lax.broadcasted_iota(jnp.int32, (blk, 1), 0)
    causal_t = keyr <= tok
    after_t = keyr > tok
    cmpr = lax.broadcasted_iota(jnp.int32, (LANES, 1), 0)
    cmp_end = (cmpr * CMP_STRIDE + (CMP_BLOCK - 1)).astype(F32)
    jrow = lax.broadcasted_iota(jnp.int32, (n_sel, blk), 0)
    tcol = lax.broadcasted_iota(jnp.int32, (n_sel, blk), 1)
    sel_lane = lax.broadcasted_iota(jnp.int32, (cols4, LANES), 1) < SEL_LANES
    qslope = qslope_ref[...]

    def col_max(parts):
        m = None
        for s in parts:
            mp = jnp.max(s, axis=0, keepdims=True)
            m = mp if m is None else jnp.maximum(m, mp)
        return m

    def softmax_pv(parts, m, keys, vt_ref):
        p = [jnp.exp(s - m).astype(BF16) for s in parts]
        p = p[0] if len(p) == 1 else jnp.concatenate(p, axis=0)
        acc = _dot(vt_ref[:, keys], p)
        return acc[0:HEAD_DIM] / acc[HEAD_DIM:LANES]

    def scores(i):
        qs = i * blk
        rows = slice(qs, qs + blk)
        q01 = q_ref[rows, 0:LANES]
        q23 = q_ref[rows, LANES:2 * LANES]
        zero = jnp.zeros_like(q01)
        qst = jnp.concatenate([jnp.where(low, q01, zero), jnp.where(low, zero, q01),
                               jnp.where(low, q23, zero), jnp.where(low, zero, q23)], axis=0)

        dist_c = (qs + tok).astype(F32) - cmp_end
        valid_c = (dist_c >= 0.0) & (cmpr < n_cmp)
        sc = _dot_nt(kc_ref[...], qst) - slope * dist_c
        sc = jnp.where(valid_c, sc, -jnp.inf)
        mc = jnp.max(sc, axis=0, keepdims=True)
        mc = jnp.where(mc == -jnp.inf, 0.0, mc)
        pc = jnp.exp(sc - mc)
        ssum = jnp.sum(pc, axis=0, keepdims=True)
        pc = pc / jnp.where(ssum > 0.0, ssum, 1.0)
        oc = _dot(vct_ref[...], pc.astype(BF16))[0:HEAD_DIM]

        pcs = (pc[:, 0:blk] + pc[:, blk:2 * blk] + pc[:, 2 * blk:3 * blk]
               + pc[:, 3 * blk:4 * blk])
        hi, lo = _split2(pcs)
        imp = _dot(mselt_ref[...], jnp.concatenate([hi, lo], axis=0))
        back = (qs + tcol) // SEL_BLOCK - jrow
        imp = jnp.where(back < 0, -jnp.inf,
                        jnp.where(back < SEL_N_LOCAL, jnp.inf,
                                  jnp.where(jrow == 0, jnp.inf, imp)))
        rank = jnp.zeros((n_sel, blk), F32)
        for c in range(n_sel):
            rowc = imp[c:c + 1, :]
            tie = jnp.where(jrow > c, 1.0, 0.0)
            rank = rank + jnp.where(rowc > imp, 1.0, jnp.where(rowc == imp, tie, 0.0))
        unsel_t = jnp.where(rank >= float(SEL_TOPK), 1.0, 0.0).astype(BF16)
        unsel_t = jnp.concatenate([unsel_t, jnp.zeros((LANES - n_sel, blk), BF16)], axis=0)
        unsel = _dot_nt(eye, unsel_t).astype(BF16)

        qop_w = jnp.concatenate([qst, qslope], axis=1)
        qaug = jnp.where(sel_lane, jnp.concatenate([unsel] * NSA_HEADS, axis=0), qslope)
        qop_s = jnp.concatenate([qst, qaug], axis=1)

        win = []
        w0 = max(qs - WINDOW, 0)
        if i >= nwin:
            win.append(jnp.where(after_t, _dot_nt(kkw_ref[w0:w0 + blk, :], qop_w), -jnp.inf))
        wm = max(qs - WINDOW + blk, 0)
        if qs > wm:
            win.append(_dot_nt(kkw_ref[wm:qs, :], qop_w))
        win.append(jnp.where(causal_t, _dot_nt(kkw_ref[rows, :], qop_w), -jnp.inf))

        sel = []
        if i > 0:
            sel.append(_dot_nt(kks_ref[0:qs, :], qop_s))
        sel.append(jnp.where(causal_t, _dot_nt(kks_ref[rows, :], qop_s), -jnp.inf))
        return oc, win, col_max(win), sel, col_max(sel)

    def outputs(i, st):
        oc, win, m_win, sel, m_sel = st
        qs = i * blk
        rows = slice(qs, qs + blk)
        o_win = softmax_pv(win, m_win, slice(max(qs - WINDOW, 0), qs + blk), vwt_ref)
        o_sel = softmax_pv(sel, m_sel, slice(0, qs + blk), vst_ref)

        sg_t = _sigmoid(misc_ref[rows, :]).T

        def gate(branch):
            return jnp.concatenate(
                [sg_t[MISC_NG + 3 * g + branch:MISC_NG + 3 * g + branch + 1, :]
                 for g in NSA_HEAD_ORDER], axis=1)

        tot = oc * gate(0) + o_sel * gate(1) + o_win * gate(2)
        by_head = {g: tot[:, k * blk:(k + 1) * blk] for k, g in enumerate(NSA_HEAD_ORDER)}
        out = jnp.concatenate([by_head[g] for g in range(NSA_HEADS)], axis=0).T
        z = z_ref[rows, :]
        o_ref[rows, :] = (out * (z * _sigmoid(z))).astype(BF16)

    _emit_pipelined([(i,) for i in range(seq // blk)], scores, outputs, depth=1)


def _nsa_constants(seq, blk):
    n_cmp = (seq - CMP_BLOCK) // CMP_STRIDE + 1
    n_sel = seq // SEL_BLOCK
    cs = np.arange(n_cmp) * CMP_STRIDE
    ce = cs + CMP_BLOCK - 1
    ss = np.arange(n_sel) * SEL_BLOCK
    se = ss + SEL_BLOCK - 1
    msel_t = np.zeros((n_sel, LANES), np.float32)
    msel_t[:, :n_cmp] = ((cs[:, None] <= se[None, :]) & (ce[:, None] >= ss[None, :])).T
    msel_t = np.concatenate([msel_t, msel_t], axis=1)
    key = np.arange(seq)
    kaug_w = np.zeros((seq, LANES), np.float32)
    kaug_w[:, POS_HI_LANE] = key // 16
    kaug_w[:, POS_LO_LANE] = key % 16
    kaug_s = kaug_w.copy()
    kaug_s[key, key // SEL_BLOCK] = PEN
    qslope = np.zeros((NSA_HEADS * blk, LANES), np.float32)
    for k, g in enumerate(NSA_HEAD_ORDER):
        qslope[k * blk:(k + 1) * blk, POS_HI_LANE] = 16.0 * NSA_SLOPES[g]
        qslope[k * blk:(k + 1) * blk, POS_LO_LANE] = NSA_SLOPES[g]
    return (jnp.asarray(msel_t, BF16), jnp.asarray(kaug_s, BF16), jnp.asarray(kaug_w, BF16),
            jnp.asarray(qslope, BF16))


def _nsa(pb3, pf3, pos_k, w1_k, w2_k, pos_v, w1_v, w2_v, blk=128):
    bsz, seq, _ = pb3.shape
    nchunk = seq // CMP_STRIDE
    w1k = w1_k.astype(BF16).reshape(CMP_BLOCK, HEAD_DIM, HEAD_DIM)
    w1v = w1_v.astype(BF16).reshape(CMP_BLOCK, HEAD_DIM, HEAD_DIM)
    w1z = jnp.zeros_like(w1k)
    w1 = jnp.concatenate([jnp.concatenate([w1k, w1z], axis=2),
                          jnp.concatenate([w1z, w1v], axis=2)], axis=1)
    pos = jnp.concatenate([pos_k, pos_v], axis=1)
    zero = jnp.zeros((LANES, LANES), F32)
    w2k = zero.at[:HEAD_DIM, :].set(jnp.concatenate([w2_k, w2_k], axis=1))
    w2v = zero.at[HEAD_DIM:, :].set(jnp.concatenate([w2_v, w2_v], axis=1))
    consts = _nsa_constants(seq, blk)

    def col(base, nblk=1):
        return pl.BlockSpec((None, seq, nblk * LANES), lambda b: (b, 0, base // nblk))

    def whole(shape):
        return pl.BlockSpec(shape, lambda b: (0,) * len(shape))

    return pl.pallas_call(
        functools.partial(_nsa_kernel, seq=seq, blk=blk),
        grid=(bsz,),
        in_specs=[
            col(PB_NQ, 2), col(PB_NSEL), col(PB_NWIN),
            col(PF_NZ, 2), col(PF_MISC), col(PF_KVC),
            whole(pos.shape), whole(w1.shape), whole(w2k.shape), whole(w2v.shape),
        ] + [whole(c.shape) for c in consts],
        out_specs=pl.BlockSpec((None, seq, NSA_W), lambda b: (b, 0, 0)),
        out_shape=jax.ShapeDtypeStruct((bsz, seq, NSA_W), BF16),
        scratch_shapes=[
            pltpu.VMEM((nchunk, LANES), BF16),
            pltpu.VMEM((LANES, nchunk), BF16),
            pltpu.VMEM((seq, 2 * LANES), BF16),
            pltpu.VMEM((seq, 2 * LANES), BF16),
            pltpu.VMEM((LANES, seq), BF16),
            pltpu.VMEM((LANES, seq), BF16),
        ],
        compiler_params=pltpu.CompilerParams(
            dimension_semantics=("arbitrary",), vmem_limit_bytes=VMEM_LIMIT),
        name="nsa_attn",
    )(pb3, pb3, pb3, pf3, pf3, pf3, pos, w1, w2k.astype(BF16), w2v.astype(BF16), *consts)


def kernel(x, norm_g, w_in, b_f, cmp_pos_k, cmp_w1_k, cmp_w2_k,
           cmp_pos_v, cmp_w1_v, cmp_w2_v, w_out, final_g):
    bsz, seq, d = x.shape
    xf = x.reshape(bsz * seq, d)
    for l in range(DEPTH):
        pb, pf = _inproj(xf, norm_g[l], _pack_w_in(w_in[l]))
        pb3 = pb.reshape(bsz, seq, NB_COLS)
        pf3 = pf.reshape(bsz, seq, NF_COLS)
        o_fox = _fox(pb3, pf3, b_f[l])
        o_sb = _sb(pb3, pf3)
        o_nsa = _nsa(pb3, pf3, cmp_pos_k[l], cmp_w1_k[l], cmp_w2_k[l],
                     cmp_pos_v[l], cmp_w1_v[l], cmp_w2_v[l])
        xf = _outproj(o_fox.reshape(bsz * seq, FOX_W), o_sb.reshape(bsz * seq, SB_W),
                      o_nsa.reshape(bsz * seq, NSA_W), xf, w_out[l].astype(BF16),
                      final_g, final=(l == DEPTH - 1))
    return xf.reshape(bsz, seq, d)
```

```python
import functools

import numpy as np
import jax
import jax.numpy as jnp
from jax import lax
from jax.experimental import pallas as pl
from jax.experimental.pallas import tpu as pltpu

F32 = jnp.float32
BF16 = jnp.bfloat16

D_MODEL = 1024
DEPTH = 2
HEAD_DIM = 64
LANES = 128
VT_ROWS = HEAD_DIM + 16
FOX_HEADS = 6
SB_HEADS = 6
NSA_HEADS = 4
FOX_W = FOX_HEADS * HEAD_DIM
SB_W = SB_HEADS * HEAD_DIM
NSA_W = NSA_HEADS * HEAD_DIM
CMP_BLOCK = 32
CMP_STRIDE = 16
SEL_BLOCK = 64
SEL_TOPK = 8
SEL_N_LOCAL = 2
WINDOW = 512
NORM_EPS = 1e-6
QK_SCALE = HEAD_DIM ** -0.5
PEN = -(2.0 ** 100)

PB_FQ, PB_FK, PB_FV, PB_SQ, PB_SK, PB_SV, PB_NQ, PB_NSEL, PB_NWIN = (
    0, 3, 6, 9, 12, 15, 18, 20, 21)
PB_BLOCKS = 22
PF_FZ, PF_SZ, PF_NZ, PF_KVC, PF_MISC = 0, 3, 6, 8, 9
PF_BLOCKS = 10
NB_COLS = PB_BLOCKS * LANES
NF_COLS = PF_BLOCKS * LANES
MISC_PAIR_LANES = 16
MISC_FF_COPIES = 6
MISC_NG = MISC_PAIR_LANES * (FOX_HEADS // 2)

VMEM_PHYSICAL = 64 * 1024 * 1024
VMEM_LIMIT = VMEM_PHYSICAL - 4 * 1024 * 1024

_NT = (((1,), (1,)), ((), ()))


def _dot(a, b):
    return jnp.dot(a, b, preferred_element_type=F32)


def _dot_nt(a, b):
    return lax.dot_general(a, b, _NT, preferred_element_type=F32)


def _sigmoid(x):
    return 1.0 / (1.0 + jnp.exp(-x))


def _log_sigmoid(x):
    return -(jnp.maximum(-x, 0.0) + jnp.log1p(jnp.exp(-jnp.abs(x))))


def _split3(x):
    hi = x.astype(BF16)
    r = x - hi.astype(F32)
    mid = r.astype(BF16)
    lo = (r - mid.astype(F32)).astype(BF16)
    return hi, mid, lo


def _split2(x):
    hi = x.astype(BF16)
    lo = (x - hi.astype(F32)).astype(BF16)
    return hi, lo


PIPELINE_DEPTH = 4


def _emit_pipelined(items, first, second, depth=PIPELINE_DEPTH):
    pending = {}
    for n, item in enumerate(items):
        pending[item] = first(*item)
        if n >= depth:
            prev = items[n - depth]
            second(*prev, pending.pop(prev))
    for prev in items[max(len(items) - depth, 0):]:
        second(*prev, pending.pop(prev))


def _inproj_kernel(x_ref, g_ref, w_ref, pb_ref, pf_ref):
    x = x_ref[...]
    ms = jnp.mean(x * x, axis=-1, keepdims=True)
    h = (x * lax.rsqrt(ms + NORM_EPS) * g_ref[...]).astype(BF16)
    chunk = 4 * LANES
    for c in range(0, NB_COLS, chunk):
        e = min(c + chunk, NB_COLS)
        pb_ref[:, c:e] = _dot(h, w_ref[:, c:e]).astype(BF16)
    for c in range(0, NF_COLS, chunk):
        e = min(c + chunk, NF_COLS)
        pf_ref[:, c:e] = _dot(h, w_ref[:, NB_COLS + c:NB_COLS + e])


def _inproj(xf, g, w_all, tm=512):
    m, d = xf.shape
    return pl.pallas_call(
        _inproj_kernel,
        grid=(m // tm,),
        in_specs=[
            pl.BlockSpec((tm, d), lambda i: (i, 0)),
            pl.BlockSpec((1, d), lambda i: (0, 0)),
            pl.BlockSpec((d, NB_COLS + NF_COLS), lambda i: (0, 0)),
        ],
        out_specs=[
            pl.BlockSpec((tm, NB_COLS), lambda i: (i, 0)),
            pl.BlockSpec((tm, NF_COLS), lambda i: (i, 0)),
        ],
        out_shape=[
            jax.ShapeDtypeStruct((m, NB_COLS), BF16),
            jax.ShapeDtypeStruct((m, NF_COLS), F32),
        ],
        compiler_params=pltpu.CompilerParams(
            dimension_semantics=("arbitrary",), vmem_limit_bytes=VMEM_LIMIT),
        name="inproj",
    )(xf, g.reshape(1, d), w_all)


def _pack_w_in(w):
    sizes = (FOX_W, FOX_W, FOX_W, FOX_HEADS, FOX_W, SB_W, SB_W, SB_W, SB_W,
             NSA_W, HEAD_DIM, HEAD_DIM, HEAD_DIM, HEAD_DIM, HEAD_DIM, HEAD_DIM,
             3 * NSA_HEADS, NSA_W)
    offs = np.concatenate([[0], np.cumsum(sizes)])
    w = w.astype(BF16)
    (fq, fk, fv, ff, fz, sq, sk, sv, sz, nq, nkc, nvc, nks, nvs, nkw, nvw, ng, nz) = [
        w[:, offs[i]:offs[i + 1]] for i in range(len(sizes))]
    zeros = jnp.zeros((w.shape[0], LANES), w.dtype)
    cols = [fq * QK_SCALE, fk, fv, sq * QK_SCALE, sk, sv, nq * QK_SCALE,
            nks, nvs, nkw, nvw,
            fz, sz, nz, nkc, nvc, _misc_ff_layout(ff, zeros), ng,
            zeros[:, :LANES - MISC_NG - 3 * NSA_HEADS]]
    return jnp.concatenate(cols, axis=1)


def _misc_ff_layout(ff, zeros):
    cols = []
    for p in range(FOX_HEADS // 2):
        cols += [ff[:, 2 * p:2 * p + 1]] * MISC_FF_COPIES + [ff[:, 2 * p + 1:2 * p + 2]] * MISC_FF_COPIES
        cols.append(zeros[:, :MISC_PAIR_LANES - 2 * MISC_FF_COPIES])
    return jnp.concatenate(cols, axis=1)


def _outproj_kernel(of_ref, os_ref, on_ref, x_ref, w_ref, g_ref, o_ref, *, final):
    y = (x_ref[...]
         + _dot(of_ref[...], w_ref[0:FOX_W, :])
         + _dot(os_ref[...], w_ref[FOX_W:FOX_W + SB_W, :])
         + _dot(on_ref[...], w_ref[FOX_W + SB_W:, :]))
    if final:
        ms = jnp.mean(y * y, axis=-1, keepdims=True)
        y = y * lax.rsqrt(ms + NORM_EPS) * g_ref[...]
    o_ref[...] = y


def _outproj(o_fox, o_sb, o_nsa, xf, w, g, final, tm=2048):
    m, d = xf.shape
    return pl.pallas_call(
        functools.partial(_outproj_kernel, final=final),
        grid=(m // tm,),
        in_specs=[
            pl.BlockSpec((tm, FOX_W), lambda i: (i, 0)),
            pl.BlockSpec((tm, SB_W), lambda i: (i, 0)),
            pl.BlockSpec((tm, NSA_W), lambda i: (i, 0)),
            pl.BlockSpec((tm, d), lambda i: (i, 0)),
            pl.BlockSpec((d, d), lambda i: (0, 0)),
            pl.BlockSpec((1, d), lambda i: (0, 0)),
        ],
        out_specs=pl.BlockSpec((tm, d), lambda i: (i, 0)),
        out_shape=jax.ShapeDtypeStruct((m, d), F32),
        compiler_params=pltpu.CompilerParams(
            dimension_semantics=("arbitrary",), vmem_limit_bytes=VMEM_LIMIT),
        name="outproj_final" if final else "outproj",
    )(o_fox, o_sb, o_nsa, xf, w, g.reshape(1, d))


def _fox_kernel(q_ref, k_ref, v_ref, z_ref, misc_ref, bf_ref, tri_ref,
                o_ref, qq_ref, kk_ref, vpt_ref, cps_ref, *, seq, blk):
    pair = pl.program_id(1)
    low = lax.broadcasted_iota(jnp.int32, (blk, LANES), 1) < HEAD_DIM
    lane_b = lax.broadcasted_iota(jnp.int32, (blk, LANES), 1)
    nblk = seq // blk

    @pl.when(pair == 0)
    def _():
        tri = tri_ref[...]
        part = (lane_b % MISC_PAIR_LANES) % 3
        carry = jnp.zeros((1, LANES), F32)
        for b in range(nblk):
            rows = slice(b * blk, (b + 1) * blk)
            ls = _log_sigmoid(misc_ref[rows, :] + bf_ref[...])
            c3 = _dot(tri, jnp.concatenate(_split3(ls), axis=1))
            cb = c3[:, 0:LANES] + c3[:, LANES:2 * LANES] + c3[:, 2 * LANES:3 * LANES] + carry
            carry = cb[blk - 1:blk, :]
            hi, mid, lo = _split3(cb)
            cps_ref[rows, :] = jnp.where(part == 0, hi.astype(F32), jnp.where(
                part == 1, mid.astype(F32), lo.astype(F32))).astype(BF16)

    rel = lane_b - MISC_PAIR_LANES * pair
    third = jnp.where(rel < 0, -1, jnp.where(rel < 3, 0, jnp.where(rel < 6, 1, jnp.where(
        rel < 9, 2, jnp.where(rel < 12, 3, -1)))))

    def lanes_of(*thirds):
        sel = jnp.zeros((blk, LANES), F32)
        for t in thirds:
            sel = jnp.where(third == t, 1.0, sel)
        return sel.astype(BF16)

    k_ones, k_parts = lanes_of(0, 2), lanes_of(1, 3)
    qa_parts, qa_ones = lanes_of(0), lanes_of(1)
    qb_parts, qb_ones = lanes_of(2), lanes_of(3)
    eye = jnp.where(lax.broadcasted_iota(jnp.int32, (LANES, LANES), 0)
                    == lax.broadcasted_iota(jnp.int32, (LANES, LANES), 1), 1.0, 0.0).astype(BF16)
    for b in range(nblk):
        rows = slice(b * blk, (b + 1) * blk)
        cps = cps_ref[rows, :]
        q2 = q_ref[rows, :]
        v2 = v_ref[rows, :]
        zero = jnp.zeros_like(q2)
        one = jnp.ones_like(q2)
        kk_ref[rows, 0:LANES] = k_ref[rows, :]
        kk_ref[rows, LANES:2 * LANES] = k_ones - cps * k_parts
        qq_ref[0, rows, 0:LANES] = jnp.where(low, q2, zero)
        qq_ref[0, rows, LANES:2 * LANES] = cps * qa_parts + qa_ones
        qq_ref[1, rows, 0:LANES] = jnp.where(low, zero, q2)
        qq_ref[1, rows, LANES:2 * LANES] = cps * qb_parts + qb_ones
        vt = _dot_nt(eye, v2).astype(BF16)
        vpt_ref[0, 0:HEAD_DIM, rows] = vt[0:HEAD_DIM]
        vpt_ref[1, 0:HEAD_DIM, rows] = vt[HEAD_DIM:LANES]
        vpt_ref[0, HEAD_DIM:VT_ROWS, rows] = jnp.ones((VT_ROWS - HEAD_DIM, blk), BF16)
        vpt_ref[1, HEAD_DIM:VT_ROWS, rows] = jnp.ones((VT_ROWS - HEAD_DIM, blk), BF16)

    causal_t = (lax.broadcasted_iota(jnp.int32, (blk, blk), 0)
                <= lax.broadcasted_iota(jnp.int32, (blk, blk), 1))

    def scores(i, h):
        r0 = i * blk
        rows = slice(r0, r0 + blk)
        qa = qq_ref[h, rows, :]
        sd = jnp.where(causal_t, _dot_nt(kk_ref[rows, :], qa), -jnp.inf)
        m = jnp.max(sd, axis=0, keepdims=True)
        sm = None
        if i > 0:
            sm = _dot_nt(kk_ref[0:r0, :], qa)
            m = jnp.maximum(m, jnp.max(sm, axis=0, keepdims=True))
        return sd, sm, m

    outs = {}

    def weighted_sum(i, h, st):
        sd, sm, m = st
        rows = slice(i * blk, (i + 1) * blk)
        p = jnp.exp(sd - m).astype(BF16)
        if i > 0:
            p = jnp.concatenate([jnp.exp(sm - m).astype(BF16), p], axis=0)
        a = _dot(vpt_ref[h, :, 0:(i + 1) * blk], p)
        outs[i, h] = a[0:HEAD_DIM] / a[HEAD_DIM:HEAD_DIM + 1]
        if h == 1:
            o = jnp.concatenate([outs.pop((i, 0)), outs.pop((i, 1))], axis=0).T
            z = z_ref[rows, :]
            o_ref[rows, :] = (o * (z * _sigmoid(z))).astype(BF16)

    _emit_pipelined([(i, h) for i in range(nblk) for h in range(2)], scores, weighted_sum)


def _fox(pb3, pf3, b_f, blk=256):
    bsz, seq, _ = pb3.shape
    npair = FOX_HEADS // 2
    bias = jnp.zeros((1, LANES), F32).at[:, :MISC_NG].set(
        _misc_ff_layout(b_f.reshape(1, FOX_HEADS), jnp.zeros((1, LANES), F32)))
    tri = jnp.asarray(np.tril(np.ones((blk, blk), np.float32)), BF16)

    def col(base):
        return pl.BlockSpec((None, seq, LANES), lambda b, p: (b, 0, base + p))

    def whole(shape):
        return pl.BlockSpec(shape, lambda b, p: (0,) * len(shape))

    return pl.pallas_call(
        functools.partial(_fox_kernel, seq=seq, blk=blk),
        grid=(bsz, npair),
        in_specs=[
            col(PB_FQ), col(PB_FK), col(PB_FV), col(PF_FZ),
            pl.BlockSpec((None, seq, LANES), lambda b, p: (b, 0, PF_MISC)),
            whole((1, LANES)), whole((blk, blk)),
        ],
        out_specs=pl.BlockSpec((None, seq, LANES), lambda b, p: (b, 0, p)),
        out_shape=jax.ShapeDtypeStruct((bsz, seq, FOX_W), BF16),
        scratch_shapes=[
            pltpu.VMEM((2, seq, 2 * LANES), BF16),
            pltpu.VMEM((seq, 2 * LANES), BF16),
            pltpu.VMEM((2, VT_ROWS, seq), BF16),
            pltpu.VMEM((seq, LANES), BF16),
        ],
        compiler_params=pltpu.CompilerParams(
            dimension_semantics=("arbitrary", "arbitrary"), vmem_limit_bytes=VMEM_LIMIT),
        name="fox_attn",
    )(pb3, pb3, pb3, pf3, pf3, bias, tri)


SB_NEAR_TILES = 2
SB_DEAD = -104.0


def _sb_kernel(q_ref, k_ref, v_ref, z_ref, suf_ref, o_ref, acc_ref, carry_ref, *, seq, blk):
    low = lax.broadcasted_iota(jnp.int32, (blk, LANES), 1) < HEAD_DIM
    strict = (lax.broadcasted_iota(jnp.int32, (blk, blk), 1)
              < lax.broadcasted_iota(jnp.int32, (blk, blk), 0))
    suf = suf_ref[...]

    def log_terms(qh, j_hi, j_lo):
        z = _dot_nt(qh, k_ref[j_lo * blk:(j_hi + 1) * blk, :])
        lsz = jnp.minimum(z, 0.0) - jnp.log(1.0 + jnp.exp(-jnp.abs(z)))
        return lsz, lsz - z

    def tiles(i, j_hi, j_lo, carry, terms):
        lsz, l1m = terms
        k0 = j_lo * blk
        out = None
        for j in range(j_hi, j_lo - 1, -1):
            cs = slice(j * blk - k0, (j + 1) * blk - k0)
            l1 = l1m[:, cs]
            if j == i:
                l1 = jnp.where(strict, l1, 0.0)
            r = _dot(l1.astype(BF16), suf)
            a = jnp.exp(lsz[:, cs] + (r + carry))
            if j == i:
                a = jnp.where(strict, a, 0.0)
            pv = _dot(a.astype(BF16), v_ref[j * blk:(j + 1) * blk, :])
            out = pv if out is None else out + pv
            carry = carry + (r[:, 0:1] + l1[:, 0:1])
        return out, carry

    nblk = seq // blk

    def q_head(i, h):
        q2 = q_ref[i * blk:(i + 1) * blk, :]
        zero = jnp.zeros_like(q2)
        return jnp.where(low, q2, zero) if h == 0 else jnp.where(low, zero, q2)

    def j_near(i):
        return max(i - SB_NEAR_TILES + 1, 0)

    alive = {}

    def near_terms(i, h):
        return log_terms(q_head(i, h), i, j_near(i))

    def near_tiles(i, h, terms):
        out, carry = tiles(i, i, j_near(i), jnp.zeros((blk, 1), F32), terms)
        acc_ref[i, h] = out
        if j_near(i) > 0:
            carry_ref[i, h] = jnp.broadcast_to(carry, (blk, LANES))
            alive[i, h] = jnp.max(carry) >= SB_DEAD

    _emit_pipelined([(i, h) for i in range(nblk) for h in range(2)], near_terms, near_tiles)

    for (i, h), flag in alive.items():
        @pl.when(flag)
        def _(i=i, h=h):
            far, _ = tiles(i, j_near(i) - 1, 0, carry_ref[i, h][:, 0:1],
                           log_terms(q_head(i, h), j_near(i) - 1, 0))
            acc_ref[i, h] = acc_ref[i, h] + far

    for i in range(nblk):
        rows = slice(i * blk, (i + 1) * blk)
        o = jnp.where(low, acc_ref[i, 0], acc_ref[i, 1])
        zg = z_ref[rows, :]
        o_ref[rows, :] = (o * (zg * _sigmoid(zg))).astype(BF16)


def _sb(pb3, pf3, blk=256):
    bsz, seq, _ = pb3.shape
    npair = SB_HEADS // 2
    suf = jnp.asarray(np.tril(np.ones((blk, blk), np.float32), -1), BF16)

    def col(base):
        return pl.BlockSpec((None, seq, LANES), lambda b, p: (b, 0, base + p))

    return pl.pallas_call(
        functools.partial(_sb_kernel, seq=seq, blk=blk),
        grid=(bsz, npair),
        in_specs=[
            col(PB_SQ), col(PB_SK), col(PB_SV), col(PF_SZ),
            pl.BlockSpec(suf.shape, lambda b, p: (0, 0)),
        ],
        out_specs=pl.BlockSpec((None, seq, LANES), lambda b, p: (b, 0, p)),
        out_shape=jax.ShapeDtypeStruct((bsz, seq, SB_W), BF16),
        scratch_shapes=[pltpu.VMEM((seq // blk, 2, blk, LANES), F32),
                        pltpu.VMEM((seq // blk, 2, blk, LANES), F32)],
        compiler_params=pltpu.CompilerParams(
            dimension_semantics=("arbitrary", "arbitrary"), vmem_limit_bytes=VMEM_LIMIT),
        name="sb_attn",
    )(pb3, pb3, pb3, pf3, suf)


NSA_HEAD_ORDER = (0, 1, 2, 3)
SEL_LANES = 32
POS_HI_LANE = 32
POS_LO_LANE = 33
NSA_SLOPES = tuple(2.0 ** (-8.0 * (g + 1) / NSA_HEADS) for g in range(NSA_HEADS))


def _nsa_kernel(q_ref, ksv_ref, kwv_ref, z_ref, misc_ref, kvc_ref,
                pos_ref, w1_ref, w2k_ref, w2v_ref,
                mselt_ref, kaugs_ref, kaugw_ref, qslope_ref,
                o_ref, kc_ref, vct_ref, kks_ref, kkw_ref, vst_ref, vwt_ref, *, seq, blk):
    cols4 = NSA_HEADS * blk
    n_cmp = (seq - CMP_BLOCK) // CMP_STRIDE + 1
    n_sel = seq // SEL_BLOCK
    nwin = WINDOW // blk
    eye = jnp.where(lax.broadcasted_iota(jnp.int32, (LANES, LANES), 0)
                    == lax.broadcasted_iota(jnp.int32, (LANES, LANES), 1), 1.0, 0.0).astype(BF16)

    nchunk = seq // CMP_STRIDE
    ha = jnp.zeros((nchunk, LANES), F32)
    hb = jnp.zeros((nchunk, LANES), F32)
    for p in range(CMP_STRIDE):
        xp = kvc_ref[pl.ds(p, nchunk, stride=CMP_STRIDE), :]
        q = p + CMP_STRIDE
        ha = ha + _dot((xp + pos_ref[p:p + 1, :]).astype(BF16), w1_ref[p])
        hb = hb + _dot((xp + pos_ref[q:q + 1, :]).astype(BF16), w1_ref[q])
    hid = ha + pltpu.roll(hb, nchunk - 1, axis=0)
    hid = (hid * _sigmoid(hid)).astype(BF16)
    kc_ref[...] = _dot(hid, w2k_ref[...]).astype(BF16)
    vct_ref[...] = _dot_nt(eye, _dot(hid, w2v_ref[...]).astype(BF16)).astype(BF16)

    tb = 2 * LANES
    lane_t = lax.broadcasted_iota(jnp.int32, (tb, LANES), 1)
    low_t = lane_t < HEAD_DIM
    er = lax.broadcasted_iota(jnp.int32, (LANES, LANES), 0)
    ec = lax.broadcasted_iota(jnp.int32, (LANES, LANES), 1)
    dup_lo = jnp.where(er == ec % HEAD_DIM, 1.0, 0.0).astype(BF16)
    swap = jnp.where(ec == (er + HEAD_DIM) % LANES, 1.0, 0.0).astype(BF16)
    kks_ref[:, LANES:2 * LANES] = kaugs_ref[...]
    kkw_ref[:, LANES:2 * LANES] = kaugw_ref[...]
    for b in range(seq // tb):
        rows = slice(b * tb, (b + 1) * tb)
        xs = ksv_ref[rows, :]
        xw = kwv_ref[rows, :]
        one = jnp.ones_like(xs)
        kks_ref[rows, 0:LANES] = _dot(xs, dup_lo).astype(BF16)
        kkw_ref[rows, 0:LANES] = _dot(xw, dup_lo).astype(BF16)
        vst_ref[:, rows] = _dot_nt(swap, jnp.where(low_t, one, xs)).astype(BF16)[0:VT_ROWS]
        vwt_ref[:, rows] = _dot_nt(swap, jnp.where(low_t, one, xw)).astype(BF16)[0:VT_ROWS]

    low = lax.broadcasted_iota(jnp.int32, (blk, LANES), 1) < HEAD_DIM
    colt = lax.broadcasted_iota(jnp.int32, (1, cols4), 1)
    grp = colt // blk
    tok = colt - grp * blk
    gslope = [NSA_SLOPES[g] for g in NSA_HEAD_ORDER]
    slope = jnp.where(grp == 0, gslope[0], jnp.where(grp == 1, gslope[1],
                      jnp.where(grp == 2, gslope[2], gslope[3]))).astype(F32)
    keyr = lax.broadcasted_iota(jnp.int32, (blk, 1), 0)
    causal_t = keyr <= tok
    after_t = keyr > tok
    cmpr = lax.broadcasted_iota(jnp.int32, (LANES, 1), 0)
    cmp_end = (cmpr * CMP_STRIDE + (CMP_BLOCK - 1)).astype(F32)
    jrow = lax.broadcasted_iota(jnp.int32, (n_sel, blk), 0)
    tcol = lax.broadcasted_iota(jnp.int32, (n_sel, blk), 1)
    sel_lane = lax.broadcasted_iota(jnp.int32, (cols4, LANES), 1) < SEL_LANES
    qslope = qslope_ref[...]

    def col_max(parts):
        m = None
        for s in parts:
            mp = jnp.max(s, axis=0, keepdims=True)
            m = mp if m is None else jnp.maximum(m, mp)
        return m

    def softmax_pv(parts, m, keys, vt_ref):
        p = [jnp.exp(s - m).astype(BF16) for s in parts]
        p = p[0] if len(p) == 1 else jnp.concatenate(p, axis=0)
        acc = _dot(vt_ref[:, keys], p)
        return acc[0:HEAD_DIM] / acc[HEAD_DIM:HEAD_DIM + 1]

    def scores(i):
        qs = i * blk
        rows = slice(qs, qs + blk)
        q01 = q_ref[rows, 0:LANES]
        q23 = q_ref[rows, LANES:2 * LANES]
        zero = jnp.zeros_like(q01)
        qst = jnp.concatenate([jnp.where(low, q01, zero), jnp.where(low, zero, q01),
                               jnp.where(low, q23, zero), jnp.where(low, zero, q23)], axis=0)

        dist_c = (qs + tok).astype(F32) - cmp_end
        valid_c = (dist_c >= 0.0) & (cmpr < n_cmp)
        sc = _dot_nt(kc_ref[...], qst) - slope * dist_c
        sc = jnp.where(valid_c, sc, -jnp.inf)
        mc = jnp.max(sc, axis=0, keepdims=True)
        mc = jnp.where(mc == -jnp.inf, 0.0, mc)
        pc = jnp.exp(sc - mc)
        ssum = jnp.sum(pc, axis=0, keepdims=True)
        pc = pc / jnp.where(ssum > 0.0, ssum, 1.0)
        oc = _dot(vct_ref[0:HEAD_DIM, :], pc.astype(BF16))

        pcs = (pc[:, 0:blk] + pc[:, blk:2 * blk] + pc[:, 2 * blk:3 * blk]
               + pc[:, 3 * blk:4 * blk])
        hi, lo = _split2(pcs)
        imp = _dot(mselt_ref[...], jnp.concatenate([hi, lo], axis=0))
        back = (qs + tcol) // SEL_BLOCK - jrow
        imp = jnp.where(back < 0, -jnp.inf,
                        jnp.where(back < SEL_N_LOCAL, jnp.inf,
                                  jnp.where(jrow == 0, jnp.inf, imp)))
        rank = jnp.zeros((n_sel, blk), F32)
        for c in range(n_sel):
            rowc = imp[c:c + 1, :]
            tie = jnp.where(jrow > c, 1.0, 0.0)
            rank = rank + jnp.where(rowc > imp, 1.0, jnp.where(rowc == imp, tie, 0.0))
        unsel_t = jnp.where(rank >= float(SEL_TOPK), 1.0, 0.0).astype(BF16)
        unsel_t = jnp.concatenate([unsel_t, jnp.zeros((LANES - n_sel, blk), BF16)], axis=0)
        unsel = _dot_nt(eye, unsel_t).astype(BF16)

        qop_w = jnp.concatenate([qst, qslope], axis=1)
        qaug = jnp.where(sel_lane, jnp.concatenate([unsel] * NSA_HEADS, axis=0), qslope)
        qop_s = jnp.concatenate([qst, qaug], axis=1)

        win = []
        w0 = max(qs - WINDOW, 0)
        if i >= nwin:
            win.append(jnp.where(after_t, _dot_nt(kkw_ref[w0:w0 + blk, :], qop_w), -jnp.inf))
        wm = max(qs - WINDOW + blk, 0)
        if qs > wm:
            win.append(_dot_nt(kkw_ref[wm:qs, :], qop_w))
        win.append(jnp.where(causal_t, _dot_nt(kkw_ref[rows, :], qop_w), -jnp.inf))

        sel = []
        if i > 0:
            sel.append(_dot_nt(kks_ref[0:qs, :], qop_s))
        sel.append(jnp.where(causal_t, _dot_nt(kks_ref[rows, :], qop_s), -jnp.inf))
        return oc, win, col_max(win), sel, col_max(sel)

    def outputs(i, st):
        oc, win, m_win, sel, m_sel = st
        qs = i * blk
        rows = slice(qs, qs + blk)
        o_win = softmax_pv(win, m_win, slice(max(qs - WINDOW, 0), qs + blk), vwt_ref)
        o_sel = softmax_pv(sel, m_sel, slice(0, qs + blk), vst_ref)

        sg_t = _sigmoid(misc_ref[rows, :]).T

        def gate(branch):
            return jnp.concatenate(
                [sg_t[MISC_NG + 3 * g + branch:MISC_NG + 3 * g + branch + 1, :]
                 for g in NSA_HEAD_ORDER], axis=1)

        tot = oc * gate(0) + o_sel * gate(1) + o_win * gate(2)
        by_head = {g: tot[:, k * blk:(k + 1) * blk] for k, g in enumerate(NSA_HEAD_ORDER)}
        out = jnp.concatenate([by_head[g] for g in range(NSA_HEADS)], axis=0).T
        z = z_ref[rows, :]
        o_ref[rows, :] = (out * (z * _sigmoid(z))).astype(BF16)

    _emit_pipelined([(i,) for i in range(seq // blk)], scores, outputs, depth=1)


def _nsa_constants(seq, blk):
    n_cmp = (seq - CMP_BLOCK) // CMP_STRIDE + 1
    n_sel = seq // SEL_BLOCK
    cs = np.arange(n_cmp) * CMP_STRIDE
    ce = cs + CMP_BLOCK - 1
    ss = np.arange(n_sel) * SEL_BLOCK
    se = ss + SEL_BLOCK - 1
    msel_t = np.zeros((n_sel, LANES), np.float32)
    msel_t[:, :n_cmp] = ((cs[:, None] <= se[None, :]) & (ce[:, None] >= ss[None, :])).T
    msel_t = np.concatenate([msel_t, msel_t], axis=1)
    key = np.arange(seq)
    kaug_w = np.zeros((seq, LANES), np.float32)
    kaug_w[:, POS_HI_LANE] = key // 16
    kaug_w[:, POS_LO_LANE] = key % 16
    kaug_s = kaug_w.copy()
    kaug_s[key, key // SEL_BLOCK] = PEN
    qslope = np.zeros((NSA_HEADS * blk, LANES), np.float32)
    for k, g in enumerate(NSA_HEAD_ORDER):
        qslope[k * blk:(k + 1) * blk, POS_HI_LANE] = 16.0 * NSA_SLOPES[g]
        qslope[k * blk:(k + 1) * blk, POS_LO_LANE] = NSA_SLOPES[g]
    return (jnp.asarray(msel_t, BF16), jnp.asarray(kaug_s, BF16), jnp.asarray(kaug_w, BF16),
            jnp.asarray(qslope, BF16))


def _nsa(pb3, pf3, pos_k, w1_k, w2_k, pos_v, w1_v, w2_v, blk=128):
    bsz, seq, _ = pb3.shape
    nchunk = seq // CMP_STRIDE
    w1k = w1_k.astype(BF16).reshape(CMP_BLOCK, HEAD_DIM, HEAD_DIM)
    w1v = w1_v.astype(BF16).reshape(CMP_BLOCK, HEAD_DIM, HEAD_DIM)
    w1z = jnp.zeros_like(w1k)
    w1 = jnp.concatenate([jnp.concatenate([w1k, w1z], axis=2),
                          jnp.concatenate([w1z, w1v], axis=2)], axis=1)
    pos = jnp.concatenate([pos_k, pos_v], axis=1)
    zero = jnp.zeros((LANES, LANES), F32)
    w2k = zero.at[:HEAD_DIM, :].set(jnp.concatenate([w2_k, w2_k], axis=1))
    w2v = zero.at[HEAD_DIM:, :].set(jnp.concatenate([w2_v, w2_v], axis=1))
    consts = _nsa_constants(seq, blk)

    def col(base, nblk=1):
        return pl.BlockSpec((None, seq, nblk * LANES), lambda b: (b, 0, base // nblk))

    def whole(shape):
        return pl.BlockSpec(shape, lambda b: (0,) * len(shape))

    return pl.pallas_call(
        functools.partial(_nsa_kernel, seq=seq, blk=blk),
        grid=(bsz,),
        in_specs=[
            col(PB_NQ, 2), col(PB_NSEL), col(PB_NWIN),
            col(PF_NZ, 2), col(PF_MISC), col(PF_KVC),
            whole(pos.shape), whole(w1.shape), whole(w2k.shape), whole(w2v.shape),
        ] + [whole(c.shape) for c in consts],
        out_specs=pl.BlockSpec((None, seq, NSA_W), lambda b: (b, 0, 0)),
        out_shape=jax.ShapeDtypeStruct((bsz, seq, NSA_W), BF16),
        scratch_shapes=[
            pltpu.VMEM((nchunk, LANES), BF16),
            pltpu.VMEM((LANES, nchunk), BF16),
            pltpu.VMEM((seq, 2 * LANES), BF16),
            pltpu.VMEM((seq, 2 * LANES), BF16),
            pltpu.VMEM((VT_ROWS, seq), BF16),
            pltpu.VMEM((VT_ROWS, seq), BF16),
        ],
        compiler_params=pltpu.CompilerParams(
            dimension_semantics=("arbitrary",), vmem_limit_bytes=VMEM_LIMIT),
        name="nsa_attn",
    )(pb3, pb3, pb3, pf3, pf3, pf3, pos, w1, w2k.astype(BF16), w2v.astype(BF16), *consts)


def kernel(x, norm_g, w_in, b_f, cmp_pos_k, cmp_w1_k, cmp_w2_k,
           cmp_pos_v, cmp_w1_v, cmp_w2_v, w_out, final_g):
    bsz, seq, d = x.shape
    xf = x.reshape(bsz * seq, d)
    for l in range(DEPTH):
        pb, pf = _inproj(xf, norm_g[l], _pack_w_in(w_in[l]))
        pb3 = pb.reshape(bsz, seq, NB_COLS)
        pf3 = pf.reshape(bsz, seq, NF_COLS)
        o_fox = _fox(pb3, pf3, b_f[l])
        o_sb = _sb(pb3, pf3)
        o_nsa = _nsa(pb3, pf3, cmp_pos_k[l], cmp_w1_k[l], cmp_w2_k[l],
                     cmp_pos_v[l], cmp_w1_v[l], cmp_w2_v[l])
        xf = _outproj(o_fox.reshape(bsz * seq, FOX_W), o_sb.reshape(bsz * seq, SB_W),
                      o_nsa.reshape(bsz * seq, NSA_W), xf, w_out[l].astype(BF16),
                      final_g, final=(l == DEPTH - 1))
    return xf.reshape(bsz, seq, d)
```

```python
import functools

import numpy as np
import jax
import jax.numpy as jnp
from jax import lax
from jax.experimental import pallas as pl
from jax.experimental.pallas import tpu as pltpu

F32 = jnp.float32
BF16 = jnp.bfloat16

D_MODEL = 1024
DEPTH = 2
HEAD_DIM = 64
LANES = 128
VT_ROWS = HEAD_DIM + 16
FOX_HEADS = 6
SB_HEADS = 6
NSA_HEADS = 4
FOX_W = FOX_HEADS * HEAD_DIM
SB_W = SB_HEADS * HEAD_DIM
NSA_W = NSA_HEADS * HEAD_DIM
CMP_BLOCK = 32
CMP_STRIDE = 16
SEL_BLOCK = 64
SEL_TOPK = 8
SEL_N_LOCAL = 2
WINDOW = 512
NORM_EPS = 1e-6
QK_SCALE = HEAD_DIM ** -0.5
PEN = -(2.0 ** 100)

PB_FQ, PB_FK, PB_FV, PB_SQ, PB_SK, PB_SV, PB_NQ, PB_NSEL, PB_NWIN = (
    0, 3, 6, 9, 12, 15, 18, 20, 21)
PB_BLOCKS = 22
PF_FZ, PF_SZ, PF_NZ, PF_KVC, PF_MISC = 0, 3, 6, 8, 9
PF_BLOCKS = 10
NB_COLS = PB_BLOCKS * LANES
NF_COLS = PF_BLOCKS * LANES
MISC_PAIR_LANES = 16
MISC_FF_COPIES = 6
MISC_NG = MISC_PAIR_LANES * (FOX_HEADS // 2)

VMEM_PHYSICAL = 64 * 1024 * 1024
VMEM_LIMIT = VMEM_PHYSICAL - 4 * 1024 * 1024

_NT = (((1,), (1,)), ((), ()))


def _dot(a, b):
    return jnp.dot(a, b, preferred_element_type=F32)


def _dot_nt(a, b):
    return lax.dot_general(a, b, _NT, preferred_element_type=F32)


def _sigmoid(x):
    return 1.0 / (1.0 + jnp.exp(-x))


def _log_sigmoid(x):
    return -(jnp.maximum(-x, 0.0) + jnp.log1p(jnp.exp(-jnp.abs(x))))


def _split3(x):
    hi = x.astype(BF16)
    r = x - hi.astype(F32)
    mid = r.astype(BF16)
    lo = (r - mid.astype(F32)).astype(BF16)
    return hi, mid, lo


def _split2(x):
    hi = x.astype(BF16)
    lo = (x - hi.astype(F32)).astype(BF16)
    return hi, lo


PIPELINE_DEPTH = 4


def _emit_pipelined(items, first, second, depth=PIPELINE_DEPTH):
    pending = {}
    for n, item in enumerate(items):
        pending[item] = first(*item)
        if n >= depth:
            prev = items[n - depth]
            second(*prev, pending.pop(prev))
    for prev in items[max(len(items) - depth, 0):]:
        second(*prev, pending.pop(prev))


def _proj_kernel(*refs, has_out, has_in):
    it = iter(refs)
    y = next(it)[...]
    if has_out:
        of_ref, os_ref, on_ref, wo_ref = next(it), next(it), next(it), next(it)
        y = (y + _dot(of_ref[...], wo_ref[0:FOX_W, :])
             + _dot(os_ref[...], wo_ref[FOX_W:FOX_W + SB_W, :])
             + _dot(on_ref[...], wo_ref[FOX_W + SB_W:, :]))
    g_ref = next(it)
    ms = jnp.mean(y * y, axis=-1, keepdims=True)
    h = y * lax.rsqrt(ms + NORM_EPS) * g_ref[...]
    if not has_in:
        next(it)[...] = h
        return
    wi_ref = next(it)
    if has_out:
        next(it)[...] = y
    pb_ref, pf_ref = next(it), next(it)
    h = h.astype(BF16)
    chunk = 4 * LANES
    for c in range(0, NB_COLS, chunk):
        e = min(c + chunk, NB_COLS)
        pb_ref[:, c:e] = _dot(h, wi_ref[:, c:e]).astype(BF16)
    for c in range(0, NF_COLS, chunk):
        e = min(c + chunk, NF_COLS)
        pf_ref[:, c:e] = _dot(h, wi_ref[:, NB_COLS + c:NB_COLS + e])


def _proj(xf, gain, attn=None, w_out=None, w_in=None):
    m, d = xf.shape
    has_out, has_in = attn is not None, w_in is not None
    tm = 512 if has_in else 2048

    def rows(width):
        return pl.BlockSpec((tm, width), lambda i: (i, 0))

    def whole(shape):
        return pl.BlockSpec(shape, lambda i: (0, 0))

    args, in_specs, out_specs, out_shape = [xf], [rows(d)], [], []
    if has_out:
        args += [*attn, w_out]
        in_specs += [rows(FOX_W), rows(SB_W), rows(NSA_W), whole((d, d))]
    args.append(gain.reshape(1, d))
    in_specs.append(whole((1, d)))
    if has_in:
        args.append(w_in)
        in_specs.append(whole(w_in.shape))
    if has_out or not has_in:
        out_specs.append(rows(d))
        out_shape.append(jax.ShapeDtypeStruct((m, d), F32))
    if has_in:
        out_specs += [rows(NB_COLS), rows(NF_COLS)]
        out_shape += [jax.ShapeDtypeStruct((m, NB_COLS), BF16),
                      jax.ShapeDtypeStruct((m, NF_COLS), F32)]
    return pl.pallas_call(
        functools.partial(_proj_kernel, has_out=has_out, has_in=has_in),
        grid=(m // tm,),
        in_specs=in_specs,
        out_specs=out_specs,
        out_shape=out_shape,
        compiler_params=pltpu.CompilerParams(
            dimension_semantics=("arbitrary",), vmem_limit_bytes=VMEM_LIMIT),
        name="proj_" + ("out" if has_out else "") + ("in" if has_in else "norm"),
    )(*args)


def _pack_w_in(w):
    sizes = (FOX_W, FOX_W, FOX_W, FOX_HEADS, FOX_W, SB_W, SB_W, SB_W, SB_W,
             NSA_W, HEAD_DIM, HEAD_DIM, HEAD_DIM, HEAD_DIM, HEAD_DIM, HEAD_DIM,
             3 * NSA_HEADS, NSA_W)
    offs = np.concatenate([[0], np.cumsum(sizes)])
    w = w.astype(BF16)
    (fq, fk, fv, ff, fz, sq, sk, sv, sz, nq, nkc, nvc, nks, nvs, nkw, nvw, ng, nz) = [
        w[:, offs[i]:offs[i + 1]] for i in range(len(sizes))]
    zeros = jnp.zeros((w.shape[0], LANES), w.dtype)
    cols = [fq * QK_SCALE, fk, fv, sq * QK_SCALE, sk, sv, nq * QK_SCALE,
            nks, nvs, nkw, nvw,
            fz, sz, nz, nkc, nvc, _misc_ff_layout(ff, zeros), ng,
            zeros[:, :LANES - MISC_NG - 3 * NSA_HEADS]]
    return jnp.concatenate(cols, axis=1)


def _misc_ff_layout(ff, zeros):
    cols = []
    for p in range(FOX_HEADS // 2):
        cols += [ff[:, 2 * p:2 * p + 1]] * MISC_FF_COPIES + [ff[:, 2 * p + 1:2 * p + 2]] * MISC_FF_COPIES
        cols.append(zeros[:, :MISC_PAIR_LANES - 2 * MISC_FF_COPIES])
    return jnp.concatenate(cols, axis=1)


def _fox_kernel(q_ref, k_ref, v_ref, z_ref, misc_ref, bf_ref, tri_ref,
                o_ref, qq_ref, kk_ref, vpt_ref, cps_ref, *, seq, blk):
    pair = pl.program_id(1)
    low = lax.broadcasted_iota(jnp.int32, (blk, LANES), 1) < HEAD_DIM
    lane_b = lax.broadcasted_iota(jnp.int32, (blk, LANES), 1)
    nblk = seq // blk

    @pl.when(pair == 0)
    def _():
        tri = tri_ref[...]
        part = (lane_b % MISC_PAIR_LANES) % 3
        carry = jnp.zeros((1, LANES), F32)
        for b in range(nblk):
            rows = slice(b * blk, (b + 1) * blk)
            ls = _log_sigmoid(misc_ref[rows, :] + bf_ref[...])
            c3 = _dot(tri, jnp.concatenate(_split3(ls), axis=1))
            cb = c3[:, 0:LANES] + c3[:, LANES:2 * LANES] + c3[:, 2 * LANES:3 * LANES] + carry
            carry = cb[blk - 1:blk, :]
            hi, mid, lo = _split3(cb)
            cps_ref[rows, :] = jnp.where(part == 0, hi.astype(F32), jnp.where(
                part == 1, mid.astype(F32), lo.astype(F32))).astype(BF16)

    rel = lane_b - MISC_PAIR_LANES * pair
    third = jnp.where(rel < 0, -1, jnp.where(rel < 3, 0, jnp.where(rel < 6, 1, jnp.where(
        rel < 9, 2, jnp.where(rel < 12, 3, -1)))))

    def lanes_of(*thirds):
        sel = jnp.zeros((blk, LANES), F32)
        for t in thirds:
            sel = jnp.where(third == t, 1.0, sel)
        return sel.astype(BF16)

    k_ones, k_parts = lanes_of(0, 2), lanes_of(1, 3)
    qa_parts, qa_ones = lanes_of(0), lanes_of(1)
    qb_parts, qb_ones = lanes_of(2), lanes_of(3)
    eye = jnp.where(lax.broadcasted_iota(jnp.int32, (LANES, LANES), 0)
                    == lax.broadcasted_iota(jnp.int32, (LANES, LANES), 1), 1.0, 0.0).astype(BF16)
    for b in range(nblk):
        rows = slice(b * blk, (b + 1) * blk)
        cps = cps_ref[rows, :]
        q2 = q_ref[rows, :]
        v2 = v_ref[rows, :]
        zero = jnp.zeros_like(q2)
        one = jnp.ones_like(q2)
        kk_ref[rows, 0:LANES] = k_ref[rows, :]
        kk_ref[rows, LANES:2 * LANES] = k_ones - cps * k_parts
        qq_ref[0, rows, 0:LANES] = jnp.where(low, q2, zero)
        qq_ref[0, rows, LANES:2 * LANES] = cps * qa_parts + qa_ones
        qq_ref[1, rows, 0:LANES] = jnp.where(low, zero, q2)
        qq_ref[1, rows, LANES:2 * LANES] = cps * qb_parts + qb_ones
        vt = _dot_nt(eye, v2).astype(BF16)
        vpt_ref[0, 0:HEAD_DIM, rows] = vt[0:HEAD_DIM]
        vpt_ref[1, 0:HEAD_DIM, rows] = vt[HEAD_DIM:LANES]
        vpt_ref[0, HEAD_DIM:VT_ROWS, rows] = jnp.ones((VT_ROWS - HEAD_DIM, blk), BF16)
        vpt_ref[1, HEAD_DIM:VT_ROWS, rows] = jnp.ones((VT_ROWS - HEAD_DIM, blk), BF16)

    causal_t = (lax.broadcasted_iota(jnp.int32, (blk, blk), 0)
                <= lax.broadcasted_iota(jnp.int32, (blk, blk), 1))

    def scores(i, h):
        r0 = i * blk
        rows = slice(r0, r0 + blk)
        qa = qq_ref[h, rows, :]
        sd = jnp.where(causal_t, _dot_nt(kk_ref[rows, :], qa), -jnp.inf)
        m = jnp.max(sd, axis=0, keepdims=True)
        sm = None
        if i > 0:
            sm = _dot_nt(kk_ref[0:r0, :], qa)
            m = jnp.maximum(m, jnp.max(sm, axis=0, keepdims=True))
        return sd, sm, m

    outs = {}

    def weighted_sum(i, h, st):
        sd, sm, m = st
        rows = slice(i * blk, (i + 1) * blk)
        p = jnp.exp(sd - m).astype(BF16)
        if i > 0:
            p = jnp.concatenate([jnp.exp(sm - m).astype(BF16), p], axis=0)
        a = _dot(vpt_ref[h, :, 0:(i + 1) * blk], p)
        outs[i, h] = a[0:HEAD_DIM] / a[HEAD_DIM:HEAD_DIM + 1]
        if h == 1:
            o = jnp.concatenate([outs.pop((i, 0)), outs.pop((i, 1))], axis=0).T
            z = z_ref[rows, :]
            o_ref[rows, :] = (o * (z * _sigmoid(z))).astype(BF16)

    _emit_pipelined([(i, h) for i in range(nblk) for h in range(2)], scores, weighted_sum)


def _fox(pb3, pf3, b_f, blk=256):
    bsz, seq, _ = pb3.shape
    npair = FOX_HEADS // 2
    bias = jnp.zeros((1, LANES), F32).at[:, :MISC_NG].set(
        _misc_ff_layout(b_f.reshape(1, FOX_HEADS), jnp.zeros((1, LANES), F32)))
    tri = jnp.asarray(np.tril(np.ones((blk, blk), np.float32)), BF16)

    def col(base):
        return pl.BlockSpec((None, seq, LANES), lambda b, p: (b, 0, base + p))

    def whole(shape):
        return pl.BlockSpec(shape, lambda b, p: (0,) * len(shape))

    return pl.pallas_call(
        functools.partial(_fox_kernel, seq=seq, blk=blk),
        grid=(bsz, npair),
        in_specs=[
            col(PB_FQ), col(PB_FK), col(PB_FV), col(PF_FZ),
            pl.BlockSpec((None, seq, LANES), lambda b, p: (b, 0, PF_MISC)),
            whole((1, LANES)), whole((blk, blk)),
        ],
        out_specs=pl.BlockSpec((None, seq, LANES), lambda b, p: (b, 0, p)),
        out_shape=jax.ShapeDtypeStruct((bsz, seq, FOX_W), BF16),
        scratch_shapes=[
            pltpu.VMEM((2, seq, 2 * LANES), BF16),
            pltpu.VMEM((seq, 2 * LANES), BF16),
            pltpu.VMEM((2, VT_ROWS, seq), BF16),
            pltpu.VMEM((seq, LANES), BF16),
        ],
        compiler_params=pltpu.CompilerParams(
            dimension_semantics=("arbitrary", "arbitrary"), vmem_limit_bytes=VMEM_LIMIT),
        name="fox_attn",
    )(pb3, pb3, pb3, pf3, pf3, bias, tri)


SB_NEAR_TILES = 2
SB_DEAD = -104.0


def _sb_kernel(q_ref, k_ref, v_ref, z_ref, suf_ref, o_ref, acc_ref, carry_ref, *, seq, blk):
    low = lax.broadcasted_iota(jnp.int32, (blk, LANES), 1) < HEAD_DIM
    strict = (lax.broadcasted_iota(jnp.int32, (blk, blk), 1)
              < lax.broadcasted_iota(jnp.int32, (blk, blk), 0))
    suf = suf_ref[...]

    def log_terms(qh, j_hi, j_lo):
        z = _dot_nt(qh, k_ref[j_lo * blk:(j_hi + 1) * blk, :])
        lsz = jnp.minimum(z, 0.0) - jnp.log(1.0 + jnp.exp(-jnp.abs(z)))
        return lsz, lsz - z

    def tiles(i, j_hi, j_lo, carry, terms):
        lsz, l1m = terms
        k0 = j_lo * blk
        out = None
        for j in range(j_hi, j_lo - 1, -1):
            cs = slice(j * blk - k0, (j + 1) * blk - k0)
            l1 = l1m[:, cs]
            if j == i:
                l1 = jnp.where(strict, l1, 0.0)
            r = _dot(l1.astype(BF16), suf)
            a = jnp.exp(lsz[:, cs] + (r + carry))
            if j == i:
                a = jnp.where(strict, a, 0.0)
            pv = _dot(a.astype(BF16), v_ref[j * blk:(j + 1) * blk, :])
            out = pv if out is None else out + pv
            carry = carry + (r[:, 0:1] + l1[:, 0:1])
        return out, carry

    nblk = seq // blk

    def q_head(i, h):
        q2 = q_ref[i * blk:(i + 1) * blk, :]
        zero = jnp.zeros_like(q2)
        return jnp.where(low, q2, zero) if h == 0 else jnp.where(low, zero, q2)

    def j_near(i):
        return max(i - SB_NEAR_TILES + 1, 0)

    alive = {}

    def near_terms(i, h):
        return log_terms(q_head(i, h), i, j_near(i))

    def near_tiles(i, h, terms):
        out, carry = tiles(i, i, j_near(i), jnp.zeros((blk, 1), F32), terms)
        acc_ref[i, h] = out
        if j_near(i) > 0:
            carry_ref[i, h] = jnp.broadcast_to(carry, (blk, LANES))
            alive[i, h] = jnp.max(carry) >= SB_DEAD

    _emit_pipelined([(i, h) for i in range(nblk) for h in range(2)], near_terms, near_tiles)

    for (i, h), flag in alive.items():
        @pl.when(flag)
        def _(i=i, h=h):
            far, _ = tiles(i, j_near(i) - 1, 0, carry_ref[i, h][:, 0:1],
                           log_terms(q_head(i, h), j_near(i) - 1, 0))
            acc_ref[i, h] = acc_ref[i, h] + far

    for i in range(nblk):
        rows = slice(i * blk, (i + 1) * blk)
        o = jnp.where(low, acc_ref[i, 0], acc_ref[i, 1])
        zg = z_ref[rows, :]
        o_ref[rows, :] = (o * (zg * _sigmoid(zg))).astype(BF16)


def _sb(pb3, pf3, blk=256):
    bsz, seq, _ = pb3.shape
    npair = SB_HEADS // 2
    suf = jnp.asarray(np.tril(np.ones((blk, blk), np.float32), -1), BF16)

    def col(base):
        return pl.BlockSpec((None, seq, LANES), lambda b, p: (b, 0, base + p))

    return pl.pallas_call(
        functools.partial(_sb_kernel, seq=seq, blk=blk),
        grid=(bsz, npair),
        in_specs=[
            col(PB_SQ), col(PB_SK), col(PB_SV), col(PF_SZ),
            pl.BlockSpec(suf.shape, lambda b, p: (0, 0)),
        ],
        out_specs=pl.BlockSpec((None, seq, LANES), lambda b, p: (b, 0, p)),
        out_shape=jax.ShapeDtypeStruct((bsz, seq, SB_W), BF16),
        scratch_shapes=[pltpu.VMEM((seq // blk, 2, blk, LANES), F32),
                        pltpu.VMEM((seq // blk, 2, blk, LANES), F32)],
        compiler_params=pltpu.CompilerParams(
            dimension_semantics=("arbitrary", "arbitrary"), vmem_limit_bytes=VMEM_LIMIT),
        name="sb_attn",
    )(pb3, pb3, pb3, pf3, suf)


NSA_HEAD_ORDER = (0, 1, 2, 3)
SEL_LANES = 32
POS_HI_LANE = 32
POS_LO_LANE = 33
NSA_SLOPES = tuple(2.0 ** (-8.0 * (g + 1) / NSA_HEADS) for g in range(NSA_HEADS))


def _nsa_kernel(q_ref, ksv_ref, kwv_ref, z_ref, misc_ref, kvc_ref,
                pos_ref, w1_ref, w2k_ref, w2v_ref,
                mselt_ref, kaugs_ref, kaugw_ref, qslope_ref,
                o_ref, kc_ref, vct_ref, kks_ref, kkw_ref, vst_ref, vwt_ref, *, seq, blk):
    cols4 = NSA_HEADS * blk
    n_cmp = (seq - CMP_BLOCK) // CMP_STRIDE + 1
    n_sel = seq // SEL_BLOCK
    nwin = WINDOW // blk
    eye = jnp.where(lax.broadcasted_iota(jnp.int32, (LANES, LANES), 0)
                    == lax.broadcasted_iota(jnp.int32, (LANES, LANES), 1), 1.0, 0.0).astype(BF16)

    nchunk = seq // CMP_STRIDE
    ha = jnp.zeros((nchunk, LANES), F32)
    hb = jnp.zeros((nchunk, LANES), F32)
    for p in range(CMP_STRIDE):
        xp = kvc_ref[pl.ds(p, nchunk, stride=CMP_STRIDE), :]
        q = p + CMP_STRIDE
        ha = ha + _dot((xp + pos_ref[p:p + 1, :]).astype(BF16), w1_ref[p])
        hb = hb + _dot((xp + pos_ref[q:q + 1, :]).astype(BF16), w1_ref[q])
    hid = ha + pltpu.roll(hb, nchunk - 1, axis=0)
    hid = (hid * _sigmoid(hid)).astype(BF16)
    kc_ref[...] = _dot(hid, w2k_ref[...]).astype(BF16)
    vct_ref[...] = _dot_nt(eye, _dot(hid, w2v_ref[...]).astype(BF16)).astype(BF16)

    tb = 2 * LANES
    lane_t = lax.broadcasted_iota(jnp.int32, (tb, LANES), 1)
    low_t = lane_t < HEAD_DIM
    er = lax.broadcasted_iota(jnp.int32, (LANES, LANES), 0)
    ec = lax.broadcasted_iota(jnp.int32, (LANES, LANES), 1)
    dup_lo = jnp.where(er == ec % HEAD_DIM, 1.0, 0.0).astype(BF16)
    swap = jnp.where(ec == (er + HEAD_DIM) % LANES, 1.0, 0.0).astype(BF16)
    kks_ref[:, LANES:2 * LANES] = kaugs_ref[...]
    kkw_ref[:, LANES:2 * LANES] = kaugw_ref[...]
    for b in range(seq // tb):
        rows = slice(b * tb, (b + 1) * tb)
        xs = ksv_ref[rows, :]
        xw = kwv_ref[rows, :]
        one = jnp.ones_like(xs)
        kks_ref[rows, 0:LANES] = _dot(xs, dup_lo).astype(BF16)
        kkw_ref[rows, 0:LANES] = _dot(xw, dup_lo).astype(BF16)
        vst_ref[:, rows] = _dot_nt(swap, jnp.where(low_t, one, xs)).astype(BF16)[0:VT_ROWS]
        vwt_ref[:, rows] = _dot_nt(swap, jnp.where(low_t, one, xw)).astype(BF16)[0:VT_ROWS]

    low = lax.broadcasted_iota(jnp.int32, (blk, LANES), 1) < HEAD_DIM
    colt = lax.broadcasted_iota(jnp.int32, (1, cols4), 1)
    grp = colt // blk
    tok = colt - grp * blk
    gslope = [NSA_SLOPES[g] for g in NSA_HEAD_ORDER]
    slope = jnp.where(grp == 0, gslope[0], jnp.where(grp == 1, gslope[1],
                      jnp.where(grp == 2, gslope[2], gslope[3]))).astype(F32)
    keyr = lax.broadcasted_iota(jnp.int32, (blk, 1), 0)
    causal_t = keyr <= tok
    after_t = keyr > tok
    cmpr = lax.broadcasted_iota(jnp.int32, (LANES, 1), 0)
    cmp_end = (cmpr * CMP_STRIDE + (CMP_BLOCK - 1)).astype(F32)
    jrow = lax.broadcasted_iota(jnp.int32, (n_sel, blk), 0)
    tcol = lax.broadcasted_iota(jnp.int32, (n_sel, blk), 1)
    sel_lane = lax.broadcasted_iota(jnp.int32, (cols4, LANES), 1) < SEL_LANES
    qslope = qslope_ref[...]

    def col_max(parts):
        m = None
        for s in parts:
            mp = jnp.max(s, axis=0, keepdims=True)
            m = mp if m is None else jnp.maximum(m, mp)
        return m

    def softmax_pv(parts, m, keys, vt_ref):
        p = [jnp.exp(s - m).astype(BF16) for s in parts]
        p = p[0] if len(p) == 1 else jnp.concatenate(p, axis=0)
        acc = _dot(vt_ref[:, keys], p)
        return acc[0:HEAD_DIM] / acc[HEAD_DIM:HEAD_DIM + 1]

    def scores(i):
        qs = i * blk
        rows = slice(qs, qs + blk)
        q01 = q_ref[rows, 0:LANES]
        q23 = q_ref[rows, LANES:2 * LANES]
        zero = jnp.zeros_like(q01)
        qst = jnp.concatenate([jnp.where(low, q01, zero), jnp.where(low, zero, q01),
                               jnp.where(low, q23, zero), jnp.where(low, zero, q23)], axis=0)

        dist_c = (qs + tok).astype(F32) - cmp_end
        valid_c = (dist_c >= 0.0) & (cmpr < n_cmp)
        sc = _dot_nt(kc_ref[...], qst) - slope * dist_c
        sc = jnp.where(valid_c, sc, -jnp.inf)
        mc = jnp.max(sc, axis=0, keepdims=True)
        mc = jnp.where(mc == -jnp.inf, 0.0, mc)
        pc = jnp.exp(sc - mc)
        ssum = jnp.sum(pc, axis=0, keepdims=True)
        pc = pc / jnp.where(ssum > 0.0, ssum, 1.0)
        oc = _dot(vct_ref[0:HEAD_DIM, :], pc.astype(BF16))

        pcs = (pc[:, 0:blk] + pc[:, blk:2 * blk] + pc[:, 2 * blk:3 * blk]
               + pc[:, 3 * blk:4 * blk])
        hi, lo = _split2(pcs)
        imp = _dot(mselt_ref[...], jnp.concatenate([hi, lo], axis=0))
        back = (qs + tcol) // SEL_BLOCK - jrow
        imp = jnp.where(back < 0, -jnp.inf,
                        jnp.where(back < SEL_N_LOCAL, jnp.inf,
                                  jnp.where(jrow == 0, jnp.inf, imp)))
        rank = jnp.zeros((n_sel, blk), F32)
        for c in range(n_sel):
            rowc = imp[c:c + 1, :]
            tie = jnp.where(jrow > c, 1.0, 0.0)
            rank = rank + jnp.where(rowc > imp, 1.0, jnp.where(rowc == imp, tie, 0.0))
        unsel_t = jnp.where(rank >= float(SEL_TOPK), 1.0, 0.0).astype(BF16)
        unsel_t = jnp.concatenate([unsel_t, jnp.zeros((LANES - n_sel, blk), BF16)], axis=0)
        unsel = _dot_nt(eye, unsel_t).astype(BF16)

        qop_w = jnp.concatenate([qst, qslope], axis=1)
        qaug = jnp.where(sel_lane, jnp.concatenate([unsel] * NSA_HEADS, axis=0), qslope)
        qop_s = jnp.concatenate([qst, qaug], axis=1)

        win = []
        w0 = max(qs - WINDOW, 0)
        if i >= nwin:
            win.append(jnp.where(after_t, _dot_nt(kkw_ref[w0:w0 + blk, :], qop_w), -jnp.inf))
        wm = max(qs - WINDOW + blk, 0)
        if qs > wm:
            win.append(_dot_nt(kkw_ref[wm:qs, :], qop_w))
        win.append(jnp.where(causal_t, _dot_nt(kkw_ref[rows, :], qop_w), -jnp.inf))

        sel = []
        if i > 0:
            sel.append(_dot_nt(kks_ref[0:qs, :], qop_s))
        sel.append(jnp.where(causal_t, _dot_nt(kks_ref[rows, :], qop_s), -jnp.inf))
        return oc, win, col_max(win), sel, col_max(sel)

    def outputs(i, st):
        oc, win, m_win, sel, m_sel = st
        qs = i * blk
        rows = slice(qs, qs + blk)
        o_win = softmax_pv(win, m_win, slice(max(qs - WINDOW, 0), qs + blk), vwt_ref)
        o_sel = softmax_pv(sel, m_sel, slice(0, qs + blk), vst_ref)

        sg_t = _sigmoid(misc_ref[rows, :]).T

        def gate(branch):
            return jnp.concatenate(
                [sg_t[MISC_NG + 3 * g + branch:MISC_NG + 3 * g + branch + 1, :]
                 for g in NSA_HEAD_ORDER], axis=1)

        tot = oc * gate(0) + o_sel * gate(1) + o_win * gate(2)
        by_head = {g: tot[:, k * blk:(k + 1) * blk] for k, g in enumerate(NSA_HEAD_ORDER)}
        out = jnp.concatenate([by_head[g] for g in range(NSA_HEADS)], axis=0).T
        z = z_ref[rows, :]
        o_ref[rows, :] = (out * (z * _sigmoid(z))).astype(BF16)

    _emit_pipelined([(i,) for i in range(seq // blk)], scores, outputs, depth=1)


def _nsa_constants(seq, blk):
    n_cmp = (seq - CMP_BLOCK) // CMP_STRIDE + 1
    n_sel = seq // SEL_BLOCK
    cs = np.arange(n_cmp) * CMP_STRIDE
    ce = cs + CMP_BLOCK - 1
    ss = np.arange(n_sel) * SEL_BLOCK
    se = ss + SEL_BLOCK - 1
    msel_t = np.zeros((n_sel, LANES), np.float32)
    msel_t[:, :n_cmp] = ((cs[:, None] <= se[None, :]) & (ce[:, None] >= ss[None, :])).T
    msel_t = np.concatenate([msel_t, msel_t], axis=1)
    key = np.arange(seq)
    kaug_w = np.zeros((seq, LANES), np.float32)
    kaug_w[:, POS_HI_LANE] = key // 16
    kaug_w[:, POS_LO_LANE] = key % 16
    kaug_s = kaug_w.copy()
    kaug_s[key, key // SEL_BLOCK] = PEN
    qslope = np.zeros((NSA_HEADS * blk, LANES), np.float32)
    for k, g in enumerate(NSA_HEAD_ORDER):
        qslope[k * blk:(k + 1) * blk, POS_HI_LANE] = 16.0 * NSA_SLOPES[g]
        qslope[k * blk:(k + 1) * blk, POS_LO_LANE] = NSA_SLOPES[g]
    return (jnp.asarray(msel_t, BF16), jnp.asarray(kaug_s, BF16), jnp.asarray(kaug_w, BF16),
            jnp.asarray(qslope, BF16))


def _nsa(pb3, pf3, pos_k, w1_k, w2_k, pos_v, w1_v, w2_v, blk=128):
    bsz, seq, _ = pb3.shape
    nchunk = seq // CMP_STRIDE
    w1k = w1_k.astype(BF16).reshape(CMP_BLOCK, HEAD_DIM, HEAD_DIM)
    w1v = w1_v.astype(BF16).reshape(CMP_BLOCK, HEAD_DIM, HEAD_DIM)
    w1z = jnp.zeros_like(w1k)
    w1 = jnp.concatenate([jnp.concatenate([w1k, w1z], axis=2),
                          jnp.concatenate([w1z, w1v], axis=2)], axis=1)
    pos = jnp.concatenate([pos_k, pos_v], axis=1)
    zero = jnp.zeros((LANES, LANES), F32)
    w2k = zero.at[:HEAD_DIM, :].set(jnp.concatenate([w2_k, w2_k], axis=1))
    w2v = zero.at[HEAD_DIM:, :].set(jnp.concatenate([w2_v, w2_v], axis=1))
    consts = _nsa_constants(seq, blk)

    def col(base, nblk=1):
        return pl.BlockSpec((None, seq, nblk * LANES), lambda b: (b, 0, base // nblk))

    def whole(shape):
        return pl.BlockSpec(shape, lambda b: (0,) * len(shape))

    return pl.pallas_call(
        functools.partial(_nsa_kernel, seq=seq, blk=blk),
        grid=(bsz,),
        in_specs=[
            col(PB_NQ, 2), col(PB_NSEL), col(PB_NWIN),
            col(PF_NZ, 2), col(PF_MISC), col(PF_KVC),
            whole(pos.shape), whole(w1.shape), whole(w2k.shape), whole(w2v.shape),
        ] + [whole(c.shape) for c in consts],
        out_specs=pl.BlockSpec((None, seq, NSA_W), lambda b: (b, 0, 0)),
        out_shape=jax.ShapeDtypeStruct((bsz, seq, NSA_W), BF16),
        scratch_shapes=[
            pltpu.VMEM((nchunk, LANES), BF16),
            pltpu.VMEM((LANES, nchunk), BF16),
            pltpu.VMEM((seq, 2 * LANES), BF16),
            pltpu.VMEM((seq, 2 * LANES), BF16),
            pltpu.VMEM((VT_ROWS, seq), BF16),
            pltpu.VMEM((VT_ROWS, seq), BF16),
        ],
        compiler_params=pltpu.CompilerParams(
            dimension_semantics=("arbitrary",), vmem_limit_bytes=VMEM_LIMIT),
        name="nsa_attn",
    )(pb3, pb3, pb3, pf3, pf3, pf3, pos, w1, w2k.astype(BF16), w2v.astype(BF16), *consts)


def kernel(x, norm_g, w_in, b_f, cmp_pos_k, cmp_w1_k, cmp_w2_k,
           cmp_pos_v, cmp_w1_v, cmp_w2_v, w_out, final_g):
    bsz, seq, d = x.shape
    xf = x.reshape(bsz * seq, d)
    attn = None
    for l in range(DEPTH):
        res = _proj(xf, norm_g[l], attn, None if l == 0 else w_out[l - 1].astype(BF16),
                    _pack_w_in(w_in[l]))
        if l > 0:
            xf = res[0]
        pb, pf = res[-2:]
        pb3 = pb.reshape(bsz, seq, NB_COLS)
        pf3 = pf.reshape(bsz, seq, NF_COLS)
        o_fox = _fox(pb3, pf3, b_f[l])
        o_sb = _sb(pb3, pf3)
        o_nsa = _nsa(pb3, pf3, cmp_pos_k[l], cmp_w1_k[l], cmp_w2_k[l],
                     cmp_pos_v[l], cmp_w1_v[l], cmp_w2_v[l])
        attn = (o_fox.reshape(bsz * seq, FOX_W), o_sb.reshape(bsz * seq, SB_W),
                o_nsa.reshape(bsz * seq, NSA_W))
    out, = _proj(xf, final_g, attn, w_out[DEPTH - 1].astype(BF16))
    return out.reshape(bsz, seq, d)
```

```python
import functools

import numpy as np
import jax
import jax.numpy as jnp
from jax import lax
from jax.experimental import pallas as pl
from jax.experimental.pallas import tpu as pltpu

F32 = jnp.float32
BF16 = jnp.bfloat16

DEPTH = 2
HEAD_DIM = 64
LANES = 128
VT_ROWS = HEAD_DIM + 16
FOX_HEADS = 6
SB_HEADS = 6
NSA_HEADS = 4
FOX_W = FOX_HEADS * HEAD_DIM
SB_W = SB_HEADS * HEAD_DIM
NSA_W = NSA_HEADS * HEAD_DIM
CMP_BLOCK = 32
CMP_STRIDE = 16
SEL_BLOCK = 64
SEL_TOPK = 8
SEL_N_LOCAL = 2
WINDOW = 512
NORM_EPS = 1e-6
QK_SCALE = HEAD_DIM ** -0.5
PEN = -(2.0 ** 100)

PB_FQ, PB_FK, PB_FV, PB_SQ, PB_SK, PB_SV, PB_NQ, PB_NSEL, PB_NWIN = (
    0, 3, 6, 9, 12, 15, 18, 20, 21)
PB_BLOCKS = 22
PF_FZ, PF_SZ, PF_NZ, PF_KVC, PF_MISC = 0, 3, 6, 8, 9
PF_BLOCKS = 10
NB_COLS = PB_BLOCKS * LANES
NF_COLS = PF_BLOCKS * LANES
MISC_PAIR_LANES = 16
MISC_FF_COPIES = 6
MISC_NG = MISC_PAIR_LANES * (FOX_HEADS // 2)

VMEM_PHYSICAL = 64 * 1024 * 1024
VMEM_LIMIT = VMEM_PHYSICAL - 4 * 1024 * 1024

_NT = (((1,), (1,)), ((), ()))


def _dot(a, b):
    return jnp.dot(a, b, preferred_element_type=F32)


def _dot_nt(a, b):
    return lax.dot_general(a, b, _NT, preferred_element_type=F32)


def _sigmoid(x):
    return 1.0 / (1.0 + jnp.exp(-x))


def _log_sigmoid(x):
    return -(jnp.maximum(-x, 0.0) + jnp.log1p(jnp.exp(-jnp.abs(x))))


def _split3(x):
    hi = x.astype(BF16)
    r = x - hi.astype(F32)
    mid = r.astype(BF16)
    lo = (r - mid.astype(F32)).astype(BF16)
    return hi, mid, lo


def _split2(x):
    hi = x.astype(BF16)
    lo = (x - hi.astype(F32)).astype(BF16)
    return hi, lo


PIPELINE_DEPTH = 4


def _emit_pipelined(items, first, second, depth=PIPELINE_DEPTH):
    pending = {}
    for n, item in enumerate(items):
        pending[item] = first(*item)
        if n >= depth:
            prev = items[n - depth]
            second(*prev, pending.pop(prev))
    for prev in items[max(len(items) - depth, 0):]:
        second(*prev, pending.pop(prev))


def _proj_kernel(*refs, has_out, has_in):
    it = iter(refs)
    y = next(it)[...]
    if has_out:
        of_ref, os_ref, on_ref, wo_ref = next(it), next(it), next(it), next(it)
        y = (y + _dot(of_ref[...], wo_ref[0:FOX_W, :])
             + _dot(os_ref[...], wo_ref[FOX_W:FOX_W + SB_W, :])
             + _dot(on_ref[...], wo_ref[FOX_W + SB_W:, :]))
    g_ref = next(it)
    ms = jnp.mean(y * y, axis=-1, keepdims=True)
    h = y * lax.rsqrt(ms + NORM_EPS) * g_ref[...]
    if not has_in:
        next(it)[...] = h
        return
    wi_ref = next(it)
    if has_out:
        next(it)[...] = y
    pb_ref, pf_ref = next(it), next(it)
    h = h.astype(BF16)
    chunk = 4 * LANES
    for c in range(0, NB_COLS, chunk):
        e = min(c + chunk, NB_COLS)
        pb_ref[:, c:e] = _dot(h, wi_ref[:, c:e]).astype(BF16)
    for c in range(0, NF_COLS, chunk):
        e = min(c + chunk, NF_COLS)
        pf_ref[:, c:e] = _dot(h, wi_ref[:, NB_COLS + c:NB_COLS + e])


def _proj(xf, gain, attn=None, w_out=None, w_in=None):
    m, d = xf.shape
    has_out, has_in = attn is not None, w_in is not None
    tm = 512 if has_in else 2048

    def rows(width):
        return pl.BlockSpec((tm, width), lambda i: (i, 0))

    def whole(shape):
        return pl.BlockSpec(shape, lambda i: (0, 0))

    args, in_specs, out_specs, out_shape = [xf], [rows(d)], [], []
    if has_out:
        args += [*attn, w_out]
        in_specs += [rows(FOX_W), rows(SB_W), rows(NSA_W), whole((d, d))]
    args.append(gain.reshape(1, d))
    in_specs.append(whole((1, d)))
    if has_in:
        args.append(w_in)
        in_specs.append(whole(w_in.shape))
    if has_out or not has_in:
        out_specs.append(rows(d))
        out_shape.append(jax.ShapeDtypeStruct((m, d), F32))
    if has_in:
        out_specs += [rows(NB_COLS), rows(NF_COLS)]
        out_shape += [jax.ShapeDtypeStruct((m, NB_COLS), BF16),
                      jax.ShapeDtypeStruct((m, NF_COLS), F32)]
    return pl.pallas_call(
        functools.partial(_proj_kernel, has_out=has_out, has_in=has_in),
        grid=(m // tm,),
        in_specs=in_specs,
        out_specs=out_specs,
        out_shape=out_shape,
        compiler_params=pltpu.CompilerParams(
            dimension_semantics=("arbitrary",), vmem_limit_bytes=VMEM_LIMIT),
        name="proj_" + ("out" if has_out else "") + ("in" if has_in else "norm"),
    )(*args)


def _pack_w_in(w):
    sizes = (FOX_W, FOX_W, FOX_W, FOX_HEADS, FOX_W, SB_W, SB_W, SB_W, SB_W,
             NSA_W, HEAD_DIM, HEAD_DIM, HEAD_DIM, HEAD_DIM, HEAD_DIM, HEAD_DIM,
             3 * NSA_HEADS, NSA_W)
    offs = np.concatenate([[0], np.cumsum(sizes)])
    w = w.astype(BF16)
    (fq, fk, fv, ff, fz, sq, sk, sv, sz, nq, nkc, nvc, nks, nvs, nkw, nvw, ng, nz) = [
        w[:, offs[i]:offs[i + 1]] for i in range(len(sizes))]
    zeros = jnp.zeros((w.shape[0], LANES), w.dtype)
    cols = [fq * QK_SCALE, fk, fv, sq * QK_SCALE, sk, sv, nq * QK_SCALE,
            nks, nvs, nkw, nvw,
            fz, sz, nz, nkc, nvc, _misc_ff_layout(ff, zeros), ng,
            zeros[:, :LANES - MISC_NG - 3 * NSA_HEADS]]
    return jnp.concatenate(cols, axis=1)


def _misc_ff_layout(ff, zeros):
    cols = []
    for p in range(FOX_HEADS // 2):
        cols += [ff[:, 2 * p:2 * p + 1]] * MISC_FF_COPIES + [ff[:, 2 * p + 1:2 * p + 2]] * MISC_FF_COPIES
        cols.append(zeros[:, :MISC_PAIR_LANES - 2 * MISC_FF_COPIES])
    return jnp.concatenate(cols, axis=1)


def _fox_kernel(q_ref, k_ref, v_ref, z_ref, misc_ref, bf_ref, tri_ref,
                o_ref, qq_ref, kk_ref, vpt_ref, cps_ref, *, seq, blk):
    pair = pl.program_id(1)
    low = lax.broadcasted_iota(jnp.int32, (blk, LANES), 1) < HEAD_DIM
    lane_b = lax.broadcasted_iota(jnp.int32, (blk, LANES), 1)
    nblk = seq // blk

    @pl.when(pair == 0)
    def _():
        tri = tri_ref[...]
        part = (lane_b % MISC_PAIR_LANES) % 3
        carry = jnp.zeros((1, LANES), F32)
        for b in range(nblk):
            rows = slice(b * blk, (b + 1) * blk)
            ls = _log_sigmoid(misc_ref[rows, :] + bf_ref[...])
            c3 = _dot(tri, jnp.concatenate(_split3(ls), axis=1))
            cb = c3[:, 0:LANES] + c3[:, LANES:2 * LANES] + c3[:, 2 * LANES:3 * LANES] + carry
            carry = cb[blk - 1:blk, :]
            hi, mid, lo = _split3(cb)
            cps_ref[rows, :] = jnp.where(part == 0, hi.astype(F32), jnp.where(
                part == 1, mid.astype(F32), lo.astype(F32))).astype(BF16)

    rel = lane_b - MISC_PAIR_LANES * pair
    third = jnp.where(rel < 0, -1, jnp.where(rel < 3, 0, jnp.where(rel < 6, 1, jnp.where(
        rel < 9, 2, jnp.where(rel < 12, 3, -1)))))

    def lanes_of(*thirds):
        sel = jnp.zeros((blk, LANES), F32)
        for t in thirds:
            sel = jnp.where(third == t, 1.0, sel)
        return sel.astype(BF16)

    k_ones, k_parts = lanes_of(0, 2), lanes_of(1, 3)
    qa_parts, qa_ones = lanes_of(0), lanes_of(1)
    qb_parts, qb_ones = lanes_of(2), lanes_of(3)
    eye = jnp.where(lax.broadcasted_iota(jnp.int32, (LANES, LANES), 0)
                    == lax.broadcasted_iota(jnp.int32, (LANES, LANES), 1), 1.0, 0.0).astype(BF16)
    for b in range(nblk):
        rows = slice(b * blk, (b + 1) * blk)
        cps = cps_ref[rows, :]
        q2 = q_ref[rows, :]
        v2 = v_ref[rows, :]
        zero = jnp.zeros_like(q2)
        one = jnp.ones_like(q2)
        kk_ref[rows, 0:LANES] = k_ref[rows, :]
        kk_ref[rows, LANES:2 * LANES] = k_ones - cps * k_parts
        qq_ref[0, rows, 0:LANES] = jnp.where(low, q2, zero)
        qq_ref[0, rows, LANES:2 * LANES] = cps * qa_parts + qa_ones
        qq_ref[1, rows, 0:LANES] = jnp.where(low, zero, q2)
        qq_ref[1, rows, LANES:2 * LANES] = cps * qb_parts + qb_ones
        vt = _dot_nt(eye, v2).astype(BF16)
        vpt_ref[0, 0:HEAD_DIM, rows] = vt[0:HEAD_DIM]
        vpt_ref[1, 0:HEAD_DIM, rows] = vt[HEAD_DIM:LANES]
        vpt_ref[0, HEAD_DIM:VT_ROWS, rows] = jnp.ones((VT_ROWS - HEAD_DIM, blk), BF16)
        vpt_ref[1, HEAD_DIM:VT_ROWS, rows] = jnp.ones((VT_ROWS - HEAD_DIM, blk), BF16)

    causal_t = (lax.broadcasted_iota(jnp.int32, (blk, blk), 0)
                <= lax.broadcasted_iota(jnp.int32, (blk, blk), 1))

    def scores(i, h):
        r0 = i * blk
        rows = slice(r0, r0 + blk)
        qa = qq_ref[h, rows, :]
        sd = jnp.where(causal_t, _dot_nt(kk_ref[rows, :], qa), -jnp.inf)
        m = jnp.max(sd, axis=0, keepdims=True)
        sm = None
        if i > 0:
            sm = _dot_nt(kk_ref[0:r0, :], qa)
            m = jnp.maximum(m, jnp.max(sm, axis=0, keepdims=True))
        return sd, sm, m

    outs = {}

    def weighted_sum(i, h, st):
        sd, sm, m = st
        rows = slice(i * blk, (i + 1) * blk)
        p = jnp.exp(sd - m).astype(BF16)
        if i > 0:
            p = jnp.concatenate([jnp.exp(sm - m).astype(BF16), p], axis=0)
        a = _dot(vpt_ref[h, :, 0:(i + 1) * blk], p)
        outs[i, h] = a[0:HEAD_DIM] / a[HEAD_DIM:HEAD_DIM + 1]
        if h == 1:
            o = jnp.concatenate([outs.pop((i, 0)), outs.pop((i, 1))], axis=0).T
            z = z_ref[rows, :]
            o_ref[rows, :] = (o * (z * _sigmoid(z))).astype(BF16)

    _emit_pipelined([(i, h) for i in range(nblk) for h in range(2)], scores, weighted_sum)


def _fox(pb3, pf3, b_f, blk=256):
    bsz, seq, _ = pb3.shape
    npair = FOX_HEADS // 2
    bias = jnp.zeros((1, LANES), F32).at[:, :MISC_NG].set(
        _misc_ff_layout(b_f.reshape(1, FOX_HEADS), jnp.zeros((1, LANES), F32)))
    tri = jnp.asarray(np.tril(np.ones((blk, blk), np.float32)), BF16)

    def col(base):
        return pl.BlockSpec((None, seq, LANES), lambda b, p: (b, 0, base + p))

    def whole(shape):
        return pl.BlockSpec(shape, lambda b, p: (0,) * len(shape))

    return pl.pallas_call(
        functools.partial(_fox_kernel, seq=seq, blk=blk),
        grid=(bsz, npair),
        in_specs=[
            col(PB_FQ), col(PB_FK), col(PB_FV), col(PF_FZ),
            pl.BlockSpec((None, seq, LANES), lambda b, p: (b, 0, PF_MISC)),
            whole((1, LANES)), whole((blk, blk)),
        ],
        out_specs=pl.BlockSpec((None, seq, LANES), lambda b, p: (b, 0, p)),
        out_shape=jax.ShapeDtypeStruct((bsz, seq, FOX_W), BF16),
        scratch_shapes=[
            pltpu.VMEM((2, seq, 2 * LANES), BF16),
            pltpu.VMEM((seq, 2 * LANES), BF16),
            pltpu.VMEM((2, VT_ROWS, seq), BF16),
            pltpu.VMEM((seq, LANES), BF16),
        ],
        compiler_params=pltpu.CompilerParams(
            dimension_semantics=("arbitrary", "arbitrary"), vmem_limit_bytes=VMEM_LIMIT),
        name="fox_attn",
    )(pb3, pb3, pb3, pf3, pf3, bias, tri)


SB_NEAR_TILES = 2
SB_DEAD = -104.0


def _sb_kernel(q_ref, k_ref, v_ref, z_ref, suf_ref, o_ref, acc_ref, carry_ref, *, seq, blk):
    low = lax.broadcasted_iota(jnp.int32, (blk, LANES), 1) < HEAD_DIM
    strict = (lax.broadcasted_iota(jnp.int32, (blk, blk), 1)
              < lax.broadcasted_iota(jnp.int32, (blk, blk), 0))
    suf = suf_ref[...]

    def log_terms(qh, j_hi, j_lo):
        z = _dot_nt(qh, k_ref[j_lo * blk:(j_hi + 1) * blk, :])
        lsz = jnp.minimum(z, 0.0) - jnp.log(1.0 + jnp.exp(-jnp.abs(z)))
        return lsz, lsz - z

    def tiles(i, j_hi, j_lo, carry, terms):
        lsz, l1m = terms
        k0 = j_lo * blk
        out = None
        for j in range(j_hi, j_lo - 1, -1):
            cs = slice(j * blk - k0, (j + 1) * blk - k0)
            l1 = l1m[:, cs]
            if j == i:
                l1 = jnp.where(strict, l1, 0.0)
            r = _dot(l1.astype(BF16), suf)
            a = jnp.exp(lsz[:, cs] + (r + carry))
            if j == i:
                a = jnp.where(strict, a, 0.0)
            pv = _dot(a.astype(BF16), v_ref[j * blk:(j + 1) * blk, :])
            out = pv if out is None else out + pv
            carry = carry + (r[:, 0:1] + l1[:, 0:1])
        return out, carry

    nblk = seq // blk

    def q_head(i, h):
        q2 = q_ref[i * blk:(i + 1) * blk, :]
        zero = jnp.zeros_like(q2)
        return jnp.where(low, q2, zero) if h == 0 else jnp.where(low, zero, q2)

    def j_near(i):
        return max(i - SB_NEAR_TILES + 1, 0)

    alive = {}

    def near_terms(i, h):
        return log_terms(q_head(i, h), i, j_near(i))

    def near_tiles(i, h, terms):
        out, carry = tiles(i, i, j_near(i), jnp.zeros((blk, 1), F32), terms)
        acc_ref[i, h] = out
        if j_near(i) > 0:
            carry_ref[i, h] = jnp.broadcast_to(carry, (blk, LANES))
            alive[i, h] = jnp.max(carry) >= SB_DEAD

    _emit_pipelined([(i, h) for i in range(nblk) for h in range(2)], near_terms, near_tiles)

    for (i, h), flag in alive.items():
        @pl.when(flag)
        def _(i=i, h=h):
            far, _ = tiles(i, j_near(i) - 1, 0, carry_ref[i, h][:, 0:1],
                           log_terms(q_head(i, h), j_near(i) - 1, 0))
            acc_ref[i, h] = acc_ref[i, h] + far

    for i in range(nblk):
        rows = slice(i * blk, (i + 1) * blk)
        o = jnp.where(low, acc_ref[i, 0], acc_ref[i, 1])
        zg = z_ref[rows, :]
        o_ref[rows, :] = (o * (zg * _sigmoid(zg))).astype(BF16)


def _sb(pb3, pf3, blk=256):
    bsz, seq, _ = pb3.shape
    npair = SB_HEADS // 2
    suf = jnp.asarray(np.tril(np.ones((blk, blk), np.float32), -1), BF16)

    def col(base):
        return pl.BlockSpec((None, seq, LANES), lambda b, p: (b, 0, base + p))

    return pl.pallas_call(
        functools.partial(_sb_kernel, seq=seq, blk=blk),
        grid=(bsz, npair),
        in_specs=[
            col(PB_SQ), col(PB_SK), col(PB_SV), col(PF_SZ),
            pl.BlockSpec(suf.shape, lambda b, p: (0, 0)),
        ],
        out_specs=pl.BlockSpec((None, seq, LANES), lambda b, p: (b, 0, p)),
        out_shape=jax.ShapeDtypeStruct((bsz, seq, SB_W), BF16),
        scratch_shapes=[pltpu.VMEM((seq // blk, 2, blk, LANES), F32),
                        pltpu.VMEM((seq // blk, 2, blk, LANES), F32)],
        compiler_params=pltpu.CompilerParams(
            dimension_semantics=("arbitrary", "arbitrary"), vmem_limit_bytes=VMEM_LIMIT),
        name="sb_attn",
    )(pb3, pb3, pb3, pf3, suf)


NSA_HEAD_ORDER = (0, 1, 2, 3)
SEL_LANES = 32
POS_HI_LANE = 32
POS_LO_LANE = 33
NSA_SLOPES = tuple(2.0 ** (-8.0 * (g + 1) / NSA_HEADS) for g in range(NSA_HEADS))


def _nsa_kernel(q_ref, ksv_ref, kwv_ref, z_ref, misc_ref, kvc_ref,
                pos_ref, w1_ref, w2k_ref, w2v_ref,
                mselt_ref, kaugs_ref, kaugw_ref, qslope_ref,
                o_ref, kc_ref, vct_ref, kks_ref, kkw_ref, vst_ref, vwt_ref, *, seq, blk):
    cols4 = NSA_HEADS * blk
    n_cmp = (seq - CMP_BLOCK) // CMP_STRIDE + 1
    n_sel = seq // SEL_BLOCK
    nwin = WINDOW // blk
    eye = jnp.where(lax.broadcasted_iota(jnp.int32, (LANES, LANES), 0)
                    == lax.broadcasted_iota(jnp.int32, (LANES, LANES), 1), 1.0, 0.0).astype(BF16)

    nchunk = seq // CMP_STRIDE
    ha = jnp.zeros((nchunk, LANES), F32)
    hb = jnp.zeros((nchunk, LANES), F32)
    for p in range(CMP_STRIDE):
        xp = kvc_ref[pl.ds(p, nchunk, stride=CMP_STRIDE), :]
        q = p + CMP_STRIDE
        ha = ha + _dot((xp + pos_ref[p:p + 1, :]).astype(BF16), w1_ref[p])
        hb = hb + _dot((xp + pos_ref[q:q + 1, :]).astype(BF16), w1_ref[q])
    hid = ha + pltpu.roll(hb, nchunk - 1, axis=0)
    hid = (hid * _sigmoid(hid)).astype(BF16)
    kc_ref[...] = _dot(hid, w2k_ref[...]).astype(BF16)
    vct_ref[...] = _dot_nt(eye, _dot(hid, w2v_ref[...]).astype(BF16)).astype(BF16)

    tb = 2 * LANES
    lane_t = lax.broadcasted_iota(jnp.int32, (tb, LANES), 1)
    low_t = lane_t < HEAD_DIM
    er = lax.broadcasted_iota(jnp.int32, (LANES, LANES), 0)
    ec = lax.broadcasted_iota(jnp.int32, (LANES, LANES), 1)
    dup_lo = jnp.where(er == ec % HEAD_DIM, 1.0, 0.0).astype(BF16)
    swap = jnp.where(ec == (er + HEAD_DIM) % LANES, 1.0, 0.0).astype(BF16)
    kks_ref[:, LANES:2 * LANES] = kaugs_ref[...]
    kkw_ref[:, LANES:2 * LANES] = kaugw_ref[...]
    for b in range(seq // tb):
        rows = slice(b * tb, (b + 1) * tb)
        xs = ksv_ref[rows, :]
        xw = kwv_ref[rows, :]
        one = jnp.ones_like(xs)
        kks_ref[rows, 0:LANES] = _dot(xs, dup_lo).astype(BF16)
        kkw_ref[rows, 0:LANES] = _dot(xw, dup_lo).astype(BF16)
        vst_ref[:, rows] = _dot_nt(swap, jnp.where(low_t, one, xs)).astype(BF16)[0:VT_ROWS]
        vwt_ref[:, rows] = _dot_nt(swap, jnp.where(low_t, one, xw)).astype(BF16)[0:VT_ROWS]

    low = lax.broadcasted_iota(jnp.int32, (blk, LANES), 1) < HEAD_DIM
    colt = lax.broadcasted_iota(jnp.int32, (1, cols4), 1)
    grp = colt // blk
    tok = colt - grp * blk
    gslope = [NSA_SLOPES[g] for g in NSA_HEAD_ORDER]
    slope = jnp.where(grp == 0, gslope[0], jnp.where(grp == 1, gslope[1],
                      jnp.where(grp == 2, gslope[2], gslope[3]))).astype(F32)
    keyr = lax.broadcasted_iota(jnp.int32, (blk, 1), 0)
    causal_t = keyr <= tok
    after_t = keyr > tok
    cmpr = lax.broadcasted_iota(jnp.int32, (LANES, 1), 0)
    cmp_end = (cmpr * CMP_STRIDE + (CMP_BLOCK - 1)).astype(F32)
    jrow = lax.broadcasted_iota(jnp.int32, (n_sel, blk), 0)
    tcol = lax.broadcasted_iota(jnp.int32, (n_sel, blk), 1)
    sel_lane = lax.broadcasted_iota(jnp.int32, (cols4, LANES), 1) < SEL_LANES
    qslope = qslope_ref[...]

    def col_max(parts):
        m = None
        for s in parts:
            mp = jnp.max(s, axis=0, keepdims=True)
            m = mp if m is None else jnp.maximum(m, mp)
        return m

    def softmax_pv(parts, m, keys, vt_ref):
        p = [jnp.exp(s - m).astype(BF16) for s in parts]
        p = p[0] if len(p) == 1 else jnp.concatenate(p, axis=0)
        acc = _dot(vt_ref[:, keys], p)
        return acc[0:HEAD_DIM] / acc[HEAD_DIM:HEAD_DIM + 1]

    def scores(i):
        qs = i * blk
        rows = slice(qs, qs + blk)
        q01 = q_ref[rows, 0:LANES]
        q23 = q_ref[rows, LANES:2 * LANES]
        zero = jnp.zeros_like(q01)
        qst = jnp.concatenate([jnp.where(low, q01, zero), jnp.where(low, zero, q01),
                               jnp.where(low, q23, zero), jnp.where(low, zero, q23)], axis=0)

        dist_c = (qs + tok).astype(F32) - cmp_end
        valid_c = (dist_c >= 0.0) & (cmpr < n_cmp)
        sc = _dot_nt(kc_ref[...], qst) - slope * dist_c
        sc = jnp.where(valid_c, sc, -jnp.inf)
        mc = jnp.max(sc, axis=0, keepdims=True)
        mc = jnp.where(mc == -jnp.inf, 0.0, mc)
        pc = jnp.exp(sc - mc)
        ssum = jnp.sum(pc, axis=0, keepdims=True)
        pc = pc / jnp.where(ssum > 0.0, ssum, 1.0)
        oc = _dot(vct_ref[0:HEAD_DIM, :], pc.astype(BF16))

        pcs = (pc[:, 0:blk] + pc[:, blk:2 * blk] + pc[:, 2 * blk:3 * blk]
               + pc[:, 3 * blk:4 * blk])
        hi, lo = _split2(pcs)
        imp = _dot(mselt_ref[...], jnp.concatenate([hi, lo], axis=0))
        back = (qs + tcol) // SEL_BLOCK - jrow
        imp = jnp.where(back < 0, -jnp.inf,
                        jnp.where(back < SEL_N_LOCAL, jnp.inf,
                                  jnp.where(jrow == 0, jnp.inf, imp)))
        rank = jnp.zeros((n_sel, blk), F32)
        for c in range(n_sel):
            rowc = imp[c:c + 1, :]
            tie = jnp.where(jrow > c, 1.0, 0.0)
            rank = rank + jnp.where(rowc > imp, 1.0, jnp.where(rowc == imp, tie, 0.0))
        unsel_t = jnp.where(rank >= float(SEL_TOPK), 1.0, 0.0).astype(BF16)
        unsel_t = jnp.concatenate([unsel_t, jnp.zeros((LANES - n_sel, blk), BF16)], axis=0)
        unsel = _dot_nt(eye, unsel_t).astype(BF16)

        qop_w = jnp.concatenate([qst, qslope], axis=1)
        qaug = jnp.where(sel_lane, jnp.concatenate([unsel] * NSA_HEADS, axis=0), qslope)
        qop_s = jnp.concatenate([qst, qaug], axis=1)

        win = []
        w0 = max(qs - WINDOW, 0)
        if i >= nwin:
            win.append(jnp.where(after_t, _dot_nt(kkw_ref[w0:w0 + blk, :], qop_w), -jnp.inf))
        wm = max(qs - WINDOW + blk, 0)
        if qs > wm:
            win.append(_dot_nt(kkw_ref[wm:qs, :], qop_w))
        win.append(jnp.where(causal_t, _dot_nt(kkw_ref[rows, :], qop_w), -jnp.inf))

        sel = []
        if i > 0:
            sel.append(_dot_nt(kks_ref[0:qs, :], qop_s))
        sel.append(jnp.where(causal_t, _dot_nt(kks_ref[rows, :], qop_s), -jnp.inf))
        return oc, win, col_max(win), sel, col_max(sel)

    def outputs(i, st):
        oc, win, m_win, sel, m_sel = st
        qs = i * blk
        rows = slice(qs, qs + blk)
        o_win = softmax_pv(win, m_win, slice(max(qs - WINDOW, 0), qs + blk), vwt_ref)
        o_sel = softmax_pv(sel, m_sel, slice(0, qs + blk), vst_ref)

        sg_t = _sigmoid(misc_ref[rows, :]).T

        def gate(branch):
            return jnp.concatenate(
                [sg_t[MISC_NG + 3 * g + branch:MISC_NG + 3 * g + branch + 1, :]
                 for g in NSA_HEAD_ORDER], axis=1)

        tot = oc * gate(0) + o_sel * gate(1) + o_win * gate(2)
        by_head = {g: tot[:, k * blk:(k + 1) * blk] for k, g in enumerate(NSA_HEAD_ORDER)}
        out = jnp.concatenate([by_head[g] for g in range(NSA_HEADS)], axis=0).T
        z = z_ref[rows, :]
        o_ref[rows, :] = (out * (z * _sigmoid(z))).astype(BF16)

    _emit_pipelined([(i,) for i in range(seq // blk)], scores, outputs, depth=1)


def _nsa_constants(seq, blk):
    n_cmp = (seq - CMP_BLOCK) // CMP_STRIDE + 1
    n_sel = seq // SEL_BLOCK
    cs = np.arange(n_cmp) * CMP_STRIDE
    ce = cs + CMP_BLOCK - 1
    ss = np.arange(n_sel) * SEL_BLOCK
    se = ss + SEL_BLOCK - 1
    msel_t = np.zeros((n_sel, LANES), np.float32)
    msel_t[:, :n_cmp] = ((cs[:, None] <= se[None, :]) & (ce[:, None] >= ss[None, :])).T
    msel_t = np.concatenate([msel_t, msel_t], axis=1)
    key = np.arange(seq)
    kaug_w = np.zeros((seq, LANES), np.float32)
    kaug_w[:, POS_HI_LANE] = key // 16
    kaug_w[:, POS_LO_LANE] = key % 16
    kaug_s = kaug_w.copy()
    kaug_s[key, key // SEL_BLOCK] = PEN
    qslope = np.zeros((NSA_HEADS * blk, LANES), np.float32)
    for k, g in enumerate(NSA_HEAD_ORDER):
        qslope[k * blk:(k + 1) * blk, POS_HI_LANE] = 16.0 * NSA_SLOPES[g]
        qslope[k * blk:(k + 1) * blk, POS_LO_LANE] = NSA_SLOPES[g]
    return (jnp.asarray(msel_t, BF16), jnp.asarray(kaug_s, BF16), jnp.asarray(kaug_w, BF16),
            jnp.asarray(qslope, BF16))


def _nsa(pb3, pf3, pos_k, w1_k, w2_k, pos_v, w1_v, w2_v, blk=128):
    bsz, seq, _ = pb3.shape
    nchunk = seq // CMP_STRIDE
    w1k = w1_k.astype(BF16).reshape(CMP_BLOCK, HEAD_DIM, HEAD_DIM)
    w1v = w1_v.astype(BF16).reshape(CMP_BLOCK, HEAD_DIM, HEAD_DIM)
    w1z = jnp.zeros_like(w1k)
    w1 = jnp.concatenate([jnp.concatenate([w1k, w1z], axis=2),
                          jnp.concatenate([w1z, w1v], axis=2)], axis=1)
    pos = jnp.concatenate([pos_k, pos_v], axis=1)
    zero = jnp.zeros((LANES, LANES), F32)
    w2k = zero.at[:HEAD_DIM, :].set(jnp.concatenate([w2_k, w2_k], axis=1))
    w2v = zero.at[HEAD_DIM:, :].set(jnp.concatenate([w2_v, w2_v], axis=1))
    consts = _nsa_constants(seq, blk)

    def col(base, nblk=1):
        return pl.BlockSpec((None, seq, nblk * LANES), lambda b: (b, 0, base // nblk))

    def whole(shape):
        return pl.BlockSpec(shape, lambda b: (0,) * len(shape))

    return pl.pallas_call(
        functools.partial(_nsa_kernel, seq=seq, blk=blk),
        grid=(bsz,),
        in_specs=[
            col(PB_NQ, 2), col(PB_NSEL), col(PB_NWIN),
            col(PF_NZ, 2), col(PF_MISC), col(PF_KVC),
            whole(pos.shape), whole(w1.shape), whole(w2k.shape), whole(w2v.shape),
        ] + [whole(c.shape) for c in consts],
        out_specs=pl.BlockSpec((None, seq, NSA_W), lambda b: (b, 0, 0)),
        out_shape=jax.ShapeDtypeStruct((bsz, seq, NSA_W), BF16),
        scratch_shapes=[
            pltpu.VMEM((nchunk, LANES), BF16),
            pltpu.VMEM((LANES, nchunk), BF16),
            pltpu.VMEM((seq, 2 * LANES), BF16),
            pltpu.VMEM((seq, 2 * LANES), BF16),
            pltpu.VMEM((VT_ROWS, seq), BF16),
            pltpu.VMEM((VT_ROWS, seq), BF16),
        ],
        compiler_params=pltpu.CompilerParams(
            dimension_semantics=("arbitrary",), vmem_limit_bytes=VMEM_LIMIT),
        name="nsa_attn",
    )(pb3, pb3, pb3, pf3, pf3, pf3, pos, w1, w2k.astype(BF16), w2v.astype(BF16), *consts)


def kernel(x, norm_g, w_in, b_f, cmp_pos_k, cmp_w1_k, cmp_w2_k,
           cmp_pos_v, cmp_w1_v, cmp_w2_v, w_out, final_g):
    bsz, seq, d = x.shape
    xf = x.reshape(bsz * seq, d)
    attn = None
    for l in range(DEPTH):
        res = _proj(xf, norm_g[l], attn, None if l == 0 else w_out[l - 1].astype(BF16),
                    _pack_w_in(w_in[l]))
        if l > 0:
            xf = res[0]
        pb, pf = res[-2:]
        pb3 = pb.reshape(bsz, seq, NB_COLS)
        pf3 = pf.reshape(bsz, seq, NF_COLS)
        o_fox = _fox(pb3, pf3, b_f[l])
        o_sb = _sb(pb3, pf3)
        o_nsa = _nsa(pb3, pf3, cmp_pos_k[l], cmp_w1_k[l], cmp_w2_k[l],
                     cmp_pos_v[l], cmp_w1_v[l], cmp_w2_v[l])
        attn = (o_fox.reshape(bsz * seq, FOX_W), o_sb.reshape(bsz * seq, SB_W),
                o_nsa.reshape(bsz * seq, NSA_W))
    out, = _proj(xf, final_g, attn, w_out[DEPTH - 1].astype(BF16))
    return out.reshape(bsz, seq, d)
```

```python
import functools

import numpy as np
import jax
import jax.numpy as jnp
from jax import lax
from jax.experimental import pallas as pl
from jax.experimental.pallas import tpu as pltpu

F32 = jnp.float32
BF16 = jnp.bfloat16

D_MODEL = 1024
DEPTH = 2
HEAD_DIM = 64
LANES = 128
VT_ROWS = HEAD_DIM + 16
FOX_HEADS = 6
SB_HEADS = 6
NSA_HEADS = 4
FOX_W = FOX_HEADS * HEAD_DIM
SB_W = SB_HEADS * HEAD_DIM
NSA_W = NSA_HEADS * HEAD_DIM
CMP_BLOCK = 32
CMP_STRIDE = 16
SEL_BLOCK = 64
SEL_TOPK = 8
SEL_N_LOCAL = 2
WINDOW = 512
NORM_EPS = 1e-6
QK_SCALE = HEAD_DIM ** -0.5
PEN = -(2.0 ** 100)

PB_FQ, PB_FK, PB_FV, PB_SQ, PB_SK, PB_SV, PB_NQ, PB_NSEL, PB_NWIN = (
    0, 3, 6, 9, 12, 15, 18, 20, 21)
PB_BLOCKS = 22
PF_FZ, PF_SZ, PF_NZ, PF_KVC, PF_MISC = 0, 3, 6, 8, 9
PF_BLOCKS = 10
NB_COLS = PB_BLOCKS * LANES
NF_COLS = PF_BLOCKS * LANES
MISC_PAIR_LANES = 16
MISC_FF_COPIES = 6
MISC_NG = MISC_PAIR_LANES * (FOX_HEADS // 2)

VMEM_PHYSICAL = 64 * 1024 * 1024
VMEM_LIMIT = VMEM_PHYSICAL - 4 * 1024 * 1024

_NT = (((1,), (1,)), ((), ()))


def _dot(a, b):
    return jnp.dot(a, b, preferred_element_type=F32)


def _dot_nt(a, b):
    return lax.dot_general(a, b, _NT, preferred_element_type=F32)


def _sigmoid(x):
    return 1.0 / (1.0 + jnp.exp(-x))


def _log_sigmoid(x):
    return -(jnp.maximum(-x, 0.0) + jnp.log1p(jnp.exp(-jnp.abs(x))))


def _split3(x):
    hi = x.astype(BF16)
    r = x - hi.astype(F32)
    mid = r.astype(BF16)
    lo = (r - mid.astype(F32)).astype(BF16)
    return hi, mid, lo


def _split2(x):
    hi = x.astype(BF16)
    lo = (x - hi.astype(F32)).astype(BF16)
    return hi, lo


PIPELINE_DEPTH = 4


def _emit_pipelined(items, first, second, depth=PIPELINE_DEPTH):
    pending = {}
    for n, item in enumerate(items):
        pending[item] = first(*item)
        if n >= depth:
            prev = items[n - depth]
            second(*prev, pending.pop(prev))
    for prev in items[max(len(items) - depth, 0):]:
        second(*prev, pending.pop(prev))


def _proj_kernel(*refs, has_out, has_in):
    it = iter(refs)
    y = next(it)[...]
    if has_out:
        of_ref, os_ref, on_ref, wo_ref = next(it), next(it), next(it), next(it)
        y = (y + _dot(of_ref[...], wo_ref[0:FOX_W, :])
             + _dot(os_ref[...], wo_ref[FOX_W:FOX_W + SB_W, :])
             + _dot(on_ref[...], wo_ref[FOX_W + SB_W:, :]))
    g_ref = next(it)
    ms = jnp.mean(y * y, axis=-1, keepdims=True)
    h = y * lax.rsqrt(ms + NORM_EPS) * g_ref[...]
    if not has_in:
        next(it)[...] = h
        return
    wi_ref = next(it)
    if has_out:
        next(it)[...] = y
    pb_ref, pf_ref = next(it), next(it)
    h = h.astype(BF16)
    chunk = 4 * LANES
    for c in range(0, NB_COLS, chunk):
        e = min(c + chunk, NB_COLS)
        pb_ref[:, c:e] = _dot(h, wi_ref[:, c:e]).astype(BF16)
    for c in range(0, NF_COLS, chunk):
        e = min(c + chunk, NF_COLS)
        pf_ref[:, c:e] = _dot(h, wi_ref[:, NB_COLS + c:NB_COLS + e])


def _proj(xf, gain, attn=None, w_out=None, w_in=None):
    m, d = xf.shape
    has_out, has_in = attn is not None, w_in is not None
    tm = 512 if has_in else 2048

    def rows(width):
        return pl.BlockSpec((tm, width), lambda i: (i, 0))

    def whole(shape):
        return pl.BlockSpec(shape, lambda i: (0, 0))

    args, in_specs, out_specs, out_shape = [xf], [rows(d)], [], []
    if has_out:
        args += [*attn, w_out]
        in_specs += [rows(FOX_W), rows(SB_W), rows(NSA_W), whole((d, d))]
    args.append(gain.reshape(1, d))
    in_specs.append(whole((1, d)))
    if has_in:
        args.append(w_in)
        in_specs.append(whole(w_in.shape))
    if has_out or not has_in:
        out_specs.append(rows(d))
        out_shape.append(jax.ShapeDtypeStruct((m, d), F32))
    if has_in:
        out_specs += [rows(NB_COLS), rows(NF_COLS)]
        out_shape += [jax.ShapeDtypeStruct((m, NB_COLS), BF16),
                      jax.ShapeDtypeStruct((m, NF_COLS), F32)]
    return pl.pallas_call(
        functools.partial(_proj_kernel, has_out=has_out, has_in=has_in),
        grid=(m // tm,),
        in_specs=in_specs,
        out_specs=out_specs,
        out_shape=out_shape,
        compiler_params=pltpu.CompilerParams(
            dimension_semantics=("arbitrary",), vmem_limit_bytes=VMEM_LIMIT),
        name="proj_" + ("out" if has_out else "") + ("in" if has_in else "norm"),
    )(*args)


def _pack_w_in(w):
    sizes = (FOX_W, FOX_W, FOX_W, FOX_HEADS, FOX_W, SB_W, SB_W, SB_W, SB_W,
             NSA_W, HEAD_DIM, HEAD_DIM, HEAD_DIM, HEAD_DIM, HEAD_DIM, HEAD_DIM,
             3 * NSA_HEADS, NSA_W)
    offs = np.concatenate([[0], np.cumsum(sizes)])
    w = w.astype(BF16)
    (fq, fk, fv, ff, fz, sq, sk, sv, sz, nq, nkc, nvc, nks, nvs, nkw, nvw, ng, nz) = [
        w[:, offs[i]:offs[i + 1]] for i in range(len(sizes))]
    zeros = jnp.zeros((w.shape[0], LANES), w.dtype)
    cols = [fq * QK_SCALE, fk, fv, sq * QK_SCALE, sk, sv, nq * QK_SCALE,
            nks, nvs, nkw, nvw,
            fz, sz, nz, nkc, nvc, _misc_ff_layout(ff, zeros), ng,
            zeros[:, :LANES - MISC_NG - 3 * NSA_HEADS]]
    return jnp.concatenate(cols, axis=1)


def _misc_ff_layout(ff, zeros):
    cols = []
    for p in range(FOX_HEADS // 2):
        cols += [ff[:, 2 * p:2 * p + 1]] * MISC_FF_COPIES + [ff[:, 2 * p + 1:2 * p + 2]] * MISC_FF_COPIES
        cols.append(zeros[:, :MISC_PAIR_LANES - 2 * MISC_FF_COPIES])
    return jnp.concatenate(cols, axis=1)


def _fox_kernel(q_ref, k_ref, v_ref, z_ref, misc_ref, bf_ref, tri_ref,
                o_ref, qq_ref, kk_ref, vpt_ref, cps_ref, *, seq, blk):
    pair = pl.program_id(1)
    low = lax.broadcasted_iota(jnp.int32, (blk, LANES), 1) < HEAD_DIM
    lane_b = lax.broadcasted_iota(jnp.int32, (blk, LANES), 1)
    nblk = seq // blk

    @pl.when(pair == 0)
    def _():
        tri = tri_ref[...]
        part = (lane_b % MISC_PAIR_LANES) % 3
        carry = jnp.zeros((1, LANES), F32)
        for b in range(nblk):
            rows = slice(b * blk, (b + 1) * blk)
            ls = _log_sigmoid(misc_ref[rows, :] + bf_ref[...])
            c3 = _dot(tri, jnp.concatenate(_split3(ls), axis=1))
            cb = c3[:, 0:LANES] + c3[:, LANES:2 * LANES] + c3[:, 2 * LANES:3 * LANES] + carry
            carry = cb[blk - 1:blk, :]
            hi, mid, lo = _split3(cb)
            cps_ref[rows, :] = jnp.where(part == 0, hi.astype(F32), jnp.where(
                part == 1, mid.astype(F32), lo.astype(F32))).astype(BF16)

    rel = lane_b - MISC_PAIR_LANES * pair
    third = jnp.where(rel < 0, -1, jnp.where(rel < 3, 0, jnp.where(rel < 6, 1, jnp.where(
        rel < 9, 2, jnp.where(rel < 12, 3, -1)))))

    def lanes_of(*thirds):
        sel = jnp.zeros((blk, LANES), F32)
        for t in thirds:
            sel = jnp.where(third == t, 1.0, sel)
        return sel.astype(BF16)

    k_ones, k_parts = lanes_of(0, 2), lanes_of(1, 3)
    qa_parts, qa_ones = lanes_of(0), lanes_of(1)
    qb_parts, qb_ones = lanes_of(2), lanes_of(3)
    eye = jnp.where(lax.broadcasted_iota(jnp.int32, (LANES, LANES), 0)
                    == lax.broadcasted_iota(jnp.int32, (LANES, LANES), 1), 1.0, 0.0).astype(BF16)
    for b in range(nblk):
        rows = slice(b * blk, (b + 1) * blk)
        cps = cps_ref[rows, :]
        q2 = q_ref[rows, :]
        v2 = v_ref[rows, :]
        zero = jnp.zeros_like(q2)
        one = jnp.ones_like(q2)
        kk_ref[rows, 0:LANES] = k_ref[rows, :]
        kk_ref[rows, LANES:2 * LANES] = k_ones - cps * k_parts
        qq_ref[0, rows, 0:LANES] = jnp.where(low, q2, zero)
        qq_ref[0, rows, LANES:2 * LANES] = cps * qa_parts + qa_ones
        qq_ref[1, rows, 0:LANES] = jnp.where(low, zero, q2)
        qq_ref[1, rows, LANES:2 * LANES] = cps * qb_parts + qb_ones
        vt = _dot_nt(eye, v2).astype(BF16)
        vpt_ref[0, 0:HEAD_DIM, rows] = vt[0:HEAD_DIM]
        vpt_ref[1, 0:HEAD_DIM, rows] = vt[HEAD_DIM:LANES]
        vpt_ref[0, HEAD_DIM:VT_ROWS, rows] = jnp.ones((VT_ROWS - HEAD_DIM, blk), BF16)
        vpt_ref[1, HEAD_DIM:VT_ROWS, rows] = jnp.ones((VT_ROWS - HEAD_DIM, blk), BF16)

    causal_t = (lax.broadcasted_iota(jnp.int32, (blk, blk), 0)
                <= lax.broadcasted_iota(jnp.int32, (blk, blk), 1))

    def scores(i, h):
        r0 = i * blk
        rows = slice(r0, r0 + blk)
        qa = qq_ref[h, rows, :]
        sd = jnp.where(causal_t, _dot_nt(kk_ref[rows, :], qa), -jnp.inf)
        m = jnp.max(sd, axis=0, keepdims=True)
        sm = None
        if i > 0:
            sm = _dot_nt(kk_ref[0:r0, :], qa)
            m = jnp.maximum(m, jnp.max(sm, axis=0, keepdims=True))
        return sd, sm, m

    outs = {}

    def weighted_sum(i, h, st):
        sd, sm, m = st
        rows = slice(i * blk, (i + 1) * blk)
        p = jnp.exp(sd - m).astype(BF16)
        if i > 0:
            p = jnp.concatenate([jnp.exp(sm - m).astype(BF16), p], axis=0)
        a = _dot(vpt_ref[h, :, 0:(i + 1) * blk], p)
        outs[i, h] = a[0:HEAD_DIM] / a[HEAD_DIM:HEAD_DIM + 1]
        if h == 1:
            o = jnp.concatenate([outs.pop((i, 0)), outs.pop((i, 1))], axis=0).T
            z = z_ref[rows, :]
            o_ref[rows, :] = (o * (z * _sigmoid(z))).astype(BF16)

    _emit_pipelined([(i, h) for i in range(nblk) for h in range(2)], scores, weighted_sum)


def _fox(pb3, pf3, b_f, blk=256):
    bsz, seq, _ = pb3.shape
    npair = FOX_HEADS // 2
    bias = jnp.zeros((1, LANES), F32).at[:, :MISC_NG].set(
        _misc_ff_layout(b_f.reshape(1, FOX_HEADS), jnp.zeros((1, LANES), F32)))
    tri = jnp.asarray(np.tril(np.ones((blk, blk), np.float32)), BF16)

    def col(base):
        return pl.BlockSpec((None, seq, LANES), lambda b, p: (b, 0, base + p))

    def whole(shape):
        return pl.BlockSpec(shape, lambda b, p: (0,) * len(shape))

    return pl.pallas_call(
        functools.partial(_fox_kernel, seq=seq, blk=blk),
        grid=(bsz, npair),
        in_specs=[
            col(PB_FQ), col(PB_FK), col(PB_FV), col(PF_FZ),
            pl.BlockSpec((None, seq, LANES), lambda b, p: (b, 0, PF_MISC)),
            whole((1, LANES)), whole((blk, blk)),
        ],
        out_specs=pl.BlockSpec((None, seq, LANES), lambda b, p: (b, 0, p)),
        out_shape=jax.ShapeDtypeStruct((bsz, seq, FOX_W), BF16),
        scratch_shapes=[
            pltpu.VMEM((2, seq, 2 * LANES), BF16),
            pltpu.VMEM((seq, 2 * LANES), BF16),
            pltpu.VMEM((2, VT_ROWS, seq), BF16),
            pltpu.VMEM((seq, LANES), BF16),
        ],
        compiler_params=pltpu.CompilerParams(
            dimension_semantics=("arbitrary", "arbitrary"), vmem_limit_bytes=VMEM_LIMIT),
        name="fox_attn",
    )(pb3, pb3, pb3, pf3, pf3, bias, tri)


SB_NEAR_TILES = 2
SB_DEAD = -104.0


def _sb_kernel(q_ref, k_ref, v_ref, z_ref, suf_ref, o_ref, vt_ref, acc_ref, carry_ref,
               *, seq, blk):
    low = lax.broadcasted_iota(jnp.int32, (blk, LANES), 1) < HEAD_DIM
    strict = (lax.broadcasted_iota(jnp.int32, (blk, blk), 0)
              < lax.broadcasted_iota(jnp.int32, (blk, blk), 1))
    suf = suf_ref[...]
    nblk = seq // blk
    eye = jnp.where(lax.broadcasted_iota(jnp.int32, (LANES, LANES), 0)
                    == lax.broadcasted_iota(jnp.int32, (LANES, LANES), 1), 1.0, 0.0).astype(BF16)
    for b in range(nblk):
        rows = slice(b * blk, (b + 1) * blk)
        vt_ref[:, rows] = _dot_nt(eye, v_ref[rows, :]).astype(BF16)

    def log_terms(qh, j_hi, j_lo):
        z = _dot_nt(k_ref[j_lo * blk:(j_hi + 1) * blk, :], qh)
        lsz = jnp.minimum(z, 0.0) - jnp.log(1.0 + jnp.exp(-jnp.abs(z)))
        return lsz, lsz - z

    def tiles(i, h, j_hi, j_lo, carry, terms):
        lsz, l1m = terms
        k0 = j_lo * blk
        vrows = slice(h * HEAD_DIM, (h + 1) * HEAD_DIM)
        out = None
        for j in range(j_hi, j_lo - 1, -1):
            cs = slice(j * blk - k0, (j + 1) * blk - k0)
            l1 = l1m[cs, :]
            if j == i:
                l1 = jnp.where(strict, l1, 0.0)
            r = _dot(suf, l1.astype(BF16))
            a = jnp.exp(lsz[cs, :] + (r + carry))
            if j == i:
                a = jnp.where(strict, a, 0.0)
            pv = _dot(vt_ref[vrows, j * blk:(j + 1) * blk], a.astype(BF16))
            out = pv if out is None else out + pv
            carry = carry + (r[0:1, :] + l1[0:1, :])
        return out, carry

    def q_head(i, h):
        q2 = q_ref[i * blk:(i + 1) * blk, :]
        zero = jnp.zeros_like(q2)
        return jnp.where(low, q2, zero) if h == 0 else jnp.where(low, zero, q2)

    def j_near(i):
        return max(i - SB_NEAR_TILES + 1, 0)

    alive = {}

    def near_terms(i, h):
        return log_terms(q_head(i, h), i, j_near(i))

    def near_tiles(i, h, terms):
        out, carry = tiles(i, h, i, j_near(i), jnp.zeros((1, blk), F32), terms)
        acc_ref[i, h] = out
        if j_near(i) > 0:
            carry_ref[i, h] = jnp.broadcast_to(carry, (8, blk))
            alive[i, h] = jnp.max(carry) >= SB_DEAD

    _emit_pipelined([(i, h) for i in range(nblk) for h in range(2)], near_terms, near_tiles)

    for (i, h), flag in alive.items():
        @pl.when(flag)
        def _(i=i, h=h):
            far, _ = tiles(i, h, j_near(i) - 1, 0, carry_ref[i, h][0:1, :],
                           log_terms(q_head(i, h), j_near(i) - 1, 0))
            acc_ref[i, h] = acc_ref[i, h] + far

    for i in range(nblk):
        rows = slice(i * blk, (i + 1) * blk)
        o = jnp.concatenate([acc_ref[i, 0], acc_ref[i, 1]], axis=0).T
        zg = z_ref[rows, :]
        o_ref[rows, :] = (o * (zg * _sigmoid(zg))).astype(BF16)


def _sb(pb3, pf3, blk=256):
    bsz, seq, _ = pb3.shape
    npair = SB_HEADS // 2
    suf = jnp.asarray(np.triu(np.ones((blk, blk), np.float32), 1), BF16)

    def col(base):
        return pl.BlockSpec((None, seq, LANES), lambda b, p: (b, 0, base + p))

    return pl.pallas_call(
        functools.partial(_sb_kernel, seq=seq, blk=blk),
        grid=(bsz, npair),
        in_specs=[
            col(PB_SQ), col(PB_SK), col(PB_SV), col(PF_SZ),
            pl.BlockSpec(suf.shape, lambda b, p: (0, 0)),
        ],
        out_specs=pl.BlockSpec((None, seq, LANES), lambda b, p: (b, 0, p)),
        out_shape=jax.ShapeDtypeStruct((bsz, seq, SB_W), BF16),
        scratch_shapes=[pltpu.VMEM((LANES, seq), BF16),
                        pltpu.VMEM((seq // blk, 2, HEAD_DIM, blk), F32),
                        pltpu.VMEM((seq // blk, 2, 8, blk), F32)],
        compiler_params=pltpu.CompilerParams(
            dimension_semantics=("arbitrary", "arbitrary"), vmem_limit_bytes=VMEM_LIMIT),
        name="sb_attn",
    )(pb3, pb3, pb3, pf3, suf)


NSA_HEAD_ORDER = (0, 1, 2, 3)
SEL_LANES = 32
POS_HI_LANE = 32
POS_LO_LANE = 33
NSA_SLOPES = tuple(2.0 ** (-8.0 * (g + 1) / NSA_HEADS) for g in range(NSA_HEADS))


def _nsa_kernel(q_ref, ksv_ref, kwv_ref, z_ref, misc_ref, kvc_ref,
                pos_ref, w1_ref, w2k_ref, w2v_ref,
                mselt_ref, kaugs_ref, kaugw_ref, qslope_ref,
                o_ref, kc_ref, vct_ref, kks_ref, kkw_ref, vst_ref, vwt_ref, *, seq, blk):
    cols4 = NSA_HEADS * blk
    n_cmp = (seq - CMP_BLOCK) // CMP_STRIDE + 1
    n_sel = seq // SEL_BLOCK
    nwin = WINDOW // blk
    eye = jnp.where(lax.broadcasted_iota(jnp.int32, (LANES, LANES), 0)
                    == lax.broadcasted_iota(jnp.int32, (LANES, LANES), 1), 1.0, 0.0).astype(BF16)

    nchunk = seq // CMP_STRIDE
    ha = jnp.zeros((nchunk, LANES), F32)
    hb = jnp.zeros((nchunk, LANES), F32)
    for p in range(CMP_STRIDE):
        xp = kvc_ref[pl.ds(p, nchunk, stride=CMP_STRIDE), :]
        q = p + CMP_STRIDE
        ha = ha + _dot((xp + pos_ref[p:p + 1, :]).astype(BF16), w1_ref[p])
        hb = hb + _dot((xp + pos_ref[q:q + 1, :]).astype(BF16), w1_ref[q])
    hid = ha + pltpu.roll(hb, nchunk - 1, axis=0)
    hid = (hid * _sigmoid(hid)).astype(BF16)
    kc_ref[...] = _dot(hid, w2k_ref[...]).astype(BF16)
    vct_ref[...] = _dot_nt(eye, _dot(hid, w2v_ref[...]).astype(BF16)).astype(BF16)

    tb = 2 * LANES
    lane_t = lax.broadcasted_iota(jnp.int32, (tb, LANES), 1)
    low_t = lane_t < HEAD_DIM
    er = lax.broadcasted_iota(jnp.int32, (LANES, LANES), 0)
    ec = lax.broadcasted_iota(jnp.int32, (LANES, LANES), 1)
    dup_lo = jnp.where(er == ec % HEAD_DIM, 1.0, 0.0).astype(BF16)
    swap = jnp.where(ec == (er + HEAD_DIM) % LANES, 1.0, 0.0).astype(BF16)
    kks_ref[:, LANES:2 * LANES] = kaugs_ref[...]
    kkw_ref[:, LANES:2 * LANES] = kaugw_ref[...]
    for b in range(seq // tb):
        rows = slice(b * tb, (b + 1) * tb)
        xs = ksv_ref[rows, :]
        xw = kwv_ref[rows, :]
        one = jnp.ones_like(xs)
        kks_ref[rows, 0:LANES] = _dot(xs, dup_lo).astype(BF16)
        kkw_ref[rows, 0:LANES] = _dot(xw, dup_lo).astype(BF16)
        vst_ref[:, rows] = _dot_nt(swap, jnp.where(low_t, one, xs)).astype(BF16)[0:VT_ROWS]
        vwt_ref[:, rows] = _dot_nt(swap, jnp.where(low_t, one, xw)).astype(BF16)[0:VT_ROWS]

    low = lax.broadcasted_iota(jnp.int32, (blk, LANES), 1) < HEAD_DIM
    colt = lax.broadcasted_iota(jnp.int32, (1, cols4), 1)
    grp = colt // blk
    tok = colt - grp * blk
    gslope = [NSA_SLOPES[g] for g in NSA_HEAD_ORDER]
    slope = jnp.where(grp == 0, gslope[0], jnp.where(grp == 1, gslope[1],
                      jnp.where(grp == 2, gslope[2], gslope[3]))).astype(F32)
    keyr = lax.broadcasted_iota(jnp.int32, (blk, 1), 0)
    causal_t = keyr <= tok
    after_t = keyr > tok
    cmpr = lax.broadcasted_iota(jnp.int32, (LANES, 1), 0)
    cmp_end = (cmpr * CMP_STRIDE + (CMP_BLOCK - 1)).astype(F32)
    jrow = lax.broadcasted_iota(jnp.int32, (n_sel, blk), 0)
    tcol = lax.broadcasted_iota(jnp.int32, (n_sel, blk), 1)
    sel_lane = lax.broadcasted_iota(jnp.int32, (cols4, LANES), 1) < SEL_LANES
    qslope = qslope_ref[...]

    def col_max(parts):
        m = None
        for s in parts:
            mp = jnp.max(s, axis=0, keepdims=True)
            m = mp if m is None else jnp.maximum(m, mp)
        return m

    def softmax_pv(parts, m, keys, vt_ref):
        p = [jnp.exp(s - m).astype(BF16) for s in parts]
        p = p[0] if len(p) == 1 else jnp.concatenate(p, axis=0)
        acc = _dot(vt_ref[:, keys], p)
        return acc[0:HEAD_DIM] / acc[HEAD_DIM:HEAD_DIM + 1]

    def scores(i):
        qs = i * blk
        rows = slice(qs, qs + blk)
        q01 = q_ref[rows, 0:LANES]
        q23 = q_ref[rows, LANES:2 * LANES]
        zero = jnp.zeros_like(q01)
        qst = jnp.concatenate([jnp.where(low, q01, zero), jnp.where(low, zero, q01),
                               jnp.where(low, q23, zero), jnp.where(low, zero, q23)], axis=0)

        dist_c = (qs + tok).astype(F32) - cmp_end
        valid_c = (dist_c >= 0.0) & (cmpr < n_cmp)
        sc = _dot_nt(kc_ref[...], qst) - slope * dist_c
        sc = jnp.where(valid_c, sc, -jnp.inf)
        mc = jnp.max(sc, axis=0, keepdims=True)
        mc = jnp.where(mc == -jnp.inf, 0.0, mc)
        pc = jnp.exp(sc - mc)
        ssum = jnp.sum(pc, axis=0, keepdims=True)
        pc = pc / jnp.where(ssum > 0.0, ssum, 1.0)
        oc = _dot(vct_ref[0:HEAD_DIM, :], pc.astype(BF16))

        pcs = (pc[:, 0:blk] + pc[:, blk:2 * blk] + pc[:, 2 * blk:3 * blk]
               + pc[:, 3 * blk:4 * blk])
        hi, lo = _split2(pcs)
        imp = _dot(mselt_ref[...], jnp.concatenate([hi, lo], axis=0))
        back = (qs + tcol) // SEL_BLOCK - jrow
        imp = jnp.where(back < 0, -jnp.inf,
                        jnp.where(back < SEL_N_LOCAL, jnp.inf,
                                  jnp.where(jrow == 0, jnp.inf, imp)))
        rank = jnp.zeros((n_sel, blk), F32)
        for c in range(n_sel):
            rowc = imp[c:c + 1, :]
            tie = jnp.where(jrow > c, 1.0, 0.0)
            rank = rank + jnp.where(rowc > imp, 1.0, jnp.where(rowc == imp, tie, 0.0))
        unsel_t = jnp.where(rank >= float(SEL_TOPK), 1.0, 0.0).astype(BF16)
        unsel_t = jnp.concatenate([unsel_t, jnp.zeros((LANES - n_sel, blk), BF16)], axis=0)
        unsel = _dot_nt(eye, unsel_t).astype(BF16)

        qop_w = jnp.concatenate([qst, qslope], axis=1)
        qaug = jnp.where(sel_lane, jnp.concatenate([unsel] * NSA_HEADS, axis=0), qslope)
        qop_s = jnp.concatenate([qst, qaug], axis=1)

        win = []
        w0 = max(qs - WINDOW, 0)
        if i >= nwin:
            win.append(jnp.where(after_t, _dot_nt(kkw_ref[w0:w0 + blk, :], qop_w), -jnp.inf))
        wm = max(qs - WINDOW + blk, 0)
        if qs > wm:
            win.append(_dot_nt(kkw_ref[wm:qs, :], qop_w))
        win.append(jnp.where(causal_t, _dot_nt(kkw_ref[rows, :], qop_w), -jnp.inf))

        sel = []
        if i > 0:
            sel.append(_dot_nt(kks_ref[0:qs, :], qop_s))
        sel.append(jnp.where(causal_t, _dot_nt(kks_ref[rows, :], qop_s), -jnp.inf))
        return oc, win, col_max(win), sel, col_max(sel)

    def outputs(i, st):
        oc, win, m_win, sel, m_sel = st
        qs = i * blk
        rows = slice(qs, qs + blk)
        o_win = softmax_pv(win, m_win, slice(max(qs - WINDOW, 0), qs + blk), vwt_ref)
        o_sel = softmax_pv(sel, m_sel, slice(0, qs + blk), vst_ref)

        sg_t = _sigmoid(misc_ref[rows, :]).T

        def gate(branch):
            return jnp.concatenate(
                [sg_t[MISC_NG + 3 * g + branch:MISC_NG + 3 * g + branch + 1, :]
                 for g in NSA_HEAD_ORDER], axis=1)

        tot = oc * gate(0) + o_sel * gate(1) + o_win * gate(2)
        by_head = {g: tot[:, k * blk:(k + 1) * blk] for k, g in enumerate(NSA_HEAD_ORDER)}
        out = jnp.concatenate([by_head[g] for g in range(NSA_HEADS)], axis=0).T
        z = z_ref[rows, :]
        o_ref[rows, :] = (out * (z * _sigmoid(z))).astype(BF16)

    _emit_pipelined([(i,) for i in range(seq // blk)], scores, outputs, depth=1)


def _nsa_constants(seq, blk):
    n_cmp = (seq - CMP_BLOCK) // CMP_STRIDE + 1
    n_sel = seq // SEL_BLOCK
    cs = np.arange(n_cmp) * CMP_STRIDE
    ce = cs + CMP_BLOCK - 1
    ss = np.arange(n_sel) * SEL_BLOCK
    se = ss + SEL_BLOCK - 1
    msel_t = np.zeros((n_sel, LANES), np.float32)
    msel_t[:, :n_cmp] = ((cs[:, None] <= se[None, :]) & (ce[:, None] >= ss[None, :])).T
    msel_t = np.concatenate([msel_t, msel_t], axis=1)
    key = np.arange(seq)
    kaug_w = np.zeros((seq, LANES), np.float32)
    kaug_w[:, POS_HI_LANE] = key // 16
    kaug_w[:, POS_LO_LANE] = key % 16
    kaug_s = kaug_w.copy()
    kaug_s[key, key // SEL_BLOCK] = PEN
    qslope = np.zeros((NSA_HEADS * blk, LANES), np.float32)
    for k, g in enumerate(NSA_HEAD_ORDER):
        qslope[k * blk:(k + 1) * blk, POS_HI_LANE] = 16.0 * NSA_SLOPES[g]
        qslope[k * blk:(k + 1) * blk, POS_LO_LANE] = NSA_SLOPES[g]
    return (jnp.asarray(msel_t, BF16), jnp.asarray(kaug_s, BF16), jnp.asarray(kaug_w, BF16),
            jnp.asarray(qslope, BF16))


def _nsa(pb3, pf3, pos_k, w1_k, w2_k, pos_v, w1_v, w2_v, blk=128):
    bsz, seq, _ = pb3.shape
    nchunk = seq // CMP_STRIDE
    w1k = w1_k.astype(BF16).reshape(CMP_BLOCK, HEAD_DIM, HEAD_DIM)
    w1v = w1_v.astype(BF16).reshape(CMP_BLOCK, HEAD_DIM, HEAD_DIM)
    w1z = jnp.zeros_like(w1k)
    w1 = jnp.concatenate([jnp.concatenate([w1k, w1z], axis=2),
                          jnp.concatenate([w1z, w1v], axis=2)], axis=1)
    pos = jnp.concatenate([pos_k, pos_v], axis=1)
    zero = jnp.zeros((LANES, LANES), F32)
    w2k = zero.at[:HEAD_DIM, :].set(jnp.concatenate([w2_k, w2_k], axis=1))
    w2v = zero.at[HEAD_DIM:, :].set(jnp.concatenate([w2_v, w2_v], axis=1))
    consts = _nsa_constants(seq, blk)

    def col(base, nblk=1):
        return pl.BlockSpec((None, seq, nblk * LANES), lambda b: (b, 0, base // nblk))

    def whole(shape):
        return pl.BlockSpec(shape, lambda b: (0,) * len(shape))

    return pl.pallas_call(
        functools.partial(_nsa_kernel, seq=seq, blk=blk),
        grid=(bsz,),
        in_specs=[
            col(PB_NQ, 2), col(PB_NSEL), col(PB_NWIN),
            col(PF_NZ, 2), col(PF_MISC), col(PF_KVC),
            whole(pos.shape), whole(w1.shape), whole(w2k.shape), whole(w2v.shape),
        ] + [whole(c.shape) for c in consts],
        out_specs=pl.BlockSpec((None, seq, NSA_W), lambda b: (b, 0, 0)),
        out_shape=jax.ShapeDtypeStruct((bsz, seq, NSA_W), BF16),
        scratch_shapes=[
            pltpu.VMEM((nchunk, LANES), BF16),
            pltpu.VMEM((LANES, nchunk), BF16),
            pltpu.VMEM((seq, 2 * LANES), BF16),
            pltpu.VMEM((seq, 2 * LANES), BF16),
            pltpu.VMEM((VT_ROWS, seq), BF16),
            pltpu.VMEM((VT_ROWS, seq), BF16),
        ],
        compiler_params=pltpu.CompilerParams(
            dimension_semantics=("arbitrary",), vmem_limit_bytes=VMEM_LIMIT),
        name="nsa_attn",
    )(pb3, pb3, pb3, pf3, pf3, pf3, pos, w1, w2k.astype(BF16), w2v.astype(BF16), *consts)


def kernel(x, norm_g, w_in, b_f, cmp_pos_k, cmp_w1_k, cmp_w2_k,
           cmp_pos_v, cmp_w1_v, cmp_w2_v, w_out, final_g):
    bsz, seq, d = x.shape
    xf = x.reshape(bsz * seq, d)
    attn = None
    for l in range(DEPTH):
        res = _proj(xf, norm_g[l], attn, None if l == 0 else w_out[l - 1].astype(BF16),
                    _pack_w_in(w_in[l]))
        if l > 0:
            xf = res[0]
        pb, pf = res[-2:]
        pb3 = pb.reshape(bsz, seq, NB_COLS)
        pf3 = pf.reshape(bsz, seq, NF_COLS)
        o_fox = _fox(pb3, pf3, b_f[l])
        o_sb = _sb(pb3, pf3)
        o_nsa = _nsa(pb3, pf3, cmp_pos_k[l], cmp_w1_k[l], cmp_w2_k[l],
                     cmp_pos_v[l], cmp_w1_v[l], cmp_w2_v[l])
        attn = (o_fox.reshape(bsz * seq, FOX_W), o_sb.reshape(bsz * seq, SB_W),
                o_nsa.reshape(bsz * seq, NSA_W))
    out, = _proj(xf, final_g, attn, w_out[DEPTH - 1].astype(BF16))
    return out.reshape(bsz, seq, d)
```

```python
import functools

import numpy as np
import jax
import jax.numpy as jnp
from jax import lax
from jax.experimental import pallas as pl
from jax.experimental.pallas import tpu as pltpu

F32 = jnp.float32
BF16 = jnp.bfloat16

D_MODEL = 1024
DEPTH = 2
HEAD_DIM = 64
LANES = 128
VT_ROWS = HEAD_DIM + 16
FOX_HEADS = 6
SB_HEADS = 6
NSA_HEADS = 4
FOX_W = FOX_HEADS * HEAD_DIM
SB_W = SB_HEADS * HEAD_DIM
NSA_W = NSA_HEADS * HEAD_DIM
CMP_BLOCK = 32
CMP_STRIDE = 16
SEL_BLOCK = 64
SEL_TOPK = 8
SEL_N_LOCAL = 2
WINDOW = 512
NORM_EPS = 1e-6
QK_SCALE = HEAD_DIM ** -0.5
PEN = -(2.0 ** 100)

PB_FQ, PB_FK, PB_FV, PB_SQ, PB_SK, PB_SV, PB_NQ, PB_NSEL, PB_NWIN = (
    0, 3, 6, 9, 12, 15, 18, 20, 21)
PB_BLOCKS = 22
PF_FZ, PF_SZ, PF_NZ, PF_KVC, PF_MISC = 0, 3, 6, 8, 9
PF_BLOCKS = 10
NB_COLS = PB_BLOCKS * LANES
NF_COLS = PF_BLOCKS * LANES
MISC_PAIR_LANES = 16
MISC_FF_COPIES = 6
MISC_NG = MISC_PAIR_LANES * (FOX_HEADS // 2)

VMEM_PHYSICAL = 64 * 1024 * 1024
VMEM_LIMIT = VMEM_PHYSICAL - 4 * 1024 * 1024

_NT = (((1,), (1,)), ((), ()))


def _dot(a, b):
    return jnp.dot(a, b, preferred_element_type=F32)


def _dot_nt(a, b):
    return lax.dot_general(a, b, _NT, preferred_element_type=F32)


def _sigmoid(x):
    return 1.0 / (1.0 + jnp.exp(-x))


def _log_sigmoid(x):
    return -(jnp.maximum(-x, 0.0) + jnp.log1p(jnp.exp(-jnp.abs(x))))


def _split3(x):
    hi = x.astype(BF16)
    r = x - hi.astype(F32)
    mid = r.astype(BF16)
    lo = (r - mid.astype(F32)).astype(BF16)
    return hi, mid, lo


def _split2(x):
    hi = x.astype(BF16)
    lo = (x - hi.astype(F32)).astype(BF16)
    return hi, lo


PIPELINE_DEPTH = 4


def _emit_pipelined(items, first, second, depth=PIPELINE_DEPTH):
    pending = {}
    for n, item in enumerate(items):
        pending[item] = first(*item)
        if n >= depth:
            prev = items[n - depth]
            second(*prev, pending.pop(prev))
    for prev in items[max(len(items) - depth, 0):]:
        second(*prev, pending.pop(prev))


def _proj_kernel(*refs, has_out, has_in):
    it = iter(refs)
    y = next(it)[...]
    if has_out:
        of_ref, os_ref, on_ref, wo_ref = next(it), next(it), next(it), next(it)
        y = (y + _dot(of_ref[...], wo_ref[0:FOX_W, :])
             + _dot(os_ref[...], wo_ref[FOX_W:FOX_W + SB_W, :])
             + _dot(on_ref[...], wo_ref[FOX_W + SB_W:, :]))
    g_ref = next(it)
    ms = jnp.mean(y * y, axis=-1, keepdims=True)
    h = y * lax.rsqrt(ms + NORM_EPS) * g_ref[...]
    if not has_in:
        next(it)[...] = h
        return
    wi_ref = next(it)
    if has_out:
        next(it)[...] = y
    pb_ref, pf_ref = next(it), next(it)
    h = h.astype(BF16)
    chunk = 4 * LANES
    for c in range(0, NB_COLS, chunk):
        e = min(c + chunk, NB_COLS)
        pb_ref[:, c:e] = _dot(h, wi_ref[:, c:e]).astype(BF16)
    for c in range(0, NF_COLS, chunk):
        e = min(c + chunk, NF_COLS)
        pf_ref[:, c:e] = _dot(h, wi_ref[:, NB_COLS + c:NB_COLS + e])


def _proj(xf, gain, attn=None, w_out=None, w_in=None):
    m, d = xf.shape
    has_out, has_in = attn is not None, w_in is not None
    tm = 512 if has_in else 2048

    def rows(width):
        return pl.BlockSpec((tm, width), lambda i: (i, 0))

    def whole(shape):
        return pl.BlockSpec(shape, lambda i: (0, 0))

    args, in_specs, out_specs, out_shape = [xf], [rows(d)], [], []
    if has_out:
        args += [*attn, w_out]
        in_specs += [rows(FOX_W), rows(SB_W), rows(NSA_W), whole((d, d))]
    args.append(gain.reshape(1, d))
    in_specs.append(whole((1, d)))
    if has_in:
        args.append(w_in)
        in_specs.append(whole(w_in.shape))
    if has_out or not has_in:
        out_specs.append(rows(d))
        out_shape.append(jax.ShapeDtypeStruct((m, d), F32))
    if has_in:
        out_specs += [rows(NB_COLS), rows(NF_COLS)]
        out_shape += [jax.ShapeDtypeStruct((m, NB_COLS), BF16),
                      jax.ShapeDtypeStruct((m, NF_COLS), F32)]
    return pl.pallas_call(
        functools.partial(_proj_kernel, has_out=has_out, has_in=has_in),
        grid=(m // tm,),
        in_specs=in_specs,
        out_specs=out_specs,
        out_shape=out_shape,
        compiler_params=pltpu.CompilerParams(
            dimension_semantics=("arbitrary",), vmem_limit_bytes=VMEM_LIMIT),
        name="proj_" + ("out" if has_out else "") + ("in" if has_in else "norm"),
    )(*args)


def _pack_w_in(w):
    sizes = (FOX_W, FOX_W, FOX_W, FOX_HEADS, FOX_W, SB_W, SB_W, SB_W, SB_W,
             NSA_W, HEAD_DIM, HEAD_DIM, HEAD_DIM, HEAD_DIM, HEAD_DIM, HEAD_DIM,
             3 * NSA_HEADS, NSA_W)
    offs = np.concatenate([[0], np.cumsum(sizes)])
    w = w.astype(BF16)
    (fq, fk, fv, ff, fz, sq, sk, sv, sz, nq, nkc, nvc, nks, nvs, nkw, nvw, ng, nz) = [
        w[:, offs[i]:offs[i + 1]] for i in range(len(sizes))]
    zeros = jnp.zeros((w.shape[0], LANES), w.dtype)
    cols = [fq * QK_SCALE, fk, fv, sq * QK_SCALE, sk, sv, nq * QK_SCALE,
            nks, nvs, nkw, nvw,
            fz, sz, nz, nkc, nvc, _misc_ff_layout(ff, zeros), ng,
            zeros[:, :LANES - MISC_NG - 3 * NSA_HEADS]]
    return jnp.concatenate(cols, axis=1)


def _misc_ff_layout(ff, zeros):
    cols = []
    for p in range(FOX_HEADS // 2):
        cols += [ff[:, 2 * p:2 * p + 1]] * MISC_FF_COPIES + [ff[:, 2 * p + 1:2 * p + 2]] * MISC_FF_COPIES
        cols.append(zeros[:, :MISC_PAIR_LANES - 2 * MISC_FF_COPIES])
    return jnp.concatenate(cols, axis=1)


def _fox_kernel(q_ref, k_ref, v_ref, z_ref, misc_ref, bf_ref, tri_ref,
                o_ref, qq_ref, kk_ref, vpt_ref, cps_ref, *, seq, blk):
    pair = pl.program_id(1)
    low = lax.broadcasted_iota(jnp.int32, (blk, LANES), 1) < HEAD_DIM
    lane_b = lax.broadcasted_iota(jnp.int32, (blk, LANES), 1)
    nblk = seq // blk

    @pl.when(pair == 0)
    def _():
        tri = tri_ref[...]
        part = (lane_b % MISC_PAIR_LANES) % 3
        carry = jnp.zeros((1, LANES), F32)
        for b in range(nblk):
            rows = slice(b * blk, (b + 1) * blk)
            ls = _log_sigmoid(misc_ref[rows, :] + bf_ref[...])
            c3 = _dot(tri, jnp.concatenate(_split3(ls), axis=1))
            cb = c3[:, 0:LANES] + c3[:, LANES:2 * LANES] + c3[:, 2 * LANES:3 * LANES] + carry
            carry = cb[blk - 1:blk, :]
            hi, mid, lo = _split3(cb)
            cps_ref[rows, :] = jnp.where(part == 0, hi.astype(F32), jnp.where(
                part == 1, mid.astype(F32), lo.astype(F32))).astype(BF16)

    rel = lane_b - MISC_PAIR_LANES * pair
    third = jnp.where(rel < 0, -1, jnp.where(rel < 3, 0, jnp.where(rel < 6, 1, jnp.where(
        rel < 9, 2, jnp.where(rel < 12, 3, -1)))))

    def lanes_of(*thirds):
        sel = jnp.zeros((blk, LANES), F32)
        for t in thirds:
            sel = jnp.where(third == t, 1.0, sel)
        return sel.astype(BF16)

    k_ones, k_parts = lanes_of(0, 2), lanes_of(1, 3)
    qa_parts, qa_ones = lanes_of(0), lanes_of(1)
    qb_parts, qb_ones = lanes_of(2), lanes_of(3)
    eye = jnp.where(lax.broadcasted_iota(jnp.int32, (LANES, LANES), 0)
                    == lax.broadcasted_iota(jnp.int32, (LANES, LANES), 1), 1.0, 0.0).astype(BF16)
    for b in range(nblk):
        rows = slice(b * blk, (b + 1) * blk)
        cps = cps_ref[rows, :]
        q2 = q_ref[rows, :]
        v2 = v_ref[rows, :]
        zero = jnp.zeros_like(q2)
        one = jnp.ones_like(q2)
        kk_ref[rows, 0:LANES] = k_ref[rows, :]
        kk_ref[rows, LANES:2 * LANES] = k_ones - cps * k_parts
        qq_ref[0, rows, 0:LANES] = jnp.where(low, q2, zero)
        qq_ref[0, rows, LANES:2 * LANES] = cps * qa_parts + qa_ones
        qq_ref[1, rows, 0:LANES] = jnp.where(low, zero, q2)
        qq_ref[1, rows, LANES:2 * LANES] = cps * qb_parts + qb_ones
        vt = _dot_nt(eye, v2).astype(BF16)
        vpt_ref[0, 0:HEAD_DIM, rows] = vt[0:HEAD_DIM]
        vpt_ref[1, 0:HEAD_DIM, rows] = vt[HEAD_DIM:LANES]
        vpt_ref[0, HEAD_DIM:VT_ROWS, rows] = jnp.ones((VT_ROWS - HEAD_DIM, blk), BF16)
        vpt_ref[1, HEAD_DIM:VT_ROWS, rows] = jnp.ones((VT_ROWS - HEAD_DIM, blk), BF16)

    causal_t = (lax.broadcasted_iota(jnp.int32, (blk, blk), 0)
                <= lax.broadcasted_iota(jnp.int32, (blk, blk), 1))

    def scores(i, h):
        r0 = i * blk
        rows = slice(r0, r0 + blk)
        qa = qq_ref[h, rows, :]
        sd = jnp.where(causal_t, _dot_nt(kk_ref[rows, :], qa), -jnp.inf)
        m = jnp.max(sd, axis=0, keepdims=True)
        sm = None
        if i > 0:
            sm = _dot_nt(kk_ref[0:r0, :], qa)
            m = jnp.maximum(m, jnp.max(sm, axis=0, keepdims=True))
        return sd, sm, m

    outs = {}

    def weighted_sum(i, h, st):
        sd, sm, m = st
        rows = slice(i * blk, (i + 1) * blk)
        p = jnp.exp(sd - m).astype(BF16)
        if i > 0:
            p = jnp.concatenate([jnp.exp(sm - m).astype(BF16), p], axis=0)
        a = _dot(vpt_ref[h, :, 0:(i + 1) * blk], p)
        outs[i, h] = a[0:HEAD_DIM] / a[HEAD_DIM:HEAD_DIM + 1]
        if h == 1:
            o = jnp.concatenate([outs.pop((i, 0)), outs.pop((i, 1))], axis=0).T
            z = z_ref[rows, :]
            o_ref[rows, :] = (o * (z * _sigmoid(z))).astype(BF16)

    _emit_pipelined([(i, h) for i in range(nblk) for h in range(2)], scores, weighted_sum)


def _fox(pb3, pf3, b_f, blk=256):
    bsz, seq, _ = pb3.shape
    npair = FOX_HEADS // 2
    bias = jnp.zeros((1, LANES), F32).at[:, :MISC_NG].set(
        _misc_ff_layout(b_f.reshape(1, FOX_HEADS), jnp.zeros((1, LANES), F32)))
    tri = jnp.asarray(np.tril(np.ones((blk, blk), np.float32)), BF16)

    def col(base):
        return pl.BlockSpec((None, seq, LANES), lambda b, p: (b, 0, base + p))

    def whole(shape):
        return pl.BlockSpec(shape, lambda b, p: (0,) * len(shape))

    return pl.pallas_call(
        functools.partial(_fox_kernel, seq=seq, blk=blk),
        grid=(bsz, npair),
        in_specs=[
            col(PB_FQ), col(PB_FK), col(PB_FV), col(PF_FZ),
            pl.BlockSpec((None, seq, LANES), lambda b, p: (b, 0, PF_MISC)),
            whole((1, LANES)), whole((blk, blk)),
        ],
        out_specs=pl.BlockSpec((None, seq, LANES), lambda b, p: (b, 0, p)),
        out_shape=jax.ShapeDtypeStruct((bsz, seq, FOX_W), BF16),
        scratch_shapes=[
            pltpu.VMEM((2, seq, 2 * LANES), BF16),
            pltpu.VMEM((seq, 2 * LANES), BF16),
            pltpu.VMEM((2, VT_ROWS, seq), BF16),
            pltpu.VMEM((seq, LANES), BF16),
        ],
        compiler_params=pltpu.CompilerParams(
            dimension_semantics=("arbitrary", "arbitrary"), vmem_limit_bytes=VMEM_LIMIT),
        name="fox_attn",
    )(pb3, pb3, pb3, pf3, pf3, bias, tri)


SB_NEAR_TILES = 2
SB_DEAD = -104.0


def _sb_kernel(q_ref, k_ref, v_ref, z_ref, suf_ref, o_ref, acc_ref, carry_ref, *, seq, blk):
    low = lax.broadcasted_iota(jnp.int32, (blk, LANES), 1) < HEAD_DIM
    strict = (lax.broadcasted_iota(jnp.int32, (blk, blk), 1)
              < lax.broadcasted_iota(jnp.int32, (blk, blk), 0))
    suf = suf_ref[...]

    def log_terms(qh, j_hi, j_lo):
        z = _dot_nt(qh, k_ref[j_lo * blk:(j_hi + 1) * blk, :])
        lsz = jnp.minimum(z, 0.0) - jnp.log(1.0 + jnp.exp(-jnp.abs(z)))
        return lsz, lsz - z

    def tiles(i, j_hi, j_lo, carry, terms):
        lsz, l1m = terms
        k0 = j_lo * blk
        out = None
        for j in range(j_hi, j_lo - 1, -1):
            cs = slice(j * blk - k0, (j + 1) * blk - k0)
            l1 = l1m[:, cs]
            if j == i:
                l1 = jnp.where(strict, l1, 0.0)
            r = _dot(l1.astype(BF16), suf)
            a = jnp.exp(lsz[:, cs] + (r + carry))
            if j == i:
                a = jnp.where(strict, a, 0.0)
            pv = _dot(a.astype(BF16), v_ref[j * blk:(j + 1) * blk, :])
            out = pv if out is None else out + pv
            carry = carry + (r[:, 0:1] + l1[:, 0:1])
        return out, carry

    nblk = seq // blk

    def q_head(i, h):
        q2 = q_ref[i * blk:(i + 1) * blk, :]
        zero = jnp.zeros_like(q2)
        return jnp.where(low, q2, zero) if h == 0 else jnp.where(low, zero, q2)

    def j_near(i):
        return max(i - SB_NEAR_TILES + 1, 0)

    alive = {}

    def near_terms(i, h):
        return log_terms(q_head(i, h), i, j_near(i))

    def near_tiles(i, h, terms):
        out, carry = tiles(i, i, j_near(i), jnp.zeros((blk, 1), F32), terms)
        acc_ref[i, h] = out
        if j_near(i) > 0:
            carry_ref[i, h] = jnp.broadcast_to(carry, (blk, LANES))
            alive[i, h] = jnp.max(carry) >= SB_DEAD

    _emit_pipelined([(i, h) for i in range(nblk) for h in range(2)], near_terms, near_tiles)

    for (i, h), flag in alive.items():
        @pl.when(flag)
        def _(i=i, h=h):
            far, _ = tiles(i, j_near(i) - 1, 0, carry_ref[i, h][:, 0:1],
                           log_terms(q_head(i, h), j_near(i) - 1, 0))
            acc_ref[i, h] = acc_ref[i, h] + far

    for i in range(nblk):
        rows = slice(i * blk, (i + 1) * blk)
        o = jnp.where(low, acc_ref[i, 0], acc_ref[i, 1])
        zg = z_ref[rows, :]
        o_ref[rows, :] = (o * (zg * _sigmoid(zg))).astype(BF16)


def _sb(pb3, pf3, blk=256):
    bsz, seq, _ = pb3.shape
    npair = SB_HEADS // 2
    suf = jnp.asarray(np.tril(np.ones((blk, blk), np.float32), -1), BF16)

    def col(base):
        return pl.BlockSpec((None, seq, LANES), lambda b, p: (b, 0, base + p))

    return pl.pallas_call(
        functools.partial(_sb_kernel, seq=seq, blk=blk),
        grid=(bsz, npair),
        in_specs=[
            col(PB_SQ), col(PB_SK), col(PB_SV), col(PF_SZ),
            pl.BlockSpec(suf.shape, lambda b, p: (0, 0)),
        ],
        out_specs=pl.BlockSpec((None, seq, LANES), lambda b, p: (b, 0, p)),
        out_shape=jax.ShapeDtypeStruct((bsz, seq, SB_W), BF16),
        scratch_shapes=[pltpu.VMEM((seq // blk, 2, blk, LANES), F32),
                        pltpu.VMEM((seq // blk, 2, blk, LANES), F32)],
        compiler_params=pltpu.CompilerParams(
            dimension_semantics=("arbitrary", "arbitrary"), vmem_limit_bytes=VMEM_LIMIT),
        name="sb_attn",
    )(pb3, pb3, pb3, pf3, suf)


NSA_HEAD_ORDER = (0, 1, 2, 3)
SEL_LANES = 32
POS_HI_LANE = 32
POS_LO_LANE = 33
NSA_SLOPES = tuple(2.0 ** (-8.0 * (g + 1) / NSA_HEADS) for g in range(NSA_HEADS))


def _nsa_kernel(q_ref, ksv_ref, kwv_ref, z_ref, misc_ref, kvc_ref,
                pos_ref, w1_ref, w2k_ref, w2v_ref,
                mselt_ref, kaugs_ref, kaugw_ref, qslope_ref,
                o_ref, kc_ref, vct_ref, kks_ref, kkw_ref, vst_ref, vwt_ref, *, seq, blk):
    cols4 = NSA_HEADS * blk
    n_cmp = (seq - CMP_BLOCK) // CMP_STRIDE + 1
    n_sel = seq // SEL_BLOCK
    nwin = WINDOW // blk
    eye = jnp.where(lax.broadcasted_iota(jnp.int32, (LANES, LANES), 0)
                    == lax.broadcasted_iota(jnp.int32, (LANES, LANES), 1), 1.0, 0.0).astype(BF16)

    nchunk = seq // CMP_STRIDE
    ha = jnp.zeros((nchunk, LANES), F32)
    hb = jnp.zeros((nchunk, LANES), F32)
    for p in range(CMP_STRIDE):
        xp = kvc_ref[pl.ds(p, nchunk, stride=CMP_STRIDE), :]
        q = p + CMP_STRIDE
        ha = ha + _dot((xp + pos_ref[p:p + 1, :]).astype(BF16), w1_ref[p])
        hb = hb + _dot((xp + pos_ref[q:q + 1, :]).astype(BF16), w1_ref[q])
    hid = ha + pltpu.roll(hb, nchunk - 1, axis=0)
    hid = (hid * _sigmoid(hid)).astype(BF16)
    kc_ref[...] = _dot(hid, w2k_ref[...]).astype(BF16)
    vct_ref[...] = _dot_nt(eye, _dot(hid, w2v_ref[...]).astype(BF16)).astype(BF16)

    tb = 2 * LANES
    lane_t = lax.broadcasted_iota(jnp.int32, (tb, LANES), 1)
    low_t = lane_t < HEAD_DIM
    er = lax.broadcasted_iota(jnp.int32, (LANES, LANES), 0)
    ec = lax.broadcasted_iota(jnp.int32, (LANES, LANES), 1)
    dup_lo = jnp.where(er == ec % HEAD_DIM, 1.0, 0.0).astype(BF16)
    swap = jnp.where(ec == (er + HEAD_DIM) % LANES, 1.0, 0.0).astype(BF16)
    kks_ref[:, LANES:2 * LANES] = kaugs_ref[...]
    kkw_ref[:, LANES:2 * LANES] = kaugw_ref[...]
    for b in range(seq // tb):
        rows = slice(b * tb, (b + 1) * tb)
        xs = ksv_ref[rows, :]
        xw = kwv_ref[rows, :]
        one = jnp.ones_like(xs)
        kks_ref[rows, 0:LANES] = _dot(xs, dup_lo).astype(BF16)
        kkw_ref[rows, 0:LANES] = _dot(xw, dup_lo).astype(BF16)
        vst_ref[:, rows] = _dot_nt(swap, jnp.where(low_t, one, xs)).astype(BF16)[0:VT_ROWS]
        vwt_ref[:, rows] = _dot_nt(swap, jnp.where(low_t, one, xw)).astype(BF16)[0:VT_ROWS]

    low = lax.broadcasted_iota(jnp.int32, (blk, LANES), 1) < HEAD_DIM
    colt = lax.broadcasted_iota(jnp.int32, (1, cols4), 1)
    grp = colt // blk
    tok = colt - grp * blk
    gslope = [NSA_SLOPES[g] for g in NSA_HEAD_ORDER]
    slope = jnp.where(grp == 0, gslope[0], jnp.where(grp == 1, gslope[1],
                      jnp.where(grp == 2, gslope[2], gslope[3]))).astype(F32)
    keyr = lax.broadcasted_iota(jnp.int32, (blk, 1), 0)
    causal_t = keyr <= tok
    after_t = keyr > tok
    cmpr = lax.broadcasted_iota(jnp.int32, (LANES, 1), 0)
    cmp_end = (cmpr * CMP_STRIDE + (CMP_BLOCK - 1)).astype(F32)
    jrow = lax.broadcasted_iota(jnp.int32, (n_sel, blk), 0)
    tcol = lax.broadcasted_iota(jnp.int32, (n_sel, blk), 1)
    sel_lane = lax.broadcasted_iota(jnp.int32, (cols4, LANES), 1) < SEL_LANES
    qslope = qslope_ref[...]

    def col_max(parts):
        m = None
        for s in parts:
            mp = jnp.max(s, axis=0, keepdims=True)
            m = mp if m is None else jnp.maximum(m, mp)
        return m

    def softmax_pv(parts, m, keys, vt_ref):
        p = [jnp.exp(s - m).astype(BF16) for s in parts]
        p = p[0] if len(p) == 1 else jnp.concatenate(p, axis=0)
        acc = _dot(vt_ref[:, keys], p)
        return acc[0:HEAD_DIM] / acc[HEAD_DIM:HEAD_DIM + 1]

    def scores(i):
        qs = i * blk
        rows = slice(qs, qs + blk)
        q01 = q_ref[rows, 0:LANES]
        q23 = q_ref[rows, LANES:2 * LANES]
        zero = jnp.zeros_like(q01)
        qst = jnp.concatenate([jnp.where(low, q01, zero), jnp.where(low, zero, q01),
                               jnp.where(low, q23, zero), jnp.where(low, zero, q23)], axis=0)

        dist_c = (qs + tok).astype(F32) - cmp_end
        valid_c = (dist_c >= 0.0) & (cmpr < n_cmp)
        sc = _dot_nt(kc_ref[...], qst) - slope * dist_c
        sc = jnp.where(valid_c, sc, -jnp.inf)
        mc = jnp.max(sc, axis=0, keepdims=True)
        mc = jnp.where(mc == -jnp.inf, 0.0, mc)
        pc = jnp.exp(sc - mc)
        ssum = jnp.sum(pc, axis=0, keepdims=True)
        pc = pc / jnp.where(ssum > 0.0, ssum, 1.0)
        oc = _dot(vct_ref[0:HEAD_DIM, :], pc.astype(BF16))

        pcs = (pc[:, 0:blk] + pc[:, blk:2 * blk] + pc[:, 2 * blk:3 * blk]
               + pc[:, 3 * blk:4 * blk])
        hi, lo = _split2(pcs)
        imp = _dot(mselt_ref[...], jnp.concatenate([hi, lo], axis=0))
        back = (qs + tcol) // SEL_BLOCK - jrow
        imp = jnp.where(back < 0, -jnp.inf,
                        jnp.where(back < SEL_N_LOCAL, jnp.inf,
                                  jnp.where(jrow == 0, jnp.inf, imp)))
        rank = jnp.zeros((n_sel, blk), F32)
        for c in range(n_sel):
            rowc = imp[c:c + 1, :]
            tie = jnp.where(jrow > c, 1.0, 0.0)
            rank = rank + jnp.where(rowc > imp, 1.0, jnp.where(rowc == imp, tie, 0.0))
        unsel_t = jnp.where(rank >= float(SEL_TOPK), 1.0, 0.0).astype(BF16)
        unsel_t = jnp.concatenate([unsel_t, jnp.zeros((LANES - n_sel, blk), BF16)], axis=0)
        unsel = _dot_nt(eye, unsel_t).astype(BF16)

        qop_w = jnp.concatenate([qst, qslope], axis=1)
        qaug = jnp.where(sel_lane, jnp.concatenate([unsel] * NSA_HEADS, axis=0), qslope)
        qop_s = jnp.concatenate([qst, qaug], axis=1)

        win = []
        w0 = max(qs - WINDOW, 0)
        if i >= nwin:
            win.append(jnp.where(after_t, _dot_nt(kkw_ref[w0:w0 + blk, :], qop_w), -jnp.inf))
        wm = max(qs - WINDOW + blk, 0)
        if qs > wm:
            win.append(_dot_nt(kkw_ref[wm:qs, :], qop_w))
        win.append(jnp.where(causal_t, _dot_nt(kkw_ref[rows, :], qop_w), -jnp.inf))

        sel = []
        if i > 0:
            sel.append(_dot_nt(kks_ref[0:qs, :], qop_s))
        sel.append(jnp.where(causal_t, _dot_nt(kks_ref[rows, :], qop_s), -jnp.inf))
        return oc, win, col_max(win), sel, col_max(sel)

    def outputs(i, st):
        oc, win, m_win, sel, m_sel = st
        qs = i * blk
        rows = slice(qs, qs + blk)
        o_win = softmax_pv(win, m_win, slice(max(qs - WINDOW, 0), qs + blk), vwt_ref)
        o_sel = softmax_pv(sel, m_sel, slice(0, qs + blk), vst_ref)

        sg_t = _sigmoid(misc_ref[rows, :]).T

        def gate(branch):
            return jnp.concatenate(
                [sg_t[MISC_NG + 3 * g + branch:MISC_NG + 3 * g + branch + 1, :]
                 for g in NSA_HEAD_ORDER], axis=1)

        tot = oc * gate(0) + o_sel * gate(1) + o_win * gate(2)
        by_head = {g: tot[:, k * blk:(k + 1) * blk] for k, g in enumerate(NSA_HEAD_ORDER)}
        out = jnp.concatenate([by_head[g] for g in range(NSA_HEADS)], axis=0).T
        z = z_ref[rows, :]
        o_ref[rows, :] = (out * (z * _sigmoid(z))).astype(BF16)

    _emit_pipelined([(i,) for i in range(seq // blk)], scores, outputs, depth=2)


def _nsa_constants(seq, blk):
    n_cmp = (seq - CMP_BLOCK) // CMP_STRIDE + 1
    n_sel = seq // SEL_BLOCK
    cs = np.arange(n_cmp) * CMP_STRIDE
    ce = cs + CMP_BLOCK - 1
    ss = np.arange(n_sel) * SEL_BLOCK
    se = ss + SEL_BLOCK - 1
    msel_t = np.zeros((n_sel, LANES), np.float32)
    msel_t[:, :n_cmp] = ((cs[:, None] <= se[None, :]) & (ce[:, None] >= ss[None, :])).T
    msel_t = np.concatenate([msel_t, msel_t], axis=1)
    key = np.arange(seq)
    kaug_w = np.zeros((seq, LANES), np.float32)
    kaug_w[:, POS_HI_LANE] = key // 16
    kaug_w[:, POS_LO_LANE] = key % 16
    kaug_s = kaug_w.copy()
    kaug_s[key, key // SEL_BLOCK] = PEN
    qslope = np.zeros((NSA_HEADS * blk, LANES), np.float32)
    for k, g in enumerate(NSA_HEAD_ORDER):
        qslope[k * blk:(k + 1) * blk, POS_HI_LANE] = 16.0 * NSA_SLOPES[g]
        qslope[k * blk:(k + 1) * blk, POS_LO_LANE] = NSA_SLOPES[g]
    return (jnp.asarray(msel_t, BF16), jnp.asarray(kaug_s, BF16), jnp.asarray(kaug_w, BF16),
            jnp.asarray(qslope, BF16))


def _nsa(pb3, pf3, pos_k, w1_k, w2_k, pos_v, w1_v, w2_v, blk=128):
    bsz, seq, _ = pb3.shape
    nchunk = seq // CMP_STRIDE
    w1k = w1_k.astype(BF16).reshape(CMP_BLOCK, HEAD_DIM, HEAD_DIM)
    w1v = w1_v.astype(BF16).reshape(CMP_BLOCK, HEAD_DIM, HEAD_DIM)
    w1z = jnp.zeros_like(w1k)
    w1 = jnp.concatenate([jnp.concatenate([w1k, w1z], axis=2),
                          jnp.concatenate([w1z, w1v], axis=2)], axis=1)
    pos = jnp.concatenate([pos_k, pos_v], axis=1)
    zero = jnp.zeros((LANES, LANES), F32)
    w2k = zero.at[:HEAD_DIM, :].set(jnp.concatenate([w2_k, w2_k], axis=1))
    w2v = zero.at[HEAD_DIM:, :].set(jnp.concatenate([w2_v, w2_v], axis=1))
    consts = _nsa_constants(seq, blk)

    def col(base, nblk=1):
        return pl.BlockSpec((None, seq, nblk * LANES), lambda b: (b, 0, base // nblk))

    def whole(shape):
        return pl.BlockSpec(shape, lambda b: (0,) * len(shape))

    return pl.pallas_call(
        functools.partial(_nsa_kernel, seq=seq, blk=blk),
        grid=(bsz,),
        in_specs=[
            col(PB_NQ, 2), col(PB_NSEL), col(PB_NWIN),
            col(PF_NZ, 2), col(PF_MISC), col(PF_KVC),
            whole(pos.shape), whole(w1.shape), whole(w2k.shape), whole(w2v.shape),
        ] + [whole(c.shape) for c in consts],
        out_specs=pl.BlockSpec((None, seq, NSA_W), lambda b: (b, 0, 0)),
        out_shape=jax.ShapeDtypeStruct((bsz, seq, NSA_W), BF16),
        scratch_shapes=[
            pltpu.VMEM((nchunk, LANES), BF16),
            pltpu.VMEM((LANES, nchunk), BF16),
            pltpu.VMEM((seq, 2 * LANES), BF16),
            pltpu.VMEM((seq, 2 * LANES), BF16),
            pltpu.VMEM((VT_ROWS, seq), BF16),
            pltpu.VMEM((VT_ROWS, seq), BF16),
        ],
        compiler_params=pltpu.CompilerParams(
            dimension_semantics=("arbitrary",), vmem_limit_bytes=VMEM_LIMIT),
        name="nsa_attn",
    )(pb3, pb3, pb3, pf3, pf3, pf3, pos, w1, w2k.astype(BF16), w2v.astype(BF16), *consts)


def kernel(x, norm_g, w_in, b_f, cmp_pos_k, cmp_w1_k, cmp_w2_k,
           cmp_pos_v, cmp_w1_v, cmp_w2_v, w_out, final_g):
    bsz, seq, d = x.shape
    xf = x.reshape(bsz * seq, d)
    attn = None
    for l in range(DEPTH):
        res = _proj(xf, norm_g[l], attn, None if l == 0 else w_out[l - 1].astype(BF16),
                    _pack_w_in(w_in[l]))
        if l > 0:
            xf = res[0]
        pb, pf = res[-2:]
        pb3 = pb.reshape(bsz, seq, NB_COLS)
        pf3 = pf.reshape(bsz, seq, NF_COLS)
        o_fox = _fox(pb3, pf3, b_f[l])
        o_sb = _sb(pb3, pf3)
        o_nsa = _nsa(pb3, pf3, cmp_pos_k[l], cmp_w1_k[l], cmp_w2_k[l],
                     cmp_pos_v[l], cmp_w1_v[l], cmp_w2_v[l])
        attn = (o_fox.reshape(bsz * seq, FOX_W), o_sb.reshape(bsz * seq, SB_W),
                o_nsa.reshape(bsz * seq, NSA_W))
    out, = _proj(xf, final_g, attn, w_out[DEPTH - 1].astype(BF16))
    return out.reshape(bsz, seq, d)
```
